```python
import math
import jax, jax.numpy as jnp
from jax import lax
import numpy as np

D_MODEL = 1024
BATCH = 8
SEQ = 2048
DEPTH = 2
DEC_BATCH = 128
DEC_SEQ = 1
PAST_LEN = 2048
PAGE_SIZE = 128

PLE_DIM = 256
BRANCH_WIDTH = D_MODEL // 2
N_BRANCH = 3
M_HEADS = 4
M_HEAD_DIM = BRANCH_WIDTH // M_HEADS
M_WIDTH = BRANCH_WIDTH
M_CHUNK = 64
FORGET_BIAS = 3.0
CONV_WIDTH = BRANCH_WIDTH
CONV_K = 3
SB_HEADS = 4
SB_HEAD_DIM = BRANCH_WIDTH // SB_HEADS
SB_WIDTH = BRANCH_WIDTH
SB_BLOCK = 128
SB_BIAS_INIT = -6.0
FFN_DIM = 4 * D_MODEL
RMS_EPS = 1e-6
SPLIT_SIZES = (M_WIDTH, M_WIDTH, M_WIDTH, M_WIDTH, M_HEADS, M_HEADS,
               CONV_WIDTH, CONV_WIDTH, CONV_WIDTH,
               SB_WIDTH, SB_WIDTH, SB_WIDTH,
               N_BRANCH * D_MODEL)
N_IN = sum(SPLIT_SIZES)

kernel_name = 'hybrid_mlstm_shortconv_stickbreaking_step'


def rmsnorm(x, g):
    xf = x.astype(jnp.float32)
    y = xf * lax.rsqrt(jnp.mean(jnp.square(xf), axis=-1, keepdims=True) + RMS_EPS)
    return (y * g.astype(jnp.float32)).astype(x.dtype)


def mlstm_chunkwise(q, k, v, i_pre, log_f, C0, n0, m0):
    B, T, H, Dh = q.shape
    L = T if T <= M_CHUNK else math.gcd(T, M_CHUNK)
    nc = T // L

    def to_chunks(a):
        a = a.reshape((B, nc, L) + a.shape[2:])
        return jnp.moveaxis(jnp.moveaxis(a, 1, 0), 2, 3)

    causal = jnp.tril(jnp.ones((L, L), dtype=bool))

    def step(carry, xs):
        C, n, m = carry
        qc, kc, vc, ic, fc = xs
        b = jnp.cumsum(fc, axis=-1)
        d = jnp.where(causal, b[..., :, None] - b[..., None, :] + ic[..., None, :], -jnp.inf)
        inter = b + m[..., None]
        m_t = jnp.maximum(inter, jnp.max(d, axis=-1))
        w = jnp.exp(d - m_t[..., None]) * jnp.einsum('bhtd,bhsd->bhts', qc, kc)
        s = jnp.exp(inter - m_t)
        num = s[..., None] * jnp.einsum('bhtd,bhde->bhte', qc, C) + jnp.einsum('bhts,bhse->bhte', w, vc)
        den = s * jnp.einsum('bhtd,bhd->bht', qc, n) + jnp.sum(w, axis=-1)
        hc = num / jnp.maximum(jnp.abs(den), jnp.exp(-m_t))[..., None]
        m_new = m_t[..., -1]
        wk = jnp.exp(b[..., -1:] - b + ic - m_new[..., None])
        decay = jnp.exp(b[..., -1] + m - m_new)
        C = decay[..., None, None] * C + jnp.einsum('bhs,bhsd,bhse->bhde', wk, kc, vc)
        n = decay[..., None] * n + jnp.einsum('bhs,bhsd->bhd', wk, kc)
        return (C, n, m_new), hc

    ks = k * (Dh ** -0.5)
    xs = (to_chunks(q), to_chunks(ks), to_chunks(v), to_chunks(i_pre), to_chunks(log_f))
    (C, n, m), hs = lax.scan(step, (C0, n0, m0), xs)
    h = jnp.swapaxes(jnp.moveaxis(hs, 0, 1), 2, 3).reshape(B, T, H, Dh)
    return h, C, n, m


def stick_breaking(q, k, v, bias, q_pos, k_pos):
    z = (jnp.einsum('bqhd,bkhd->bhqk', q, k).astype(jnp.float32) * (SB_HEAD_DIM ** -0.5)
         + bias.astype(jnp.float32)[None, :, None, None])
    mask = k_pos[None, :] < q_pos[:, None]
    log_keep = jnp.where(mask, jax.nn.log_sigmoid(-z), 0.0)
    later = lax.cumsum(log_keep, axis=3, reverse=True) - log_keep
    a = jnp.where(mask, jnp.exp(jax.nn.log_sigmoid(z) + later), 0.0)
    return jnp.einsum('bhqk,bkhd->bqhd', a.astype(v.dtype), v)


def sb_prompt(q, k, v, bias):
    B, T, H, Dh = q.shape
    blk = SB_BLOCK if T % SB_BLOCK == 0 else T
    nb = T // blk
    pos = jnp.arange(T)
    qb = jnp.swapaxes(q.reshape(B, nb, blk, H, Dh), 0, 1)
    out = lax.map(lambda a: stick_breaking(a[0], k, v, bias, a[1], pos), (qb, pos.reshape(nb, blk)))
    return jnp.swapaxes(out, 0, 1).reshape(B, T, H, Dh)


def decoder_layer(h, p_l, w, mlstm0, conv0, past_kv):
    (g_mix, w_in, b_if, b_sb, conv_w, w_branch, w_out, g_mlp, w_up, w_down, g_ple, w_ple_gate, w_ple) = w
    B, T, _ = h.shape
    f32 = jnp.float32
    xn = rmsnorm(h, g_mix)
    proj = xn @ w_in
    split_points = np.cumsum(SPLIT_SIZES)[:-1].tolist()
    (mq, mk, mv, mo, mi, mf, cb, cc, cx, sq, sk, sv, gates) = jnp.split(proj, split_points, axis=-1)

    i_pre = (mi + b_if[:M_HEADS]).astype(f32)
    log_f = jax.nn.log_sigmoid((mf + b_if[M_HEADS:]).astype(f32))
    C0, n0, m0 = mlstm0
    hm, C, n, m = mlstm_chunkwise(
        mq.reshape(B, T, M_HEADS, M_HEAD_DIM).astype(f32),
        mk.reshape(B, T, M_HEADS, M_HEAD_DIM).astype(f32),
        mv.reshape(B, T, M_HEADS, M_HEAD_DIM).astype(f32),
        i_pre, log_f, C0.astype(f32), n0.astype(f32), m0.astype(f32))
    ym = jax.nn.sigmoid(mo) * hm.reshape(B, T, M_WIDTH).astype(h.dtype)

    xc = cc * cx
    full = jnp.concatenate([conv0.astype(xc.dtype), xc], axis=1)
    conv = sum(conv_w[j] * full[:, j:j + T] for j in range(CONV_K))
    yc = cb * conv
    new_buf = full[:, T:]

    q = sq.reshape(B, T, SB_HEADS, SB_HEAD_DIM)
    k = sk.reshape(B, T, SB_HEADS, SB_HEAD_DIM)
    v = sv.reshape(B, T, SB_HEADS, SB_HEAD_DIM)
    if past_kv is None:
        ys = sb_prompt(q, k, v, b_sb)
    else:
        pk, pv = past_kv
        P = pk.shape[1]
        keys = jnp.concatenate([pk.astype(k.dtype), k], axis=1)
        vals = jnp.concatenate([pv.astype(v.dtype), v], axis=1)
        ys = stick_breaking(q, keys, vals, b_sb, P + jnp.arange(T), jnp.arange(P + T))
    ys = ys.reshape(B, T, SB_WIDTH)

    branches = jnp.stack([ym, yc, ys], axis=2)
    pb = jnp.einsum('btnw,nwd->btnd', branches, w_branch)
    g = jax.nn.sigmoid(gates.reshape(B, T, N_BRANCH, D_MODEL))
    h = h + jnp.einsum('btnd,de->bte', g * pb, w_out)

    u = rmsnorm(h, g_mlp) @ w_up
    h = h + jnp.square(jax.nn.relu(u)) @ w_down

    h = h + (p_l @ w_ple) * jax.nn.sigmoid(rmsnorm(h, g_ple) @ w_ple_gate)
    return h, (k, v, C, n, m, new_buf)


def setup_inputs(seed: int = 0) -> dict:
    key = jax.random.key(seed)
    ks = jax.random.split(key, 32)
    nrm = jax.random.normal
    f32 = jnp.float32
    n_pages = PAST_LEN // PAGE_SIZE
    n_used = DEC_BATCH * n_pages
    n_phys = n_used + max(1, n_used // 4)
    page_table = jax.random.permutation(ks[0], n_phys)[:n_used].reshape(DEC_BATCH, n_pages).astype(jnp.int32)
    b_if = jnp.concatenate([0.1 * nrm(ks[1], (DEPTH, M_HEADS), f32),
                            FORGET_BIAS + 0.1 * nrm(ks[2], (DEPTH, M_HEADS), f32)], axis=-1)
    return {
        'x_prompt': nrm(ks[3], (BATCH, SEQ, D_MODEL), f32),
        'x_sample': nrm(ks[4], (DEC_BATCH, DEC_SEQ, D_MODEL), f32),
        'cache_k': nrm(ks[5], (DEPTH, n_phys, PAGE_SIZE, SB_HEADS, SB_HEAD_DIM), f32),
        'cache_v': nrm(ks[6], (DEPTH, n_phys, PAGE_SIZE, SB_HEADS, SB_HEAD_DIM), f32),
        'state_mlstm_C': 0.1 * nrm(ks[7], (DEPTH, DEC_BATCH, M_HEADS, M_HEAD_DIM, M_HEAD_DIM), f32),
        'state_mlstm_n': 0.5 * nrm(ks[8], (DEPTH, DEC_BATCH, M_HEADS, M_HEAD_DIM), f32),
        'state_mlstm_m': nrm(ks[9], (DEPTH, DEC_BATCH, M_HEADS), f32),
        'state_conv': nrm(ks[10], (DEPTH, DEC_BATCH, CONV_K - 1, CONV_WIDTH), f32),
        'page_table': page_table,
        'p_prompt': nrm(ks[11], (DEPTH, BATCH, SEQ, PLE_DIM), f32),
        'p_sample': nrm(ks[12], (DEPTH, DEC_BATCH, DEC_SEQ, PLE_DIM), f32),
        'g_mix': 1.0 + 0.05 * nrm(ks[13], (DEPTH, D_MODEL), f32),
        'w_in': nrm(ks[14], (DEPTH, D_MODEL, N_IN), f32) * D_MODEL ** -0.5,
        'b_if': b_if,
        'b_sb': SB_BIAS_INIT + 0.1 * nrm(ks[25], (DEPTH, SB_HEADS), f32),
        'conv_w': nrm(ks[15], (DEPTH, CONV_K, CONV_WIDTH), f32) * CONV_K ** -0.5,
        'w_branch': nrm(ks[16], (DEPTH, N_BRANCH, BRANCH_WIDTH, D_MODEL), f32) * BRANCH_WIDTH ** -0.5,
        'w_out': nrm(ks[17], (DEPTH, D_MODEL, D_MODEL), f32) * D_MODEL ** -0.5,
        'g_mlp': 1.0 + 0.05 * nrm(ks[18], (DEPTH, D_MODEL), f32),
        'w_up': nrm(ks[19], (DEPTH, D_MODEL, FFN_DIM), f32) * D_MODEL ** -0.5,
        'w_down': nrm(ks[20], (DEPTH, FFN_DIM, D_MODEL), f32) * FFN_DIM ** -0.5,
        'g_ple': 1.0 + 0.05 * nrm(ks[21], (DEPTH, D_MODEL), f32),
        'w_ple_gate': nrm(ks[22], (DEPTH, D_MODEL, D_MODEL), f32) * D_MODEL ** -0.5,
        'w_ple': nrm(ks[23], (DEPTH, PLE_DIM, D_MODEL), f32) * PLE_DIM ** -0.5,
        'g_final': 1.0 + 0.05 * nrm(ks[24], (D_MODEL,), f32),
    }


def reference(x_prompt, x_sample, cache_k, cache_v, state_mlstm_C, state_mlstm_n, state_mlstm_m,
              state_conv, page_table, p_prompt, p_sample, g_mix, w_in, b_if, b_sb, conv_w, w_branch,
              w_out, g_mlp, w_up, w_down, g_ple, w_ple_gate, w_ple, g_final):
    f32 = jnp.float32
    Bp = x_prompt.shape[0]
    Bs = x_sample.shape[0]
    past_len = page_table.shape[1] * cache_k.shape[2]
    mlstm_zero = (jnp.zeros((Bp, M_HEADS, M_HEAD_DIM, M_HEAD_DIM), f32),
                  jnp.zeros((Bp, M_HEADS, M_HEAD_DIM), f32),
                  jnp.zeros((Bp, M_HEADS), f32))
    conv_zero = jnp.zeros((Bp, CONV_K - 1, CONV_WIDTH), x_prompt.dtype)
    hp, hs = x_prompt, x_sample
    sp_all, ss_all = [], []
    for li in range(DEPTH):
        w = (g_mix[li], w_in[li], b_if[li], b_sb[li], conv_w[li], w_branch[li], w_out[li], g_mlp[li],
             w_up[li], w_down[li], g_ple[li], w_ple_gate[li], w_ple[li])
        hp, sp = decoder_layer(hp, p_prompt[li], w, mlstm_zero, conv_zero, None)
        pk = cache_k[li][page_table].reshape(Bs, past_len, SB_HEADS, SB_HEAD_DIM)
        pv = cache_v[li][page_table].reshape(Bs, past_len, SB_HEADS, SB_HEAD_DIM)
        hs, ss = decoder_layer(hs, p_sample[li], w,
                               (state_mlstm_C[li], state_mlstm_n[li], state_mlstm_m[li]),
                               state_conv[li], (pk, pv))
        sp_all.append(sp)
        ss_all.append(ss)
    y_prompt = rmsnorm(hp, g_final)
    y_sample = rmsnorm(hs, g_final)
    k_p, v_p, C_p, n_p, m_p, conv_p = [jnp.stack([s[j] for s in sp_all]) for j in range(6)]
    k_s, v_s, C_s, n_s, m_s, conv_s = [jnp.stack([s[j] for s in ss_all]) for j in range(6)]
    return (y_prompt, y_sample, k_p, v_p, C_p, n_p, m_p, conv_p, k_s, v_s, C_s, n_s, m_s, conv_s)
```

```python
import functools

import jax
import jax.numpy as jnp
from jax import lax
from jax.experimental import pallas as pl
from jax.experimental.pallas import tpu as pltpu

F32 = jnp.float32
BF16 = jnp.bfloat16

D_MODEL = 1024
WIDTH = 512
HEADS = 4
HEAD_DIM = 128
FFN_DIM = 4 * D_MODEL
PLE_DIM = 256
RMS_EPS = 1e-6
LANES = 128

C_PM = 0
C_PC = 2048
C_SQ = 3584
C_SK = 4096
C_SV = 4608
C_GT = 5120
C_GI = 8192
C_GF = 8320
N_PROJ = 8448

MLSTM_CHUNK = 256
SB_BLOCK = 256
VMEM_LIMIT = 56 * 1024 * 1024


def _softplus(x):
    return jnp.maximum(x, 0.0) + jnp.log1p(jnp.exp(-jnp.abs(x)))


def _rms(x, g):
    return x * lax.rsqrt(jnp.mean(x * x, axis=-1, keepdims=True) + RMS_EPS) * g


def _resident(shape):
    nd = len(shape)
    return pl.BlockSpec(shape, lambda *_: (0,) * nd, pipeline_mode=pl.Buffered(1))


def _proj_kernel(x_ref, g_ref, w_ref, pm_ref, pc_ref, sq_ref, kf_ref, vf_ref, kvb_ref, gt_ref,
                 gi_ref, gf_ref):
    xn = _rms(x_ref[...], g_ref[...]).astype(BF16)

    def mm(c0, width):
        return jnp.dot(xn, w_ref[:, c0:c0 + width], preferred_element_type=F32)

    scale = HEAD_DIM ** -0.5
    pm_ref[:, 0:512] = mm(C_PM, 512).astype(BF16)
    pm_ref[:, 512:1024] = (mm(C_PM + 512, 512) * scale).astype(BF16)
    pm_ref[:, 1024:1536] = mm(C_PM + 1024, 512).astype(BF16)
    pm_ref[:, 1536:2048] = mm(C_PM + 1536, 512).astype(BF16)
    for j in range(3):
        pc_ref[:, j * 512:(j + 1) * 512] = mm(C_PC + j * 512, 512).astype(BF16)
    sq_ref[...] = (mm(C_SQ, 512) * scale).astype(BF16)
    k = mm(C_SK, 512)
    v = mm(C_SV, 512)
    kf_ref[...] = k
    vf_ref[...] = v
    kvb_ref[:, 0:512] = k.astype(BF16)
    kvb_ref[:, 512:1024] = v.astype(BF16)
    for j in range(6):
        gt_ref[:, j * 512:(j + 1) * 512] = mm(C_GT + j * 512, 512).astype(BF16)
    gi_ref[...] = mm(C_GI, LANES)
    gf_ref[...] = mm(C_GF, LANES)


def _proj(x2d, g, wp, tm):
    m = x2d.shape[0]
    row = lambda width: pl.BlockSpec((tm, width), lambda i: (i, 0))
    widths = (2048, 1536, 512, 512, 512, 1024, 3072, LANES, LANES)
    dtypes = (BF16, BF16, BF16, F32, F32, BF16, BF16, F32, F32)
    return pl.pallas_call(
        _proj_kernel,
        grid=(m // tm,),
        in_specs=[row(D_MODEL), _resident((1, D_MODEL)), _resident((D_MODEL, N_PROJ))],
        out_specs=[row(w) for w in widths],
        out_shape=[jax.ShapeDtypeStruct((m, w), d) for w, d in zip(widths, dtypes)],
        compiler_params=pltpu.CompilerParams(dimension_semantics=("parallel",),
                                             vmem_limit_bytes=VMEM_LIMIT),
        name="proj",
    )(x2d, g, wp)


def _scan_rows(x, op, fill):
    n = x.shape[0]
    row = lax.broadcasted_iota(jnp.int32, x.shape, 0)
    k = 1
    while k < n:
        shifted = pltpu.roll(x, k, axis=0)
        x = op(x, jnp.where(row >= k, shifted, fill(x)))
        k *= 2
    return x


def _mlstm_conv_kernel(pm_ref, pc_ref, gi_ref, gf_ref, bi_ref, bf_ref, cw_ref,
                       ym_ref, yc_ref, s_ref, m_ref, cs_ref):
    c = pl.program_id(1)
    L = pm_ref.shape[0]

    @pl.when(c == 0)
    def _():
        s_ref[...] = jnp.zeros_like(s_ref)
        m_ref[...] = jnp.zeros_like(m_ref)
        cs_ref[...] = jnp.zeros_like(cs_ref)

    cb = pc_ref[:, 0:512].astype(F32)
    xc = pc_ref[:, 512:1024].astype(F32) * pc_ref[:, 1024:1536].astype(F32)
    prev = cs_ref[...]
    row = lax.broadcasted_iota(jnp.int32, xc.shape, 0)
    xc1 = jnp.where(row == 0, prev[1:2, :], pltpu.roll(xc, 1, axis=0))
    xc2 = jnp.where(row == 0, prev[0:1, :], jnp.where(row == 1, prev[1:2, :], pltpu.roll(xc, 2, axis=0)))
    cw = cw_ref[...]
    yc_ref[...] = (cb * (cw[0:1, :] * xc2 + cw[1:2, :] * xc1 + cw[2:3, :] * xc)).astype(BF16)
    cs_ref[...] = xc[L - 2:L, :]

    m_prev = m_ref[...]
    ig = gi_ref[...] + bi_ref[...]
    logf = -_softplus(-(gf_ref[...] + bf_ref[...]))
    b = _scan_rows(logf, jnp.add, jnp.zeros_like)
    a = ig - b
    mx = jnp.maximum(m_prev, _scan_rows(a, jnp.maximum, lambda x: x))
    m_t = b + mx
    a_t = a.T
    s_in = jnp.exp(m_prev - mx)
    floor = jnp.exp(-m_t)
    mx_last = mx[L - 1:L, :]
    wk = jnp.exp(a - mx_last)
    decay = jnp.exp(m_prev - mx_last)
    m_ref[...] = m_t[L - 1:L, :]

    r_i = lax.broadcasted_iota(jnp.int32, (L, L), 0)
    c_i = lax.broadcasted_iota(jnp.int32, (L, L), 1)
    causal = c_i <= r_i
    one_col = (lax.broadcasted_iota(jnp.int32, (L, HEAD_DIM), 1) == 0).astype(BF16)
    for h in range(HEADS):
        sl = slice(h * HEAD_DIM, (h + 1) * HEAD_DIM)
        q = pm_ref[:, sl]
        k = pm_ref[:, 512 + h * HEAD_DIM:512 + (h + 1) * HEAD_DIM]
        v = pm_ref[:, 1024 + h * HEAD_DIM:1024 + (h + 1) * HEAD_DIM]
        o = pm_ref[:, 1536 + h * HEAD_DIM:1536 + (h + 1) * HEAD_DIM]
        vext = jnp.concatenate([v, one_col], axis=1)
        qk = lax.dot_general(q, k, (((1,), (1,)), ((), ())), preferred_element_type=F32)
        e = jnp.exp(a_t[h:h + 1, :] - mx[:, h:h + 1])
        w = jnp.where(causal, e, 0.0) * qk
        intra = jnp.dot(w.astype(BF16), vext, preferred_element_type=F32)
        state = s_ref[h]
        inter = jnp.dot(q, state.astype(BF16), preferred_element_type=F32)
        sc = s_in[:, h:h + 1]
        num = sc * inter[:, :HEAD_DIM] + intra[:, :HEAD_DIM]
        den = sc * inter[:, HEAD_DIM:HEAD_DIM + 1] + intra[:, HEAD_DIM:HEAD_DIM + 1]
        inv = 1.0 / jnp.maximum(jnp.abs(den), floor[:, h:h + 1])
        ym_ref[:, sl] = (jax.nn.sigmoid(o.astype(F32)) * (num * inv)).astype(BF16)
        kw_t = (k.astype(F32) * wk[:, h:h + 1]).T.astype(BF16)
        s_ref[h] = decay[:, h:h + 1] * state + jnp.dot(kw_t, vext, preferred_element_type=F32)


def _mlstm_conv(pm, pc, gi, gf, bi, bf, cw, batch, seq):
    L = MLSTM_CHUNK
    nc = seq // L
    row = lambda width: pl.BlockSpec((L, width), lambda b, c: (b * nc + c, 0))
    m = batch * seq
    return pl.pallas_call(
        _mlstm_conv_kernel,
        grid=(batch, nc),
        in_specs=[row(2048), row(1536), row(LANES), row(LANES),
                  _resident((1, LANES)), _resident((1, LANES)), _resident((3, WIDTH))],
        out_specs=[row(WIDTH), row(WIDTH),
                   pl.BlockSpec((None, HEADS, HEAD_DIM, 2 * HEAD_DIM), lambda b, c: (b, 0, 0, 0)),
                   pl.BlockSpec((None, 1, LANES), lambda b, c: (b, 0, 0)),
                   pl.BlockSpec((None, 2, WIDTH), lambda b, c: (b, 0, 0))],
        out_shape=[jax.ShapeDtypeStruct((m, WIDTH), BF16), jax.ShapeDtypeStruct((m, WIDTH), BF16),
                   jax.ShapeDtypeStruct((batch, HEADS, HEAD_DIM, 2 * HEAD_DIM), F32),
                   jax.ShapeDtypeStruct((batch, 1, LANES), F32),
                   jax.ShapeDtypeStruct((batch, 2, WIDTH), F32)],
        compiler_params=pltpu.CompilerParams(dimension_semantics=("parallel", "arbitrary"),
                                             vmem_limit_bytes=VMEM_LIMIT),
        name="mlstm_conv",
    )(pm, pc, gi, gf, bi, bf, cw)


def _sb_kernel(bsb_ref, q_ref, kv_ref, o_ref):
    i = pl.program_id(1)
    blk = q_ref.shape[0]
    r_i = lax.broadcasted_iota(jnp.int32, (blk, blk), 0)
    c_i = lax.broadcasted_iota(jnp.int32, (blk, blk), 1)
    after = (r_i > c_i).astype(BF16)
    visible = c_i < r_i

    for h in range(HEADS):
        q = q_ref[:, h * HEAD_DIM:(h + 1) * HEAD_DIM]
        bias = bsb_ref[h]

        def tile(kb, run, acc, diagonal):
            k0 = pl.multiple_of(kb * blk, blk)
            k = kv_ref[pl.ds(k0, blk), h * HEAD_DIM:(h + 1) * HEAD_DIM]
            v = kv_ref[pl.ds(k0, blk), WIDTH + h * HEAD_DIM:WIDTH + (h + 1) * HEAD_DIM]
            z = lax.dot_general(q, k, (((1,), (1,)), ((), ())), preferred_element_type=F32) + bias
            sp = _softplus(z)
            log_keep = -sp
            if diagonal:
                log_keep = jnp.where(visible, log_keep, 0.0)
            later = run + jnp.dot(log_keep.astype(BF16), after, preferred_element_type=F32)
            a = jnp.exp(z - sp + later)
            if diagonal:
                a = jnp.where(visible, a, 0.0)
            acc = acc + jnp.dot(a.astype(BF16), v, preferred_element_type=F32)
            run = run + jnp.sum(log_keep, axis=1, keepdims=True)
            return run, acc

        run, acc = tile(i, jnp.zeros((blk, 1), F32), jnp.zeros((blk, HEAD_DIM), F32), True)
        run, acc = lax.fori_loop(0, i, lambda j, c: tile(i - 1 - j, c[0], c[1], False), (run, acc))
        o_ref[:, h * HEAD_DIM:(h + 1) * HEAD_DIM] = acc.astype(BF16)


def _sb_prompt(sq, kvb, bsb, batch, seq):
    blk = SB_BLOCK
    nq = seq // blk
    return pl.pallas_call(
        _sb_kernel,
        grid=(batch, nq),
        in_specs=[pl.BlockSpec(memory_space=pltpu.SMEM),
                  pl.BlockSpec((blk, WIDTH), lambda b, i: (b * nq + i, 0)),
                  pl.BlockSpec((seq, 2 * WIDTH), lambda b, i: (b, 0))],
        out_specs=pl.BlockSpec((blk, WIDTH), lambda b, i: (b * nq + i, 0)),
        out_shape=jax.ShapeDtypeStruct((batch * seq, WIDTH), BF16),
        compiler_params=pltpu.CompilerParams(dimension_semantics=("parallel", "parallel"),
                                             vmem_limit_bytes=VMEM_LIMIT),
        name="sb_prompt",
    )(bsb, sq, kvb)


def _decode_kernel(n_pages, pt_ref, bsb_ref, pm_ref, pc_ref, sq_ref, gi_ref, gf_ref, c0_ref, n0_ref,
                   m0_ref, cv0_ref, bi_ref, bf_ref, cw_ref, *refs):
    k_pages = refs[:n_pages]
    v_pages = refs[n_pages:2 * n_pages]
    ym_ref, yc_ref, ys_ref, c_ref, n_ref, m_ref, cv_ref = refs[2 * n_pages:]
    del pt_ref

    cb = pc_ref[:, 0:512].astype(F32)
    xc = pc_ref[:, 512:1024].astype(F32) * pc_ref[:, 1024:1536].astype(F32)
    prev = cv0_ref[...]
    cw = cw_ref[...]
    yc_ref[...] = (cb * (cw[0:1, :] * prev[0:1, :] + cw[1:2, :] * prev[1:2, :] + cw[2:3, :] * xc)).astype(BF16)
    cv_ref[0:1, :] = prev[1:2, :]
    cv_ref[1:2, :] = xc

    ig = gi_ref[...] + bi_ref[...]
    logf = -_softplus(-(gf_ref[...] + bf_ref[...]))
    m0 = m0_ref[...]
    m_t = jnp.maximum(logf + m0, ig)
    w_in = jnp.exp(ig - m_t)
    s_in = jnp.exp(logf + m0 - m_t)
    floor = jnp.exp(-m_t)
    m_ref[...] = m_t
    row128 = lax.broadcasted_iota(jnp.int32, (HEAD_DIM, HEAD_DIM), 0)
    for h in range(HEADS):
        sl = slice(h * HEAD_DIM, (h + 1) * HEAD_DIM)
        q = pm_ref[:, sl].astype(F32)
        k = pm_ref[:, 512 + h * HEAD_DIM:512 + (h + 1) * HEAD_DIM].astype(F32)
        v = pm_ref[:, 1024 + h * HEAD_DIM:1024 + (h + 1) * HEAD_DIM].astype(F32)
        o = pm_ref[:, 1536 + h * HEAD_DIM:1536 + (h + 1) * HEAD_DIM].astype(F32)
        qk_rows = jnp.where(row128 == 0, q, jnp.where(row128 == 1, k, 0.0))
        cols = qk_rows.T
        q_col = cols[:, 0:1]
        k_col = cols[:, 1:2]
        c0 = c0_ref[h]
        n0 = n0_ref[h:h + 1, :]
        sc = s_in[:, h:h + 1]
        wi = w_in[:, h:h + 1]
        w = wi * jnp.sum(q * k, axis=1, keepdims=True)
        num = sc * jnp.sum(q_col * c0, axis=0, keepdims=True) + w * v
        den = sc * jnp.sum(q * n0, axis=1, keepdims=True) + w
        hc = num * (1.0 / jnp.maximum(jnp.abs(den), floor[:, h:h + 1]))
        ym_ref[:, sl] = (jax.nn.sigmoid(o) * hc).astype(BF16)
        c_ref[h] = sc * c0 + wi * (k_col * v)
        n_ref[h:h + 1, :] = sc * n0 + wi * k

    page = k_pages[0].shape[0]
    r8 = lax.broadcasted_iota(jnp.int32, (8, WIDTH), 0)
    l8 = lax.broadcasted_iota(jnp.int32, (8, WIDTH), 1)
    own = (l8 // HEAD_DIM) == r8
    q_rows = jnp.where(own, sq_ref[...].astype(F32), 0.0).astype(BF16)
    rcol = lax.broadcasted_iota(jnp.int32, (8, 1), 0)
    bias = jnp.zeros((8, 1), F32)
    for h in range(HEADS):
        bias = jnp.where(rcol == h, bsb_ref[h], bias)
    zs, sps = [], []
    for j in range(n_pages):
        kp = k_pages[j][...].astype(BF16)
        z = lax.dot_general(q_rows, kp, (((1,), (1,)), ((), ())), preferred_element_type=F32) + bias
        zs.append(z)
        sps.append(_softplus(z))
    log_keep = jnp.concatenate([-s for s in sps], axis=0)
    r_i = lax.broadcasted_iota(jnp.int32, (page, page), 0)
    c_i = lax.broadcasted_iota(jnp.int32, (page, page), 1)
    after = (r_i > c_i).astype(BF16)
    within = jnp.dot(log_keep.astype(BF16), after, preferred_element_type=F32)
    totals = jnp.sum(log_keep, axis=1, keepdims=True)
    run = jnp.zeros((8, 1), F32)
    out = jnp.zeros((8, WIDTH), F32)
    for j in reversed(range(n_pages)):
        later = within[8 * j:8 * j + 8, :] + run
        a = jnp.exp(zs[j] - sps[j] + later)
        out = out + jnp.dot(a.astype(BF16), v_pages[j][...].astype(BF16), preferred_element_type=F32)
        run = run + totals[8 * j:8 * j + 8, :]
    ys_ref[...] = jnp.sum(jnp.where(own, out, 0.0), axis=0, keepdims=True).astype(BF16)


def _decode(layer, pm, pc, sq, gi, gf, state_c, state_n, m0, state_conv, bi, bf, cw, bsb,
            page_table, cache_k, cache_v):
    nseq, n_pages = page_table.shape
    page = cache_k.shape[2]
    row = lambda width: pl.BlockSpec((None, 1, width), lambda s, pt: (s, 0, 0))
    fixed = lambda shape: pl.BlockSpec(shape, lambda s, pt: (0,) * len(shape), pipeline_mode=pl.Buffered(1))
    state4 = pl.BlockSpec((None, None, HEADS, HEAD_DIM, HEAD_DIM), lambda s, pt: (layer, s, 0, 0, 0))
    state3 = pl.BlockSpec((None, None, HEADS, HEAD_DIM), lambda s, pt: (layer, s, 0, 0))
    conv3 = pl.BlockSpec((None, None, 2, WIDTH), lambda s, pt: (layer, s, 0, 0))

    def page_spec(j):
        return pl.BlockSpec((None, None, page, WIDTH), lambda s, pt: (layer, pt[s, j], 0, 0))

    in_specs = ([pl.BlockSpec(memory_space=pltpu.SMEM),
                 row(2048), row(1536), row(WIDTH), row(LANES), row(LANES),
                 state4, state3, row(LANES), conv3,
                 fixed((1, LANES)), fixed((1, LANES)), fixed((3, WIDTH))]
                + [page_spec(j) for j in range(n_pages)] * 2)
    out_specs = [row(WIDTH), row(WIDTH), row(WIDTH),
                 pl.BlockSpec((None, HEADS, HEAD_DIM, HEAD_DIM), lambda s, pt: (s, 0, 0, 0)),
                 pl.BlockSpec((None, HEADS, HEAD_DIM), lambda s, pt: (s, 0, 0)),
                 row(LANES),
                 pl.BlockSpec((None, 2, WIDTH), lambda s, pt: (s, 0, 0))]
    out_shape = [jax.ShapeDtypeStruct((nseq, 1, WIDTH), BF16)] * 3 + [
        jax.ShapeDtypeStruct((nseq, HEADS, HEAD_DIM, HEAD_DIM), F32),
        jax.ShapeDtypeStruct((nseq, HEADS, HEAD_DIM), F32),
        jax.ShapeDtypeStruct((nseq, 1, LANES), F32),
        jax.ShapeDtypeStruct((nseq, 2, WIDTH), F32)]
    r3 = lambda a: a.reshape(nseq, 1, a.shape[-1])
    return pl.pallas_call(
        functools.partial(_decode_kernel, n_pages),
        grid_spec=pltpu.PrefetchScalarGridSpec(
            num_scalar_prefetch=1, grid=(nseq,), in_specs=in_specs, out_specs=out_specs),
        out_shape=out_shape,
        compiler_params=pltpu.CompilerParams(dimension_semantics=("parallel",),
                                             vmem_limit_bytes=VMEM_LIMIT),
        name="decode",
    )(page_table, bsb, r3(pm), r3(pc), r3(sq), r3(gi), r3(gf), state_c, state_n, m0, state_conv,
      bi, bf, cw, *([cache_k] * n_pages), *([cache_v] * n_pages))


def _post_kernel(final, ym_ref, yc_ref, ys_ref, gt_ref, h_ref, p_ref, wb_ref, wo_ref, gmlp_ref, wup_ref,
                 wdn_ref, gple_ref, wpg_ref, wple_ref, gfin_ref, o_ref):
    mix = None
    for n, y_ref in enumerate((ym_ref, yc_ref, ys_ref)):
        pb = jnp.dot(y_ref[...], wb_ref[n], preferred_element_type=F32)
        gate = jax.nn.sigmoid(gt_ref[:, n * D_MODEL:(n + 1) * D_MODEL].astype(F32))
        mix = gate * pb if mix is None else mix + gate * pb
    h = h_ref[...] + jnp.dot(mix.astype(BF16), wo_ref[...], preferred_element_type=F32)

    xn = _rms(h, gmlp_ref[...]).astype(BF16)
    chunk = 1024
    for c in range(FFN_DIM // chunk):
        u = jnp.dot(xn, wup_ref[:, c * chunk:(c + 1) * chunk], preferred_element_type=F32)
        r = jnp.square(jnp.maximum(u, 0.0)).astype(BF16)
        h = h + jnp.dot(r, wdn_ref[c * chunk:(c + 1) * chunk, :], preferred_element_type=F32)

    gate = jax.nn.sigmoid(jnp.dot(_rms(h, gple_ref[...]).astype(BF16), wpg_ref[...],
                                  preferred_element_type=F32))
    h = h + jnp.dot(p_ref[...].astype(BF16), wple_ref[...], preferred_element_type=F32) * gate
    o_ref[...] = _rms(h, gfin_ref[...]) if final else h


def _post(final, ym, yc, ys, gt, h, p, wb, wo, gmlp, wup, wdn, gple, wpg, wple, gfin, tm):
    m = h.shape[0]
    row = lambda width: pl.BlockSpec((tm, width), lambda i: (i, 0))
    return pl.pallas_call(
        functools.partial(_post_kernel, final),
        grid=(m // tm,),
        in_specs=[row(WIDTH), row(WIDTH), row(WIDTH), row(3 * D_MODEL), row(D_MODEL), row(PLE_DIM),
                  _resident((3, WIDTH, D_MODEL)), _resident((D_MODEL, D_MODEL)), _resident((1, D_MODEL)),
                  _resident((D_MODEL, FFN_DIM)), _resident((FFN_DIM, D_MODEL)), _resident((1, D_MODEL)),
                  _resident((D_MODEL, D_MODEL)), _resident((PLE_DIM, D_MODEL)), _resident((1, D_MODEL))],
        out_specs=row(D_MODEL),
        out_shape=jax.ShapeDtypeStruct((m, D_MODEL), F32),
        compiler_params=pltpu.CompilerParams(dimension_semantics=("parallel",),
                                             vmem_limit_bytes=VMEM_LIMIT),
        name="post",
    )(ym, yc, ys, gt, h, p, wb, wo, gmlp, wup, wdn, gple, wpg, wple, gfin)


def _pad_lanes(x):
    return jnp.pad(x, ((0, 0), (0, LANES - x.shape[-1])))


def _prep_w_in(w):
    zpad = jnp.zeros((w.shape[0], LANES - HEADS), w.dtype)
    gates0 = 4 * WIDTH
    return jnp.concatenate([w[:, :gates0], w[:, gates0 + 2 * HEADS:],
                            w[:, gates0:gates0 + HEADS], zpad,
                            w[:, gates0 + HEADS:gates0 + 2 * HEADS], zpad], axis=1).astype(BF16)


def kernel(x_prompt, x_sample, cache_k, cache_v, state_mlstm_C, state_mlstm_n, state_mlstm_m, state_conv,
           page_table, p_prompt, p_sample, g_mix, w_in, b_if, b_sb, conv_w, w_branch, w_out, g_mlp, w_up,
           w_down, g_ple, w_ple_gate, w_ple, g_final):
    depth = w_in.shape[0]
    batch, seq, _ = x_prompt.shape
    nseq = x_sample.shape[0]
    n_phys, page = cache_k.shape[1], cache_k.shape[2]
    cache_k = cache_k.reshape(depth, n_phys, page, WIDTH)
    cache_v = cache_v.reshape(depth, n_phys, page, WIDTH)
    hp = x_prompt.reshape(batch * seq, D_MODEL)
    hs = x_sample.reshape(nseq, D_MODEL)
    gfin = g_final.reshape(1, D_MODEL)
    tm_p = 512
    tm_s = nseq

    outs_p, outs_s = [], []
    for li in range(depth):
        final = li == depth - 1
        wp = _prep_w_in(w_in[li])
        gmix = g_mix[li].reshape(1, D_MODEL)
        bi = _pad_lanes(b_if[li, :HEADS].reshape(1, HEADS))
        bf = _pad_lanes(b_if[li, HEADS:].reshape(1, HEADS))
        post_w = (w_branch[li].astype(BF16), w_out[li].astype(BF16), g_mlp[li].reshape(1, D_MODEL),
                  w_up[li].astype(BF16), w_down[li].astype(BF16), g_ple[li].reshape(1, D_MODEL),
                  w_ple_gate[li].astype(BF16), w_ple[li].astype(BF16), gfin)

        pm, pc, sq, kf, vf, kvb, gt, gi, gf = _proj(hp, gmix, wp, tm_p)
        ym, yc, s_p, m_p, cv_p = _mlstm_conv(pm, pc, gi, gf, bi, bf, conv_w[li], batch, seq)
        ys = _sb_prompt(sq, kvb, b_sb[li], batch, seq)
        hp = _post(final, ym, yc, ys, gt, hp, p_prompt[li].reshape(batch * seq, PLE_DIM), *post_w, tm_p)
        outs_p.append((kf.reshape(batch, seq, HEADS, HEAD_DIM), vf.reshape(batch, seq, HEADS, HEAD_DIM),
                       s_p[..., :HEAD_DIM], s_p[..., HEAD_DIM], m_p[:, 0, :HEADS], cv_p))

        pm, pc, sq, kf, vf, kvb, gt, gi, gf = _proj(hs, gmix, wp, tm_s)
        m0 = _pad_lanes(state_mlstm_m[li]).reshape(nseq, 1, LANES)
        ym, yc, ys, c_s, n_s, m_s, cv_s = _decode(
            li, pm, pc, sq, gi, gf, state_mlstm_C, state_mlstm_n, m0, state_conv, bi, bf, conv_w[li],
            b_sb[li], page_table, cache_k, cache_v)
        hs = _post(final, ym.reshape(nseq, WIDTH), yc.reshape(nseq, WIDTH), ys.reshape(nseq, WIDTH), gt, hs,
                   p_sample[li].reshape(nseq, PLE_DIM), *post_w, tm_s)
        outs_s.append((kf.reshape(nseq, 1, HEADS, HEAD_DIM), vf.reshape(nseq, 1, HEADS, HEAD_DIM),
                       c_s, n_s, m_s[:, 0, :HEADS], cv_s))

    stack = lambda outs, j: jnp.stack([o[j] for o in outs])
    return ((hp.reshape(batch, seq, D_MODEL), hs.reshape(nseq, 1, D_MODEL))
            + tuple(stack(outs_p, j) for j in range(6)) + tuple(stack(outs_s, j) for j in range(6)))
```

```python
import functools

import jax
import jax.numpy as jnp
from jax import lax
from jax.experimental import pallas as pl
from jax.experimental.pallas import tpu as pltpu

F32 = jnp.float32
BF16 = jnp.bfloat16

D_MODEL = 1024
WIDTH = 512
HEADS = 4
HEAD_DIM = 128
FFN_DIM = 4 * D_MODEL
PLE_DIM = 256
RMS_EPS = 1e-6
LOG2E = 1.4426950408889634
LANES = 128

C_PM = 0
C_PC = 2048
C_SQ = 3584
C_SK = 4096
C_SV = 4608
C_GT = 5120
C_GI = 8192
C_GF = 8320
N_PROJ = 8448

MLSTM_CHUNK = 256
SB_BLOCK = 256
VMEM_LIMIT = 56 * 1024 * 1024


def _softplus(x):
    return jnp.maximum(x, 0.0) + jnp.log(1.0 + jnp.exp(-jnp.abs(x)))


def _stick_terms(z):
    drop = jnp.maximum(z, 0.0) + jnp.log2(1.0 + jnp.exp2(-jnp.abs(z)))
    return drop, z - drop


def _rms(x, g):
    return x * lax.rsqrt(jnp.mean(x * x, axis=-1, keepdims=True) + RMS_EPS) * g


def _resident(shape):
    nd = len(shape)
    return pl.BlockSpec(shape, lambda *_: (0,) * nd, pipeline_mode=pl.Buffered(1))


def _proj_kernel(x_ref, g_ref, w_ref, pm_ref, pc_ref, sq_ref, kf_ref, vf_ref, kvb_ref, gt_ref,
                 gi_ref, gf_ref):
    xn = _rms(x_ref[...], g_ref[...]).astype(BF16)

    def mm(c0, width):
        return jnp.dot(xn, w_ref[:, c0:c0 + width], preferred_element_type=F32)

    scale = HEAD_DIM ** -0.5
    pm_ref[:, 0:512] = mm(C_PM, 512).astype(BF16)
    pm_ref[:, 512:1024] = (mm(C_PM + 512, 512) * scale).astype(BF16)
    pm_ref[:, 1024:1536] = mm(C_PM + 1024, 512).astype(BF16)
    pm_ref[:, 1536:2048] = mm(C_PM + 1536, 512).astype(BF16)
    for j in range(3):
        pc_ref[:, j * 512:(j + 1) * 512] = mm(C_PC + j * 512, 512).astype(BF16)
    sq_ref[...] = (mm(C_SQ, 512) * (scale * LOG2E)).astype(BF16)
    k = mm(C_SK, 512)
    v = mm(C_SV, 512)
    tm = k.shape[0]
    for h in range(HEADS):
        kf_ref[pl.ds(h, tm, stride=HEADS), :] = k[:, h * HEAD_DIM:(h + 1) * HEAD_DIM]
        vf_ref[pl.ds(h, tm, stride=HEADS), :] = v[:, h * HEAD_DIM:(h + 1) * HEAD_DIM]
    kvb_ref[:, 0:512] = k.astype(BF16)
    kvb_ref[:, 512:1024] = v.astype(BF16)
    for j in range(6):
        gt_ref[:, j * 512:(j + 1) * 512] = mm(C_GT + j * 512, 512).astype(BF16)
    gi_ref[...] = mm(C_GI, LANES)
    gf_ref[...] = mm(C_GF, LANES)


def _proj(x2d, g, wp, tm):
    m = x2d.shape[0]
    row = lambda width: pl.BlockSpec((tm, width), lambda i: (i, 0))
    widths = (2048, 1536, 512, None, None, 1024, 3072, LANES, LANES)
    dtypes = (BF16, BF16, BF16, F32, F32, BF16, BF16, F32, F32)
    out_specs = [row(w) if w else pl.BlockSpec((tm * HEADS, HEAD_DIM), lambda i: (i, 0)) for w in widths]
    out_shape = [jax.ShapeDtypeStruct((m, w) if w else (m * HEADS, HEAD_DIM), d) for w, d in zip(widths, dtypes)]
    return pl.pallas_call(
        _proj_kernel,
        grid=(m // tm,),
        in_specs=[row(D_MODEL), _resident((1, D_MODEL)), _resident((D_MODEL, N_PROJ))],
        out_specs=out_specs,
        out_shape=out_shape,
        compiler_params=pltpu.CompilerParams(dimension_semantics=("parallel",),
                                             vmem_limit_bytes=VMEM_LIMIT),
        name="proj",
    )(x2d, g, wp)


def _scan_rows(x, op, fill):
    n = x.shape[0]
    row = lax.broadcasted_iota(jnp.int32, x.shape, 0)
    k = 1
    while k < n:
        shifted = pltpu.roll(x, k, axis=0)
        x = op(x, jnp.where(row >= k, shifted, fill(x)))
        k *= 2
    return x


def _mlstm_conv_kernel(pm_ref, pc_ref, gi_ref, gf_ref, bi_ref, bf_ref, cw_ref,
                       ym_ref, yc_ref, s_ref, m_ref, cs_ref):
    c = pl.program_id(1)
    L = pm_ref.shape[0]

    @pl.when(c == 0)
    def _():
        s_ref[...] = jnp.zeros_like(s_ref)
        m_ref[...] = jnp.zeros_like(m_ref)
        cs_ref[...] = jnp.zeros_like(cs_ref)

    cb = pc_ref[:, 0:512].astype(F32)
    xc = pc_ref[:, 512:1024].astype(F32) * pc_ref[:, 1024:1536].astype(F32)
    prev = cs_ref[...]
    row = lax.broadcasted_iota(jnp.int32, xc.shape, 0)
    xc1 = jnp.where(row == 0, prev[1:2, :], pltpu.roll(xc, 1, axis=0))
    xc2 = jnp.where(row == 0, prev[0:1, :], jnp.where(row == 1, prev[1:2, :], pltpu.roll(xc, 2, axis=0)))
    cw = cw_ref[...]
    yc_ref[...] = (cb * (cw[0:1, :] * xc2 + cw[1:2, :] * xc1 + cw[2:3, :] * xc)).astype(BF16)
    cs_ref[...] = xc[L - 2:L, :]

    m_prev = m_ref[...]
    ig = gi_ref[...] + bi_ref[...]
    logf = -_softplus(-(gf_ref[...] + bf_ref[...]))
    b = _scan_rows(logf, jnp.add, jnp.zeros_like)
    a = ig - b
    mx = jnp.maximum(m_prev, _scan_rows(a, jnp.maximum, lambda x: x))
    m_t = b + mx
    a_t = a.T
    s_in = jnp.exp(m_prev - mx)
    floor = jnp.exp(-m_t)
    mx_last = mx[L - 1:L, :]
    wk = jnp.exp(a - mx_last)
    decay = jnp.exp(m_prev - mx_last)
    m_ref[...] = m_t[L - 1:L, :]

    r_i = lax.broadcasted_iota(jnp.int32, (L, L), 0)
    c_i = lax.broadcasted_iota(jnp.int32, (L, L), 1)
    causal = c_i <= r_i
    one_col = (lax.broadcasted_iota(jnp.int32, (L, HEAD_DIM), 1) == 0).astype(BF16)
    for h in range(HEADS):
        sl = slice(h * HEAD_DIM, (h + 1) * HEAD_DIM)
        q = pm_ref[:, sl]
        k = pm_ref[:, 512 + h * HEAD_DIM:512 + (h + 1) * HEAD_DIM]
        v = pm_ref[:, 1024 + h * HEAD_DIM:1024 + (h + 1) * HEAD_DIM]
        o = pm_ref[:, 1536 + h * HEAD_DIM:1536 + (h + 1) * HEAD_DIM]
        vext = jnp.concatenate([v, one_col], axis=1)
        qk = lax.dot_general(q, k, (((1,), (1,)), ((), ())), preferred_element_type=F32)
        e = jnp.exp(a_t[h:h + 1, :] - mx[:, h:h + 1])
        w = jnp.where(causal, e, 0.0) * qk
        intra = jnp.dot(w.astype(BF16), vext, preferred_element_type=F32)
        state = s_ref[h]
        inter = jnp.dot(q, state.astype(BF16), preferred_element_type=F32)
        sc = s_in[:, h:h + 1]
        num = sc * inter[:, :HEAD_DIM] + intra[:, :HEAD_DIM]
        den = sc * inter[:, HEAD_DIM:HEAD_DIM + 1] + intra[:, HEAD_DIM:HEAD_DIM + 1]
        inv = 1.0 / jnp.maximum(jnp.abs(den), floor[:, h:h + 1])
        ym_ref[:, sl] = (jax.nn.sigmoid(o.astype(F32)) * (num * inv)).astype(BF16)
        kw_t = (k.astype(F32) * wk[:, h:h + 1]).T.astype(BF16)
        s_ref[h] = decay[:, h:h + 1] * state + jnp.dot(kw_t, vext, preferred_element_type=F32)


def _mlstm_conv(pm, pc, gi, gf, bi, bf, cw, batch, seq):
    L = MLSTM_CHUNK
    nc = seq // L
    row = lambda width: pl.BlockSpec((L, width), lambda b, c: (b * nc + c, 0))
    m = batch * seq
    return pl.pallas_call(
        _mlstm_conv_kernel,
        grid=(batch, nc),
        in_specs=[row(2048), row(1536), row(LANES), row(LANES),
                  _resident((1, LANES)), _resident((1, LANES)), _resident((3, WIDTH))],
        out_specs=[row(WIDTH), row(WIDTH),
                   pl.BlockSpec((None, HEADS, HEAD_DIM, 2 * HEAD_DIM), lambda b, c: (b, 0, 0, 0)),
                   pl.BlockSpec((None, 1, LANES), lambda b, c: (b, 0, 0)),
                   pl.BlockSpec((None, 2, WIDTH), lambda b, c: (b, 0, 0))],
        out_shape=[jax.ShapeDtypeStruct((m, WIDTH), BF16), jax.ShapeDtypeStruct((m, WIDTH), BF16),
                   jax.ShapeDtypeStruct((batch, HEADS, HEAD_DIM, 2 * HEAD_DIM), F32),
                   jax.ShapeDtypeStruct((batch, 1, LANES), F32),
                   jax.ShapeDtypeStruct((batch, 2, WIDTH), F32)],
        compiler_params=pltpu.CompilerParams(dimension_semantics=("parallel", "arbitrary"),
                                             vmem_limit_bytes=VMEM_LIMIT),
        name="mlstm_conv",
    )(pm, pc, gi, gf, bi, bf, cw)


def _sb_kernel(bsb_ref, q_ref, kv_ref, o_ref, acc_ref, run_ref):
    i = pl.program_id(1)
    blk = q_ref.shape[0]
    r_i = lax.broadcasted_iota(jnp.int32, (blk, blk), 0)
    c_i = lax.broadcasted_iota(jnp.int32, (blk, blk), 1)
    after = (r_i > c_i).astype(BF16)
    visible = c_i < r_i
    acc_ref[...] = jnp.zeros_like(acc_ref)
    run_ref[...] = jnp.zeros_like(run_ref)

    def tile(kb, diagonal):
        k0 = pl.multiple_of(kb * blk, blk)
        for h in range(HEADS):
            sl = slice(h * HEAD_DIM, (h + 1) * HEAD_DIM)
            k = kv_ref[pl.ds(k0, blk), sl]
            v = kv_ref[pl.ds(k0, blk), WIDTH + h * HEAD_DIM:WIDTH + (h + 1) * HEAD_DIM]
            z = (lax.dot_general(q_ref[:, sl], k, (((1,), (1,)), ((), ())), preferred_element_type=F32)
                 + bsb_ref[h] * LOG2E)
            drop, log_beta = _stick_terms(z)
            if diagonal:
                drop = jnp.where(visible, drop, 0.0)
            run = run_ref[h]
            later = run + jnp.dot(drop.astype(BF16), after, preferred_element_type=F32)
            a = jnp.exp2(log_beta - later)
            if diagonal:
                a = jnp.where(visible, a, 0.0)
            acc_ref[:, sl] += jnp.dot(a.astype(BF16), v, preferred_element_type=F32)
            run_ref[h] = run + jnp.sum(drop, axis=1, keepdims=True)

    tile(i, True)

    def body(j, carry):
        tile(i - 1 - j, False)
        return carry

    lax.fori_loop(0, i, body, 0)
    o_ref[...] = acc_ref[...].astype(BF16)


def _sb_prompt(sq, kvb, bsb, batch, seq):
    blk = SB_BLOCK
    nq = seq // blk
    return pl.pallas_call(
        _sb_kernel,
        grid=(batch, nq),
        in_specs=[pl.BlockSpec(memory_space=pltpu.SMEM),
                  pl.BlockSpec((blk, WIDTH), lambda b, i: (b * nq + i, 0)),
                  pl.BlockSpec((seq, 2 * WIDTH), lambda b, i: (b, 0))],
        out_specs=pl.BlockSpec((blk, WIDTH), lambda b, i: (b * nq + i, 0)),
        out_shape=jax.ShapeDtypeStruct((batch * seq, WIDTH), BF16),
        scratch_shapes=[pltpu.VMEM((blk, WIDTH), F32), pltpu.VMEM((HEADS, blk, 1), F32)],
        compiler_params=pltpu.CompilerParams(dimension_semantics=("parallel", "parallel"),
                                             vmem_limit_bytes=VMEM_LIMIT),
        name="sb_prompt",
    )(bsb, sq, kvb)


def _decode_kernel(n_pages, pt_ref, bsb_ref, pm_ref, pc_ref, sq_ref, gi_ref, gf_ref, c0_ref, n0_ref,
                   m0_ref, cv0_ref, bi_ref, bf_ref, cw_ref, *refs):
    k_pages = refs[:n_pages]
    v_pages = refs[n_pages:2 * n_pages]
    ym_ref, yc_ref, ys_ref, c_ref, n_ref, m_ref, cv_ref = refs[2 * n_pages:]
    del pt_ref

    cb = pc_ref[:, 0:512].astype(F32)
    xc = pc_ref[:, 512:1024].astype(F32) * pc_ref[:, 1024:1536].astype(F32)
    prev = cv0_ref[...]
    cw = cw_ref[...]
    yc_ref[...] = (cb * (cw[0:1, :] * prev[0:1, :] + cw[1:2, :] * prev[1:2, :] + cw[2:3, :] * xc)).astype(BF16)
    cv_ref[0:1, :] = prev[1:2, :]
    cv_ref[1:2, :] = xc

    ig = gi_ref[...] + bi_ref[...]
    logf = -_softplus(-(gf_ref[...] + bf_ref[...]))
    m0 = m0_ref[...]
    m_t = jnp.maximum(logf + m0, ig)
    w_in = jnp.exp(ig - m_t)
    s_in = jnp.exp(logf + m0 - m_t)
    floor = jnp.exp(-m_t)
    m_ref[...] = m_t
    row128 = lax.broadcasted_iota(jnp.int32, (HEAD_DIM, HEAD_DIM), 0)
    for h in range(HEADS):
        sl = slice(h * HEAD_DIM, (h + 1) * HEAD_DIM)
        q = pm_ref[:, sl].astype(F32)
        k = pm_ref[:, 512 + h * HEAD_DIM:512 + (h + 1) * HEAD_DIM].astype(F32)
        v = pm_ref[:, 1024 + h * HEAD_DIM:1024 + (h + 1) * HEAD_DIM].astype(F32)
        o = pm_ref[:, 1536 + h * HEAD_DIM:1536 + (h + 1) * HEAD_DIM].astype(F32)
        qk_rows = jnp.where(row128 == 0, q, jnp.where(row128 == 1, k, 0.0))
        cols = qk_rows.T
        q_col = cols[:, 0:1]
        k_col = cols[:, 1:2]
        c0 = c0_ref[h]
        n0 = n0_ref[h:h + 1, :]
        sc = s_in[:, h:h + 1]
        wi = w_in[:, h:h + 1]
        w = wi * jnp.sum(q * k, axis=1, keepdims=True)
        num = sc * jnp.sum(q_col * c0, axis=0, keepdims=True) + w * v
        den = sc * jnp.sum(q * n0, axis=1, keepdims=True) + w
        hc = num * (1.0 / jnp.maximum(jnp.abs(den), floor[:, h:h + 1]))
        ym_ref[:, sl] = (jax.nn.sigmoid(o) * hc).astype(BF16)
        c_ref[h] = sc * c0 + wi * (k_col * v)
        n_ref[h:h + 1, :] = sc * n0 + wi * k

    pw = k_pages[0].shape[0]
    r8 = lax.broadcasted_iota(jnp.int32, (8, HEAD_DIM), 0)
    sq = sq_ref[...].astype(F32)
    q_rows = jnp.zeros((8, HEAD_DIM), F32)
    rcol = lax.broadcasted_iota(jnp.int32, (8, 1), 0)
    bias = jnp.zeros((8, 1), F32)
    for h in range(HEADS):
        q_rows = jnp.where(r8 == h, sq[:, h * HEAD_DIM:(h + 1) * HEAD_DIM], q_rows)
        bias = jnp.where(rcol == h, bsb_ref[h] * LOG2E, bias)
    q_rows = q_rows.astype(BF16)
    own = (lax.broadcasted_iota(jnp.int32, (8, pw), 1) % HEADS) == lax.broadcasted_iota(jnp.int32, (8, pw), 0)
    drops, log_betas = [], []
    for j in range(n_pages):
        kp = k_pages[j][...].astype(BF16)
        z = lax.dot_general(q_rows, kp, (((1,), (1,)), ((), ())), preferred_element_type=F32) + bias
        drop, log_beta = _stick_terms(z)
        drops.append(jnp.where(own, drop, 0.0))
        log_betas.append(log_beta)
    drop_all = jnp.concatenate(drops, axis=0)
    r_i = lax.broadcasted_iota(jnp.int32, (pw, pw), 0)
    c_i = lax.broadcasted_iota(jnp.int32, (pw, pw), 1)
    after = jnp.logical_and(r_i // HEADS > c_i // HEADS, r_i % HEADS == c_i % HEADS).astype(BF16)
    within = jnp.dot(drop_all.astype(BF16), after, preferred_element_type=F32)
    totals = jnp.sum(drop_all, axis=1, keepdims=True)
    run = jnp.zeros((8, 1), F32)
    out = jnp.zeros((8, HEAD_DIM), F32)
    for j in reversed(range(n_pages)):
        later = within[8 * j:8 * j + 8, :] + run
        a = jnp.where(own, jnp.exp2(log_betas[j] - later), 0.0)
        out = out + jnp.dot(a.astype(BF16), v_pages[j][...].astype(BF16), preferred_element_type=F32)
        run = run + totals[8 * j:8 * j + 8, :]
    for h in range(HEADS):
        ys_ref[:, h * HEAD_DIM:(h + 1) * HEAD_DIM] = out[h:h + 1, :].astype(BF16)


def _decode(layer, pm, pc, sq, gi, gf, state_c, state_n, m0, state_conv, bi, bf, cw, bsb,
            page_table, cache_k, cache_v):
    nseq, n_pages = page_table.shape
    pw = cache_k.shape[2]
    row = lambda width: pl.BlockSpec((None, 1, width), lambda s, pt: (s, 0, 0))
    fixed = lambda shape: pl.BlockSpec(shape, lambda s, pt: (0,) * len(shape), pipeline_mode=pl.Buffered(1))
    state4 = pl.BlockSpec((None, None, HEADS, HEAD_DIM, HEAD_DIM), lambda s, pt: (layer, s, 0, 0, 0))
    state3 = pl.BlockSpec((None, None, HEADS, HEAD_DIM), lambda s, pt: (layer, s, 0, 0))
    conv3 = pl.BlockSpec((None, None, 2, WIDTH), lambda s, pt: (layer, s, 0, 0))

    def page_spec(j):
        return pl.BlockSpec((None, None, pw, HEAD_DIM), lambda s, pt: (layer, pt[s, j], 0, 0))

    in_specs = ([pl.BlockSpec(memory_space=pltpu.SMEM),
                 row(2048), row(1536), row(WIDTH), row(LANES), row(LANES),
                 state4, state3, row(LANES), conv3,
                 fixed((1, LANES)), fixed((1, LANES)), fixed((3, WIDTH))]
                + [page_spec(j) for j in range(n_pages)] * 2)
    out_specs = [row(WIDTH), row(WIDTH), row(WIDTH),
                 pl.BlockSpec((None, HEADS, HEAD_DIM, HEAD_DIM), lambda s, pt: (s, 0, 0, 0)),
                 pl.BlockSpec((None, HEADS, HEAD_DIM), lambda s, pt: (s, 0, 0)),
                 row(LANES),
                 pl.BlockSpec((None, 2, WIDTH), lambda s, pt: (s, 0, 0))]
    out_shape = [jax.ShapeDtypeStruct((nseq, 1, WIDTH), BF16)] * 3 + [
        jax.ShapeDtypeStruct((nseq, HEADS, HEAD_DIM, HEAD_DIM), F32),
        jax.ShapeDtypeStruct((nseq, HEADS, HEAD_DIM), F32),
        jax.ShapeDtypeStruct((nseq, 1, LANES), F32),
        jax.ShapeDtypeStruct((nseq, 2, WIDTH), F32)]
    r3 = lambda a: a.reshape(nseq, 1, a.shape[-1])
    return pl.pallas_call(
        functools.partial(_decode_kernel, n_pages),
        grid_spec=pltpu.PrefetchScalarGridSpec(
            num_scalar_prefetch=1, grid=(nseq,), in_specs=in_specs, out_specs=out_specs),
        out_shape=out_shape,
        compiler_params=pltpu.CompilerParams(dimension_semantics=("parallel",),
                                             vmem_limit_bytes=VMEM_LIMIT),
        name="decode",
    )(page_table, bsb, r3(pm), r3(pc), r3(sq), r3(gi), r3(gf), state_c, state_n, m0, state_conv,
      bi, bf, cw, *([cache_k] * n_pages), *([cache_v] * n_pages))


def _post_kernel(final, ym_ref, yc_ref, ys_ref, gt_ref, h_ref, p_ref, wb_ref, wo_ref, gmlp_ref, wup_ref,
                 wdn_ref, gple_ref, wpg_ref, wple_ref, gfin_ref, o_ref):
    mix = None
    for n, y_ref in enumerate((ym_ref, yc_ref, ys_ref)):
        pb = jnp.dot(y_ref[...], wb_ref[n], preferred_element_type=F32)
        gate = jax.nn.sigmoid(gt_ref[:, n * D_MODEL:(n + 1) * D_MODEL].astype(F32))
        mix = gate * pb if mix is None else mix + gate * pb
    h = h_ref[...] + jnp.dot(mix.astype(BF16), wo_ref[...], preferred_element_type=F32)

    xn = _rms(h, gmlp_ref[...]).astype(BF16)
    chunk = 1024
    for c in range(FFN_DIM // chunk):
        u = jnp.dot(xn, wup_ref[:, c * chunk:(c + 1) * chunk], preferred_element_type=F32)
        r = jnp.square(jnp.maximum(u, 0.0)).astype(BF16)
        h = h + jnp.dot(r, wdn_ref[c * chunk:(c + 1) * chunk, :], preferred_element_type=F32)

    gate = jax.nn.sigmoid(jnp.dot(_rms(h, gple_ref[...]).astype(BF16), wpg_ref[...],
                                  preferred_element_type=F32))
    h = h + jnp.dot(p_ref[...].astype(BF16), wple_ref[...], preferred_element_type=F32) * gate
    o_ref[...] = _rms(h, gfin_ref[...]) if final else h


def _post(final, ym, yc, ys, gt, h, p, wb, wo, gmlp, wup, wdn, gple, wpg, wple, gfin, tm):
    m = h.shape[0]
    row = lambda width: pl.BlockSpec((tm, width), lambda i: (i, 0))
    return pl.pallas_call(
        functools.partial(_post_kernel, final),
        grid=(m // tm,),
        in_specs=[row(WIDTH), row(WIDTH), row(WIDTH), row(3 * D_MODEL), row(D_MODEL), row(PLE_DIM),
                  _resident((3, WIDTH, D_MODEL)), _resident((D_MODEL, D_MODEL)), _resident((1, D_MODEL)),
                  _resident((D_MODEL, FFN_DIM)), _resident((FFN_DIM, D_MODEL)), _resident((1, D_MODEL)),
                  _resident((D_MODEL, D_MODEL)), _resident((PLE_DIM, D_MODEL)), _resident((1, D_MODEL))],
        out_specs=row(D_MODEL),
        out_shape=jax.ShapeDtypeStruct((m, D_MODEL), F32),
        compiler_params=pltpu.CompilerParams(dimension_semantics=("parallel",),
                                             vmem_limit_bytes=VMEM_LIMIT),
        name="post",
    )(ym, yc, ys, gt, h, p, wb, wo, gmlp, wup, wdn, gple, wpg, wple, gfin)


def _pad_lanes(x):
    return jnp.pad(x, ((0, 0), (0, LANES - x.shape[-1])))


def _prep_w_in(w):
    zpad = jnp.zeros((w.shape[0], LANES - HEADS), w.dtype)
    gates0 = 4 * WIDTH
    return jnp.concatenate([w[:, :gates0], w[:, gates0 + 2 * HEADS:],
                            w[:, gates0:gates0 + HEADS], zpad,
                            w[:, gates0 + HEADS:gates0 + 2 * HEADS], zpad], axis=1).astype(BF16)


def kernel(x_prompt, x_sample, cache_k, cache_v, state_mlstm_C, state_mlstm_n, state_mlstm_m, state_conv,
           page_table, p_prompt, p_sample, g_mix, w_in, b_if, b_sb, conv_w, w_branch, w_out, g_mlp, w_up,
           w_down, g_ple, w_ple_gate, w_ple, g_final):
    depth = w_in.shape[0]
    batch, seq, _ = x_prompt.shape
    nseq = x_sample.shape[0]
    n_phys, page = cache_k.shape[1], cache_k.shape[2]
    cache_k = cache_k.reshape(depth, n_phys, page * HEADS, HEAD_DIM)
    cache_v = cache_v.reshape(depth, n_phys, page * HEADS, HEAD_DIM)
    hp = x_prompt.reshape(batch * seq, D_MODEL)
    hs = x_sample.reshape(nseq, D_MODEL)
    gfin = g_final.reshape(1, D_MODEL)
    tm_p = 512
    tm_s = nseq

    outs_p, outs_s = [], []
    for li in range(depth):
        final = li == depth - 1
        wp = _prep_w_in(w_in[li])
        gmix = g_mix[li].reshape(1, D_MODEL)
        bi = _pad_lanes(b_if[li, :HEADS].reshape(1, HEADS))
        bf = _pad_lanes(b_if[li, HEADS:].reshape(1, HEADS))
        post_w = (w_branch[li].astype(BF16), w_out[li].astype(BF16), g_mlp[li].reshape(1, D_MODEL),
                  w_up[li].astype(BF16), w_down[li].astype(BF16), g_ple[li].reshape(1, D_MODEL),
                  w_ple_gate[li].astype(BF16), w_ple[li].astype(BF16), gfin)

        pm, pc, sq, kf, vf, kvb, gt, gi, gf = _proj(hp, gmix, wp, tm_p)
        ym, yc, s_p, m_p, cv_p = _mlstm_conv(pm, pc, gi, gf, bi, bf, conv_w[li], batch, seq)
        ys = _sb_prompt(sq, kvb, b_sb[li], batch, seq)
        hp = _post(final, ym, yc, ys, gt, hp, p_prompt[li].reshape(batch * seq, PLE_DIM), *post_w, tm_p)
        outs_p.append((kf.reshape(batch, seq, HEADS, HEAD_DIM), vf.reshape(batch, seq, HEADS, HEAD_DIM),
                       s_p[..., :HEAD_DIM], s_p[..., HEAD_DIM], m_p[:, 0, :HEADS], cv_p))

        pm, pc, sq, kf, vf, kvb, gt, gi, gf = _proj(hs, gmix, wp, tm_s)
        m0 = _pad_lanes(state_mlstm_m[li]).reshape(nseq, 1, LANES)
        ym, yc, ys, c_s, n_s, m_s, cv_s = _decode(
            li, pm, pc, sq, gi, gf, state_mlstm_C, state_mlstm_n, m0, state_conv, bi, bf, conv_w[li],
            b_sb[li], page_table, cache_k, cache_v)
        hs = _post(final, ym.reshape(nseq, WIDTH), yc.reshape(nseq, WIDTH), ys.reshape(nseq, WIDTH), gt, hs,
                   p_sample[li].reshape(nseq, PLE_DIM), *post_w, tm_s)
        outs_s.append((kf.reshape(nseq, 1, HEADS, HEAD_DIM), vf.reshape(nseq, 1, HEADS, HEAD_DIM),
                       c_s, n_s, m_s[:, 0, :HEADS], cv_s))

    stack = lambda outs, j: jnp.stack([o[j] for o in outs])
    return ((hp.reshape(batch, seq, D_MODEL), hs.reshape(nseq, 1, D_MODEL))
            + tuple(stack(outs_p, j) for j in range(6)) + tuple(stack(outs_s, j) for j in range(6)))
```

```python
import functools

import jax
import jax.numpy as jnp
from jax import lax
from jax.experimental import pallas as pl
from jax.experimental.pallas import tpu as pltpu

F32 = jnp.float32
BF16 = jnp.bfloat16

D_MODEL = 1024
WIDTH = 512
HEADS = 4
HEAD_DIM = 128
FFN_DIM = 4 * D_MODEL
PLE_DIM = 256
RMS_EPS = 1e-6
LOG2E = 1.4426950408889634
LANES = 128

T_PC = 0
T_SQ = 1536
T_SK = 2048
T_SV = 2560
T_GT = 3072
N_HEAD_COLS = 2048
N_TAIL_COLS = 6144

MLSTM_CHUNK = 256
SB_BLOCK = 512
SB_K_BLOCK = 256
DECODE_SEQS_PER_STEP = 2
VMEM_LIMIT = 56 * 1024 * 1024


def _softplus(x):
    return jnp.maximum(x, 0.0) + jnp.log(1.0 + jnp.exp(-jnp.abs(x)))


def _stick_terms(z):
    drop = jnp.maximum(z, 0.0) + jnp.log2(1.0 + jnp.exp2(-jnp.abs(z)))
    return drop, z - drop


def _rms(x, g):
    return x * lax.rsqrt(jnp.mean(x * x, axis=-1, keepdims=True) + RMS_EPS) * g


def _resident(shape):
    nd = len(shape)
    return pl.BlockSpec(shape, lambda *_: (0,) * nd, pipeline_mode=pl.Buffered(1))


def _proj_kernel(first_layer, x_ref, g_ref, wh_ref, wt_ref, wif_ref, *refs):
    pm_ref, pc_ref, sq_ref, kf_ref, vf_ref, kvb_ref, gt_ref, gi_ref, gf_ref = refs[-9:]
    xn = _rms(x_ref[...], g_ref[...]).astype(BF16)

    def mm(w_ref, c0, width):
        return jnp.dot(xn, w_ref[:, c0:c0 + width], preferred_element_type=F32)

    scale = HEAD_DIM ** -0.5
    pm_ref[:, 0:512] = mm(wh_ref, 0, 512).astype(BF16)
    pm_ref[:, 512:1024] = (mm(wh_ref, 512, 512) * scale).astype(BF16)
    pm_ref[:, 1024:1536] = mm(wh_ref, 1024, 512).astype(BF16)
    pm_ref[:, 1536:2048] = mm(wh_ref, 1536, 512).astype(BF16)
    for j in range(3):
        pc_ref[:, j * 512:(j + 1) * 512] = mm(wt_ref, T_PC + j * 512, 512).astype(BF16)
    sq_ref[...] = (mm(wt_ref, T_SQ, 512) * (scale * LOG2E)).astype(BF16)
    k = mm(wt_ref, T_SK, 512)
    v = mm(wt_ref, T_SV, 512)
    tm = k.shape[0]
    slots = [kf_ref.at[l] for l in range(kf_ref.shape[0])] if first_layer else [kf_ref]
    vslots = [vf_ref.at[l] for l in range(vf_ref.shape[0])] if first_layer else [vf_ref]
    for h in range(HEADS):
        for kslot, vslot in zip(slots, vslots):
            kslot[pl.ds(h, tm, stride=HEADS), :] = k[:, h * HEAD_DIM:(h + 1) * HEAD_DIM]
            vslot[pl.ds(h, tm, stride=HEADS), :] = v[:, h * HEAD_DIM:(h + 1) * HEAD_DIM]
    kvb_ref[:, 0:512] = k.astype(BF16)
    kvb_ref[:, 512:1024] = v.astype(BF16)
    for j in range(6):
        gt_ref[:, j * 512:(j + 1) * 512] = mm(wt_ref, T_GT + j * 512, 512).astype(BF16)
    gi_ref[...] = mm(wif_ref, 0, LANES)
    gf_ref[...] = mm(wif_ref, LANES, LANES)


def _proj(x2d, g, wh, wt, wif, tm, layer, depth, kv_prev):
    m = x2d.shape[0]
    first = kv_prev is None
    row = lambda width: pl.BlockSpec((tm, width), lambda i: (i, 0))
    if first:
        kv_spec = pl.BlockSpec((depth, tm * HEADS, HEAD_DIM), lambda i: (0, i, 0))
    else:
        kv_spec = pl.BlockSpec((None, tm * HEADS, HEAD_DIM), lambda i: (layer, i, 0))
    widths = (2048, 1536, 512, None, None, 1024, 3072, LANES, LANES)
    dtypes = (BF16, BF16, BF16, F32, F32, BF16, BF16, F32, F32)
    out_specs = [row(w) if w else kv_spec for w in widths]
    out_shape = [jax.ShapeDtypeStruct((m, w) if w else (depth, m * HEADS, HEAD_DIM), d)
                 for w, d in zip(widths, dtypes)]
    in_specs = [row(D_MODEL), _resident((1, D_MODEL)), _resident((D_MODEL, N_HEAD_COLS)),
                _resident((D_MODEL, N_TAIL_COLS)), _resident((D_MODEL, 2 * LANES))]
    args = [x2d, g, wh, wt, wif]
    aliases = {}
    if not first:
        in_specs += [pl.BlockSpec(memory_space=pl.ANY)] * 2
        args += list(kv_prev)
        aliases = {5: 3, 6: 4}
    return pl.pallas_call(
        functools.partial(_proj_kernel, first),
        grid=(m // tm,),
        in_specs=in_specs,
        out_specs=out_specs,
        out_shape=out_shape,
        input_output_aliases=aliases,
        compiler_params=pltpu.CompilerParams(dimension_semantics=("parallel",),
                                             vmem_limit_bytes=VMEM_LIMIT),
        name="proj",
    )(*args)


def _scan_rows(x, op, fill):
    n = x.shape[0]
    row = lax.broadcasted_iota(jnp.int32, x.shape, 0)
    k = 1
    while k < n:
        shifted = pltpu.roll(x, k, axis=0)
        x = op(x, jnp.where(row >= k, shifted, fill(x)))
        k *= 2
    return x


def _mlstm_conv_kernel(pm_ref, pc_ref, gi_ref, gf_ref, bi_ref, bf_ref, cw_ref,
                       ym_ref, yc_ref, s_ref, m_ref, cs_ref):
    c = pl.program_id(1)
    L = pm_ref.shape[0]

    @pl.when(c == 0)
    def _():
        s_ref[...] = jnp.zeros_like(s_ref)
        m_ref[...] = jnp.zeros_like(m_ref)
        cs_ref[...] = jnp.zeros_like(cs_ref)

    cb = pc_ref[:, 0:512].astype(F32)
    xc = pc_ref[:, 512:1024].astype(F32) * pc_ref[:, 1024:1536].astype(F32)
    prev = cs_ref[...]
    row = lax.broadcasted_iota(jnp.int32, xc.shape, 0)
    xc1 = jnp.where(row == 0, prev[1:2, :], pltpu.roll(xc, 1, axis=0))
    xc2 = jnp.where(row == 0, prev[0:1, :], jnp.where(row == 1, prev[1:2, :], pltpu.roll(xc, 2, axis=0)))
    cw = cw_ref[...]
    yc_ref[...] = (cb * (cw[0:1, :] * xc2 + cw[1:2, :] * xc1 + cw[2:3, :] * xc)).astype(BF16)
    cs_ref[...] = xc[L - 2:L, :]

    m_prev = m_ref[...]
    ig = gi_ref[...] + bi_ref[...]
    logf = -_softplus(-(gf_ref[...] + bf_ref[...]))
    b = _scan_rows(logf, jnp.add, jnp.zeros_like)
    a = ig - b
    mx = jnp.maximum(m_prev, _scan_rows(a, jnp.maximum, lambda x: x))
    m_t = b + mx
    a_t = a.T
    s_in = jnp.exp(m_prev - mx)
    floor = jnp.exp(-m_t)
    mx_last = mx[L - 1:L, :]
    wk = jnp.exp(a - mx_last)
    decay = jnp.exp(m_prev - mx_last)
    m_ref[...] = m_t[L - 1:L, :]

    r_i = lax.broadcasted_iota(jnp.int32, (L, L), 0)
    c_i = lax.broadcasted_iota(jnp.int32, (L, L), 1)
    causal = c_i <= r_i
    one_col = (lax.broadcasted_iota(jnp.int32, (L, HEAD_DIM), 1) == 0).astype(BF16)
    for h in range(HEADS):
        sl = slice(h * HEAD_DIM, (h + 1) * HEAD_DIM)
        q = pm_ref[:, sl]
        k = pm_ref[:, 512 + h * HEAD_DIM:512 + (h + 1) * HEAD_DIM]
        v = pm_ref[:, 1024 + h * HEAD_DIM:1024 + (h + 1) * HEAD_DIM]
        o = pm_ref[:, 1536 + h * HEAD_DIM:1536 + (h + 1) * HEAD_DIM]
        vext = jnp.concatenate([v, one_col], axis=1)
        qk = lax.dot_general(q, k, (((1,), (1,)), ((), ())), preferred_element_type=F32)
        e = jnp.exp(a_t[h:h + 1, :] - mx[:, h:h + 1])
        w = jnp.where(causal, e, 0.0) * qk
        intra = jnp.dot(w.astype(BF16), vext, preferred_element_type=F32)
        state = s_ref[h]
        inter = jnp.dot(q, state.astype(BF16), preferred_element_type=F32)
        sc = s_in[:, h:h + 1]
        num = sc * inter[:, :HEAD_DIM] + intra[:, :HEAD_DIM]
        den = sc * inter[:, HEAD_DIM:HEAD_DIM + 1] + intra[:, HEAD_DIM:HEAD_DIM + 1]
        inv = 1.0 / jnp.maximum(jnp.abs(den), floor[:, h:h + 1])
        ym_ref[:, sl] = (jax.nn.sigmoid(o.astype(F32)) * (num * inv)).astype(BF16)
        kw_t = (k.astype(F32) * wk[:, h:h + 1]).T.astype(BF16)
        s_ref[h] = decay[:, h:h + 1] * state + jnp.dot(kw_t, vext, preferred_element_type=F32)


def _mlstm_conv(pm, pc, gi, gf, bi, bf, cw, batch, seq):
    L = MLSTM_CHUNK
    nc = seq // L
    row = lambda width: pl.BlockSpec((L, width), lambda b, c: (b * nc + c, 0))
    m = batch * seq
    return pl.pallas_call(
        _mlstm_conv_kernel,
        grid=(batch, nc),
        in_specs=[row(2048), row(1536), row(LANES), row(LANES),
                  _resident((1, LANES)), _resident((1, LANES)), _resident((3, WIDTH))],
        out_specs=[row(WIDTH), row(WIDTH),
                   pl.BlockSpec((None, HEADS, HEAD_DIM, 2 * HEAD_DIM), lambda b, c: (b, 0, 0, 0)),
                   pl.BlockSpec((None, 1, LANES), lambda b, c: (b, 0, 0)),
                   pl.BlockSpec((None, 2, WIDTH), lambda b, c: (b, 0, 0))],
        out_shape=[jax.ShapeDtypeStruct((m, WIDTH), BF16), jax.ShapeDtypeStruct((m, WIDTH), BF16),
                   jax.ShapeDtypeStruct((batch, HEADS, HEAD_DIM, 2 * HEAD_DIM), F32),
                   jax.ShapeDtypeStruct((batch, 1, LANES), F32),
                   jax.ShapeDtypeStruct((batch, 2, WIDTH), F32)],
        compiler_params=pltpu.CompilerParams(dimension_semantics=("parallel", "arbitrary"),
                                             vmem_limit_bytes=VMEM_LIMIT),
        name="mlstm_conv",
    )(pm, pc, gi, gf, bi, bf, cw)


def _sb_kernel(bsb_ref, q_ref, kv_ref, o_ref, acc_ref, run_ref):
    i = pl.program_id(1)
    tq = q_ref.shape[0]
    tk = SB_K_BLOCK
    r = tq // tk
    r_i = lax.broadcasted_iota(jnp.int32, (tk, tk), 0)
    c_i = lax.broadcasted_iota(jnp.int32, (tk, tk), 1)
    after = (r_i > c_i).astype(BF16)
    visible = c_i < r_i
    acc_ref[...] = jnp.zeros_like(acc_ref)
    run_ref[...] = jnp.zeros_like(run_ref)

    def tile(kb, rows, diagonal):
        k0 = pl.multiple_of(kb * tk, tk)
        for h in range(HEADS):
            sl = slice(h * HEAD_DIM, (h + 1) * HEAD_DIM)
            k = kv_ref[pl.ds(k0, tk), sl]
            v = kv_ref[pl.ds(k0, tk), WIDTH + h * HEAD_DIM:WIDTH + (h + 1) * HEAD_DIM]
            z = (lax.dot_general(q_ref[rows, sl], k, (((1,), (1,)), ((), ())), preferred_element_type=F32)
                 + bsb_ref[h] * LOG2E)
            drop, log_beta = _stick_terms(z)
            if diagonal:
                drop = jnp.where(visible, drop, 0.0)
            run = run_ref[h, rows]
            later = run + jnp.dot(drop.astype(BF16), after, preferred_element_type=F32)
            a = jnp.exp2(log_beta - later)
            if diagonal:
                a = jnp.where(visible, a, 0.0)
            acc_ref[rows, sl] += jnp.dot(a.astype(BF16), v, preferred_element_type=F32)
            run_ref[h, rows] = run + jnp.sum(drop, axis=1, keepdims=True)

    for d in reversed(range(r)):
        tile(i * r + d, slice(d * tk, (d + 1) * tk), True)
        if d + 1 < r:
            tile(i * r + d, slice((d + 1) * tk, tq), False)

    def body(j, carry):
        tile(i * r - 1 - j, slice(0, tq), False)
        return carry

    lax.fori_loop(0, i * r, body, 0)
    o_ref[...] = acc_ref[...].astype(BF16)


def _sb_prompt(sq, kvb, bsb, batch, seq):
    blk = SB_BLOCK
    nq = seq // blk
    return pl.pallas_call(
        _sb_kernel,
        grid=(batch, nq),
        in_specs=[pl.BlockSpec(memory_space=pltpu.SMEM),
                  pl.BlockSpec((blk, WIDTH), lambda b, i: (b * nq + i, 0)),
                  pl.BlockSpec((seq, 2 * WIDTH), lambda b, i: (b, 0))],
        out_specs=pl.BlockSpec((blk, WIDTH), lambda b, i: (b * nq + i, 0)),
        out_shape=jax.ShapeDtypeStruct((batch * seq, WIDTH), BF16),
        scratch_shapes=[pltpu.VMEM((blk, WIDTH), F32), pltpu.VMEM((HEADS, blk, 1), F32)],
        compiler_params=pltpu.CompilerParams(dimension_semantics=("parallel", "parallel"),
                                             vmem_limit_bytes=VMEM_LIMIT),
        name="sb_prompt",
    )(bsb, sq, kvb)


def _decode_kernel(n_pages, nsub, first_layer, pt_ref, bsb_ref, pm_ref, pc_ref, sq_ref, gi_ref, gf_ref,
                   c0_ref, n0_ref, m0_ref, cv0_ref, bi_ref, bf_ref, cw_ref, *refs):
    k_pages = refs[:nsub * n_pages]
    v_pages = refs[nsub * n_pages:2 * nsub * n_pages]
    ym_ref, yc_ref, ys_ref, c_ref, n_ref, m_ref, cv_ref = refs[-7:]
    del pt_ref
    c_slots = [c_ref.at[l] for l in range(c_ref.shape[0])] if first_layer else [c_ref]
    cw = cw_ref[...]
    pw = k_pages[0].shape[0]
    row128 = lax.broadcasted_iota(jnp.int32, (HEAD_DIM, HEAD_DIM), 0)
    r8 = lax.broadcasted_iota(jnp.int32, (8, HEAD_DIM), 0)
    rcol = lax.broadcasted_iota(jnp.int32, (8, 1), 0)
    bias = jnp.zeros((8, 1), F32)
    for h in range(HEADS):
        bias = jnp.where(rcol == h, bsb_ref[h] * LOG2E, bias)
    own = (lax.broadcasted_iota(jnp.int32, (8, pw), 1) % HEADS) == lax.broadcasted_iota(jnp.int32, (8, pw), 0)
    r_i = lax.broadcasted_iota(jnp.int32, (pw, pw), 0)
    c_i = lax.broadcasted_iota(jnp.int32, (pw, pw), 1)
    after = jnp.logical_and(r_i // HEADS > c_i // HEADS, r_i % HEADS == c_i % HEADS).astype(BF16)

    for s in range(nsub):
        cb = pc_ref[s, :, 0:512].astype(F32)
        xc = pc_ref[s, :, 512:1024].astype(F32) * pc_ref[s, :, 1024:1536].astype(F32)
        prev = cv0_ref[s]
        yc_ref[s] = (cb * (cw[0:1, :] * prev[0:1, :] + cw[1:2, :] * prev[1:2, :] + cw[2:3, :] * xc)).astype(BF16)
        cv_ref[s, 0:1, :] = prev[1:2, :]
        cv_ref[s, 1:2, :] = xc

        ig = gi_ref[s] + bi_ref[...]
        logf = -_softplus(-(gf_ref[s] + bf_ref[...]))
        m0 = m0_ref[s]
        m_t = jnp.maximum(logf + m0, ig)
        w_in = jnp.exp(ig - m_t)
        s_in = jnp.exp(logf + m0 - m_t)
        floor = jnp.exp(-m_t)
        m_ref[s] = m_t
        for h in range(HEADS):
            sl = slice(h * HEAD_DIM, (h + 1) * HEAD_DIM)
            q = pm_ref[s, :, sl].astype(F32)
            k = pm_ref[s, :, 512 + h * HEAD_DIM:512 + (h + 1) * HEAD_DIM].astype(F32)
            v = pm_ref[s, :, 1024 + h * HEAD_DIM:1024 + (h + 1) * HEAD_DIM].astype(F32)
            o = pm_ref[s, :, 1536 + h * HEAD_DIM:1536 + (h + 1) * HEAD_DIM].astype(F32)
            qk_rows = jnp.where(row128 == 0, q, jnp.where(row128 == 1, k, 0.0))
            cols = qk_rows.T
            q_col = cols[:, 0:1]
            k_col = cols[:, 1:2]
            c0 = c0_ref[s, h]
            n0 = n0_ref[s, h:h + 1, :]
            sc = s_in[:, h:h + 1]
            wi = w_in[:, h:h + 1]
            w = wi * jnp.sum(q * k, axis=1, keepdims=True)
            num = sc * jnp.sum(q_col * c0, axis=0, keepdims=True) + w * v
            den = sc * jnp.sum(q * n0, axis=1, keepdims=True) + w
            hc = num * (1.0 / jnp.maximum(jnp.abs(den), floor[:, h:h + 1]))
            ym_ref[s, :, sl] = (jax.nn.sigmoid(o) * hc).astype(BF16)
            c_new = sc * c0 + wi * (k_col * v)
            for slot in c_slots:
                slot[s, h] = c_new
            n_ref[s, h:h + 1, :] = sc * n0 + wi * k

        sq = sq_ref[s].astype(F32)
        q_rows = jnp.zeros((8, HEAD_DIM), F32)
        for h in range(HEADS):
            q_rows = jnp.where(r8 == h, sq[:, h * HEAD_DIM:(h + 1) * HEAD_DIM], q_rows)
        q_rows = q_rows.astype(BF16)
        drops, log_betas = [], []
        for j in range(n_pages):
            kp = k_pages[s * n_pages + j][...].astype(BF16)
            z = lax.dot_general(q_rows, kp, (((1,), (1,)), ((), ())), preferred_element_type=F32) + bias
            drop, log_beta = _stick_terms(z)
            drops.append(jnp.where(own, drop, 0.0))
            log_betas.append(log_beta)
        drop_all = jnp.concatenate(drops, axis=0)
        within = jnp.dot(drop_all.astype(BF16), after, preferred_element_type=F32)
        totals = jnp.sum(drop_all, axis=1, keepdims=True)
        run = jnp.zeros((8, 1), F32)
        out = jnp.zeros((8, HEAD_DIM), F32)
        for j in reversed(range(n_pages)):
            later = within[8 * j:8 * j + 8, :] + run
            a = jnp.where(own, jnp.exp2(log_betas[j] - later), 0.0)
            vp = v_pages[s * n_pages + j][...].astype(BF16)
            out = out + jnp.dot(a.astype(BF16), vp, preferred_element_type=F32)
            run = run + totals[8 * j:8 * j + 8, :]
        for h in range(HEADS):
            ys_ref[s, :, h * HEAD_DIM:(h + 1) * HEAD_DIM] = out[h:h + 1, :].astype(BF16)


def _decode(layer, depth, pm, pc, sq, gi, gf, state_c, state_n, m0, state_conv, bi, bf, cw, bsb,
            page_table, cache_k, cache_v, c_prev):
    nseq, n_pages = page_table.shape
    pw = cache_k.shape[2]
    nsub = DECODE_SEQS_PER_STEP
    first = c_prev is None
    row = lambda width: pl.BlockSpec((nsub, 1, width), lambda g, pt: (g, 0, 0))
    fixed = lambda shape: pl.BlockSpec(shape, lambda g, pt: (0,) * len(shape), pipeline_mode=pl.Buffered(1))
    state4 = pl.BlockSpec((None, nsub, HEADS, HEAD_DIM, HEAD_DIM), lambda g, pt: (layer, g, 0, 0, 0))
    state3 = pl.BlockSpec((None, nsub, HEADS, HEAD_DIM), lambda g, pt: (layer, g, 0, 0))
    conv3 = pl.BlockSpec((None, nsub, 2, WIDTH), lambda g, pt: (layer, g, 0, 0))

    def page_spec(s, j):
        return pl.BlockSpec((None, None, pw, HEAD_DIM), lambda g, pt: (layer, pt[g * nsub + s, j], 0, 0))

    pages = [page_spec(s, j) for s in range(nsub) for j in range(n_pages)]
    in_specs = ([pl.BlockSpec(memory_space=pltpu.SMEM),
                 row(2048), row(1536), row(WIDTH), row(LANES), row(LANES),
                 state4, state3, row(LANES), conv3,
                 fixed((1, LANES)), fixed((1, LANES)), fixed((3, WIDTH))] + pages * 2)
    args = [page_table, bsb] + [a.reshape(nseq, 1, a.shape[-1]) for a in (pm, pc, sq, gi, gf)] + [
        state_c, state_n, m0, state_conv, bi, bf, cw] + [cache_k] * len(pages) + [cache_v] * len(pages)
    if first:
        c_spec = pl.BlockSpec((depth, nsub, HEADS, HEAD_DIM, HEAD_DIM), lambda g, pt: (0, g, 0, 0, 0))
        aliases = {}
    else:
        c_spec = pl.BlockSpec((None, nsub, HEADS, HEAD_DIM, HEAD_DIM), lambda g, pt: (layer, g, 0, 0, 0))
        in_specs.append(pl.BlockSpec(memory_space=pl.ANY))
        args.append(c_prev)
        aliases = {len(args) - 1: 3}
    out_specs = [row(WIDTH), row(WIDTH), row(WIDTH), c_spec,
                 pl.BlockSpec((nsub, HEADS, HEAD_DIM), lambda g, pt: (g, 0, 0)),
                 row(LANES),
                 pl.BlockSpec((nsub, 2, WIDTH), lambda g, pt: (g, 0, 0))]
    out_shape = [jax.ShapeDtypeStruct((nseq, 1, WIDTH), BF16)] * 3 + [
        jax.ShapeDtypeStruct((depth, nseq, HEADS, HEAD_DIM, HEAD_DIM), F32),
        jax.ShapeDtypeStruct((nseq, HEADS, HEAD_DIM), F32),
        jax.ShapeDtypeStruct((nseq, 1, LANES), F32),
        jax.ShapeDtypeStruct((nseq, 2, WIDTH), F32)]
    return pl.pallas_call(
        functools.partial(_decode_kernel, n_pages, nsub, first),
        grid_spec=pltpu.PrefetchScalarGridSpec(
            num_scalar_prefetch=1, grid=(nseq // nsub,), in_specs=in_specs, out_specs=out_specs),
        out_shape=out_shape,
        input_output_aliases=aliases,
        compiler_params=pltpu.CompilerParams(dimension_semantics=("parallel",),
                                             vmem_limit_bytes=VMEM_LIMIT),
        name="decode",
    )(*args)


def _post_kernel(final, ym_ref, yc_ref, ys_ref, gt_ref, h_ref, p_ref, wb_ref, wo_ref, gmlp_ref, wup_ref,
                 wdn_ref, gple_ref, wpg_ref, wple_ref, gfin_ref, o_ref):
    mix = None
    for n, y_ref in enumerate((ym_ref, yc_ref, ys_ref)):
        pb = jnp.dot(y_ref[...], wb_ref[n], preferred_element_type=F32)
        gate = jax.nn.sigmoid(gt_ref[:, n * D_MODEL:(n + 1) * D_MODEL].astype(F32))
        mix = gate * pb if mix is None else mix + gate * pb
    h = h_ref[...] + jnp.dot(mix.astype(BF16), wo_ref[...], preferred_element_type=F32)

    xn = _rms(h, gmlp_ref[...]).astype(BF16)
    chunk = 1024
    for c in range(FFN_DIM // chunk):
        u = jnp.dot(xn, wup_ref[:, c * chunk:(c + 1) * chunk], preferred_element_type=F32)
        r = jnp.square(jnp.maximum(u, 0.0)).astype(BF16)
        h = h + jnp.dot(r, wdn_ref[c * chunk:(c + 1) * chunk, :], preferred_element_type=F32)

    gate = jax.nn.sigmoid(jnp.dot(_rms(h, gple_ref[...]).astype(BF16), wpg_ref[...],
                                  preferred_element_type=F32))
    h = h + jnp.dot(p_ref[...].astype(BF16), wple_ref[...], preferred_element_type=F32) * gate
    o_ref[...] = _rms(h, gfin_ref[...]) if final else h


def _post(final, layer, ym, yc, ys, gt, h, p, wb, wo, gmlp, wup, wdn, gple, wpg, wple, gfin, tm):
    m = h.shape[0]
    row = lambda width: pl.BlockSpec((tm, width), lambda i: (i, 0))
    p_spec = pl.BlockSpec((None, tm, PLE_DIM), lambda i: (layer, i, 0))
    return pl.pallas_call(
        functools.partial(_post_kernel, final),
        grid=(m // tm,),
        in_specs=[row(WIDTH), row(WIDTH), row(WIDTH), row(3 * D_MODEL), row(D_MODEL), p_spec,
                  _resident((3, WIDTH, D_MODEL)), _resident((D_MODEL, D_MODEL)), _resident((1, D_MODEL)),
                  _resident((D_MODEL, FFN_DIM)), _resident((FFN_DIM, D_MODEL)), _resident((1, D_MODEL)),
                  _resident((D_MODEL, D_MODEL)), _resident((PLE_DIM, D_MODEL)), _resident((1, D_MODEL))],
        out_specs=row(D_MODEL),
        out_shape=jax.ShapeDtypeStruct((m, D_MODEL), F32),
        compiler_params=pltpu.CompilerParams(dimension_semantics=("parallel",),
                                             vmem_limit_bytes=VMEM_LIMIT),
        name="post",
    )(ym, yc, ys, gt, h, p, wb, wo, gmlp, wup, wdn, gple, wpg, wple, gfin)


def _pad_lanes(x):
    return jnp.pad(x, ((0, 0), (0, LANES - x.shape[-1])))


def _prep_w_in(w):
    gates0 = 4 * WIDTH
    zpad = jnp.zeros((w.shape[0], LANES - HEADS), w.dtype)
    wif = jnp.concatenate([w[:, gates0:gates0 + HEADS], zpad, w[:, gates0 + HEADS:gates0 + 2 * HEADS], zpad], axis=1)
    return w[:, :gates0].astype(BF16), w[:, gates0 + 2 * HEADS:].astype(BF16), wif.astype(BF16)


def kernel(x_prompt, x_sample, cache_k, cache_v, state_mlstm_C, state_mlstm_n, state_mlstm_m, state_conv,
           page_table, p_prompt, p_sample, g_mix, w_in, b_if, b_sb, conv_w, w_branch, w_out, g_mlp, w_up,
           w_down, g_ple, w_ple_gate, w_ple, g_final):
    depth = w_in.shape[0]
    batch, seq, _ = x_prompt.shape
    nseq = x_sample.shape[0]
    n_phys, page = cache_k.shape[1], cache_k.shape[2]
    cache_k = cache_k.reshape(depth, n_phys, page * HEADS, HEAD_DIM)
    cache_v = cache_v.reshape(depth, n_phys, page * HEADS, HEAD_DIM)
    hp = x_prompt.reshape(batch * seq, D_MODEL)
    hs = x_sample.reshape(nseq, D_MODEL)
    pp = p_prompt.reshape(depth, batch * seq, PLE_DIM)
    ps = p_sample.reshape(depth, nseq, PLE_DIM)
    gfin = g_final.reshape(1, D_MODEL)
    tm_p = 512
    tm_s = nseq

    outs_p, outs_s = [], []
    kv_p = kv_s = c_s = None
    for li in range(depth):
        final = li == depth - 1
        wh, wt, wif = _prep_w_in(w_in[li])
        gmix = g_mix[li].reshape(1, D_MODEL)
        bi = _pad_lanes(b_if[li, :HEADS].reshape(1, HEADS))
        bf = _pad_lanes(b_if[li, HEADS:].reshape(1, HEADS))
        post_w = (w_branch[li].astype(BF16), w_out[li].astype(BF16), g_mlp[li].reshape(1, D_MODEL),
                  w_up[li].astype(BF16), w_down[li].astype(BF16), g_ple[li].reshape(1, D_MODEL),
                  w_ple_gate[li].astype(BF16), w_ple[li].astype(BF16), gfin)

        pm, pc, sq, kf, vf, kvb, gt, gi, gf = _proj(hp, gmix, wh, wt, wif, tm_p, li, depth, kv_p)
        kv_p = (kf, vf)
        ym, yc, s_p, m_p, cv_p = _mlstm_conv(pm, pc, gi, gf, bi, bf, conv_w[li], batch, seq)
        ys = _sb_prompt(sq, kvb, b_sb[li], batch, seq)
        hp = _post(final, li, ym, yc, ys, gt, hp, pp, *post_w, tm_p)
        outs_p.append((s_p[..., :HEAD_DIM], s_p[..., HEAD_DIM], m_p[:, 0, :HEADS], cv_p))

        pm, pc, sq, kf, vf, kvb, gt, gi, gf = _proj(hs, gmix, wh, wt, wif, tm_s, li, depth, kv_s)
        kv_s = (kf, vf)
        m0 = _pad_lanes(state_mlstm_m[li]).reshape(nseq, 1, LANES)
        ym, yc, ys, c_s, n_s, m_s, cv_s = _decode(
            li, depth, pm, pc, sq, gi, gf, state_mlstm_C, state_mlstm_n, m0, state_conv, bi, bf, conv_w[li],
            b_sb[li], page_table, cache_k, cache_v, c_s)
        hs = _post(final, li, ym.reshape(nseq, WIDTH), yc.reshape(nseq, WIDTH), ys.reshape(nseq, WIDTH), gt, hs,
                   ps, *post_w, tm_s)
        outs_s.append((n_s, m_s[:, 0, :HEADS], cv_s))

    stack = lambda outs, j: jnp.stack([o[j] for o in outs])
    k_p, v_p = (a.reshape(depth, batch, seq, HEADS, HEAD_DIM) for a in kv_p)
    k_s, v_s = (a.reshape(depth, nseq, 1, HEADS, HEAD_DIM) for a in kv_s)
    return ((hp.reshape(batch, seq, D_MODEL), hs.reshape(nseq, 1, D_MODEL), k_p, v_p)
            + tuple(stack(outs_p, j) for j in range(4)) + (k_s, v_s, c_s)
            + tuple(stack(outs_s, j) for j in range(3)))
```

```python
import functools

import jax
import jax.numpy as jnp
from jax import lax
from jax.experimental import pallas as pl
from jax.experimental.pallas import tpu as pltpu

F32 = jnp.float32
BF16 = jnp.bfloat16

D_MODEL = 1024
WIDTH = 512
HEADS = 4
HEAD_DIM = 128
FFN_DIM = 4 * D_MODEL
PLE_DIM = 256
RMS_EPS = 1e-6
LOG2E = 1.4426950408889634
LANES = 128

T_PC = 0
T_SQ = 1536
T_SK = 2048
T_SV = 2560
T_GT = 3072
N_HEAD_COLS = 2048
N_TAIL_COLS = 6144

MLSTM_CHUNK = 256
SB_BLOCK = 512
SB_K_BLOCK = 256
VMEM_LIMIT = 56 * 1024 * 1024


def _softplus(x):
    return jnp.maximum(x, 0.0) + jnp.log(1.0 + jnp.exp(-jnp.abs(x)))


def _stick_terms(z):
    drop = jnp.maximum(z, 0.0) + jnp.log2(1.0 + jnp.exp2(-jnp.abs(z)))
    return drop, z - drop


def _rms(x, g):
    return x * lax.rsqrt(jnp.mean(x * x, axis=-1, keepdims=True) + RMS_EPS) * g


def _resident(shape):
    nd = len(shape)
    return pl.BlockSpec(shape, lambda *_: (0,) * nd, pipeline_mode=pl.Buffered(1))


def _proj_kernel(first_layer, x_ref, g_ref, wh_ref, wt_ref, wif_ref, *refs):
    pm_ref, pc_ref, sq_ref, kf_ref, vf_ref, kvb_ref, gt_ref, gi_ref, gf_ref = refs[-9:]
    xn = _rms(x_ref[...], g_ref[...]).astype(BF16)

    def mm(w_ref, c0, width):
        return jnp.dot(xn, w_ref[:, c0:c0 + width], preferred_element_type=F32)

    scale = HEAD_DIM ** -0.5
    pm_ref[:, 0:512] = mm(wh_ref, 0, 512).astype(BF16)
    pm_ref[:, 512:1024] = (mm(wh_ref, 512, 512) * scale).astype(BF16)
    pm_ref[:, 1024:1536] = mm(wh_ref, 1024, 512).astype(BF16)
    pm_ref[:, 1536:2048] = mm(wh_ref, 1536, 512).astype(BF16)
    for j in range(3):
        pc_ref[:, j * 512:(j + 1) * 512] = mm(wt_ref, T_PC + j * 512, 512).astype(BF16)
    sq_ref[...] = (mm(wt_ref, T_SQ, 512) * (scale * LOG2E)).astype(BF16)
    k = mm(wt_ref, T_SK, 512)
    v = mm(wt_ref, T_SV, 512)
    tm = k.shape[0]
    slots = [kf_ref.at[l] for l in range(kf_ref.shape[0])] if first_layer else [kf_ref]
    vslots = [vf_ref.at[l] for l in range(vf_ref.shape[0])] if first_layer else [vf_ref]
    for h in range(HEADS):
        for kslot, vslot in zip(slots, vslots):
            kslot[pl.ds(h, tm, stride=HEADS), :] = k[:, h * HEAD_DIM:(h + 1) * HEAD_DIM]
            vslot[pl.ds(h, tm, stride=HEADS), :] = v[:, h * HEAD_DIM:(h + 1) * HEAD_DIM]
    kvb_ref[:, 0:512] = k.astype(BF16)
    kvb_ref[:, 512:1024] = v.astype(BF16)
    for j in range(6):
        gt_ref[:, j * 512:(j + 1) * 512] = mm(wt_ref, T_GT + j * 512, 512).astype(BF16)
    gi_ref[...] = mm(wif_ref, 0, LANES)
    gf_ref[...] = mm(wif_ref, LANES, LANES)


def _proj(x2d, g, wh, wt, wif, tm, layer, depth, kv_prev):
    m = x2d.shape[0]
    first = kv_prev is None
    row = lambda width: pl.BlockSpec((tm, width), lambda i: (i, 0))
    if first:
        kv_spec = pl.BlockSpec((depth, tm * HEADS, HEAD_DIM), lambda i: (0, i, 0))
    else:
        kv_spec = pl.BlockSpec((None, tm * HEADS, HEAD_DIM), lambda i: (layer, i, 0))
    widths = (2048, 1536, 512, None, None, 1024, 3072, LANES, LANES)
    dtypes = (BF16, BF16, BF16, F32, F32, BF16, BF16, F32, F32)
    out_specs = [row(w) if w else kv_spec for w in widths]
    out_shape = [jax.ShapeDtypeStruct((m, w) if w else (depth, m * HEADS, HEAD_DIM), d)
                 for w, d in zip(widths, dtypes)]
    in_specs = [row(D_MODEL), _resident((1, D_MODEL)), _resident((D_MODEL, N_HEAD_COLS)),
                _resident((D_MODEL, N_TAIL_COLS)), _resident((D_MODEL, 2 * LANES))]
    args = [x2d, g, wh, wt, wif]
    aliases = {}
    if not first:
        in_specs += [pl.BlockSpec(memory_space=pl.ANY)] * 2
        args += list(kv_prev)
        aliases = {5: 3, 6: 4}
    return pl.pallas_call(
        functools.partial(_proj_kernel, first),
        grid=(m // tm,),
        in_specs=in_specs,
        out_specs=out_specs,
        out_shape=out_shape,
        input_output_aliases=aliases,
        compiler_params=pltpu.CompilerParams(dimension_semantics=("parallel",),
                                             vmem_limit_bytes=VMEM_LIMIT),
        name="proj",
    )(*args)


def _scan_rows(x, op, fill):
    n = x.shape[0]
    row = lax.broadcasted_iota(jnp.int32, x.shape, 0)
    k = 1
    while k < n:
        shifted = pltpu.roll(x, k, axis=0)
        x = op(x, jnp.where(row >= k, shifted, fill(x)))
        k *= 2
    return x


def _mlstm_conv_kernel(pm_ref, pc_ref, gi_ref, gf_ref, bi_ref, bf_ref, cw_ref,
                       ym_ref, yc_ref, s_ref, m_ref, cs_ref):
    c = pl.program_id(1)
    L = pm_ref.shape[0]

    @pl.when(c == 0)
    def _():
        s_ref[...] = jnp.zeros_like(s_ref)
        m_ref[...] = jnp.zeros_like(m_ref)
        cs_ref[...] = jnp.zeros_like(cs_ref)

    cb = pc_ref[:, 0:512].astype(F32)
    xc = pc_ref[:, 512:1024].astype(F32) * pc_ref[:, 1024:1536].astype(F32)
    prev = cs_ref[...]
    row = lax.broadcasted_iota(jnp.int32, xc.shape, 0)
    xc1 = jnp.where(row == 0, prev[1:2, :], pltpu.roll(xc, 1, axis=0))
    xc2 = jnp.where(row == 0, prev[0:1, :], jnp.where(row == 1, prev[1:2, :], pltpu.roll(xc, 2, axis=0)))
    cw = cw_ref[...]
    yc_ref[...] = (cb * (cw[0:1, :] * xc2 + cw[1:2, :] * xc1 + cw[2:3, :] * xc)).astype(BF16)
    cs_ref[...] = xc[L - 2:L, :]

    m_prev = m_ref[...]
    ig = gi_ref[...] + bi_ref[...]
    logf = -_softplus(-(gf_ref[...] + bf_ref[...]))
    b = _scan_rows(logf, jnp.add, jnp.zeros_like)
    a = ig - b
    mx = jnp.maximum(m_prev, _scan_rows(a, jnp.maximum, lambda x: x))
    m_t = b + mx
    a_t = a.T
    s_in = jnp.exp(m_prev - mx)
    floor = jnp.exp(-m_t)
    mx_last = mx[L - 1:L, :]
    wk = jnp.exp(a - mx_last)
    decay = jnp.exp(m_prev - mx_last)
    m_ref[...] = m_t[L - 1:L, :]

    r_i = lax.broadcasted_iota(jnp.int32, (L, L), 0)
    c_i = lax.broadcasted_iota(jnp.int32, (L, L), 1)
    causal = c_i <= r_i
    one_col = (lax.broadcasted_iota(jnp.int32, (L, HEAD_DIM), 1) == 0).astype(BF16)
    for h in range(HEADS):
        sl = slice(h * HEAD_DIM, (h + 1) * HEAD_DIM)
        q = pm_ref[:, sl]
        k = pm_ref[:, 512 + h * HEAD_DIM:512 + (h + 1) * HEAD_DIM]
        v = pm_ref[:, 1024 + h * HEAD_DIM:1024 + (h + 1) * HEAD_DIM]
        o = pm_ref[:, 1536 + h * HEAD_DIM:1536 + (h + 1) * HEAD_DIM]
        vext = jnp.concatenate([v, one_col], axis=1)
        qk = lax.dot_general(q, k, (((1,), (1,)), ((), ())), preferred_element_type=F32)
        e = jnp.exp(a_t[h:h + 1, :] - mx[:, h:h + 1])
        w = jnp.where(causal, e, 0.0) * qk
        intra = jnp.dot(w.astype(BF16), vext, preferred_element_type=F32)
        state = s_ref[h]
        inter = jnp.dot(q, state.astype(BF16), preferred_element_type=F32)
        sc = s_in[:, h:h + 1]
        num = sc * inter[:, :HEAD_DIM] + intra[:, :HEAD_DIM]
        den = sc * inter[:, HEAD_DIM:HEAD_DIM + 1] + intra[:, HEAD_DIM:HEAD_DIM + 1]
        inv = 1.0 / jnp.maximum(jnp.abs(den), floor[:, h:h + 1])
        ym_ref[:, sl] = (jax.nn.sigmoid(o.astype(F32)) * (num * inv)).astype(BF16)
        kw_t = (k.astype(F32) * wk[:, h:h + 1]).T.astype(BF16)
        s_ref[h] = decay[:, h:h + 1] * state + jnp.dot(kw_t, vext, preferred_element_type=F32)


def _mlstm_conv(pm, pc, gi, gf, bi, bf, cw, batch, seq):
    L = MLSTM_CHUNK
    nc = seq // L
    row = lambda width: pl.BlockSpec((L, width), lambda b, c: (b * nc + c, 0))
    m = batch * seq
    return pl.pallas_call(
        _mlstm_conv_kernel,
        grid=(batch, nc),
        in_specs=[row(2048), row(1536), row(LANES), row(LANES),
                  _resident((1, LANES)), _resident((1, LANES)), _resident((3, WIDTH))],
        out_specs=[row(WIDTH), row(WIDTH),
                   pl.BlockSpec((None, HEADS, HEAD_DIM, 2 * HEAD_DIM), lambda b, c: (b, 0, 0, 0)),
                   pl.BlockSpec((None, 1, LANES), lambda b, c: (b, 0, 0)),
                   pl.BlockSpec((None, 2, WIDTH), lambda b, c: (b, 0, 0))],
        out_shape=[jax.ShapeDtypeStruct((m, WIDTH), BF16), jax.ShapeDtypeStruct((m, WIDTH), BF16),
                   jax.ShapeDtypeStruct((batch, HEADS, HEAD_DIM, 2 * HEAD_DIM), F32),
                   jax.ShapeDtypeStruct((batch, 1, LANES), F32),
                   jax.ShapeDtypeStruct((batch, 2, WIDTH), F32)],
        compiler_params=pltpu.CompilerParams(dimension_semantics=("parallel", "arbitrary"),
                                             vmem_limit_bytes=VMEM_LIMIT),
        name="mlstm_conv",
    )(pm, pc, gi, gf, bi, bf, cw)


def _sb_kernel(bsb_ref, q_ref, kv_ref, o_ref, acc_ref, run_ref):
    i = pl.program_id(1)
    tq = q_ref.shape[0]
    tk = SB_K_BLOCK
    r = tq // tk
    r_i = lax.broadcasted_iota(jnp.int32, (tk, tk), 0)
    c_i = lax.broadcasted_iota(jnp.int32, (tk, tk), 1)
    after = (r_i > c_i).astype(BF16)
    visible = c_i < r_i
    acc_ref[...] = jnp.zeros_like(acc_ref)
    run_ref[...] = jnp.zeros_like(run_ref)

    def tile(kb, rows, diagonal):
        k0 = pl.multiple_of(kb * tk, tk)
        for h in range(HEADS):
            sl = slice(h * HEAD_DIM, (h + 1) * HEAD_DIM)
            k = kv_ref[pl.ds(k0, tk), sl]
            v = kv_ref[pl.ds(k0, tk), WIDTH + h * HEAD_DIM:WIDTH + (h + 1) * HEAD_DIM]
            z = (lax.dot_general(q_ref[rows, sl], k, (((1,), (1,)), ((), ())), preferred_element_type=F32)
                 + bsb_ref[h] * LOG2E)
            drop, log_beta = _stick_terms(z)
            if diagonal:
                drop = jnp.where(visible, drop, 0.0)
            run = run_ref[h, rows]
            later = run + jnp.dot(drop.astype(BF16), after, preferred_element_type=F32)
            a = jnp.exp2(log_beta - later)
            if diagonal:
                a = jnp.where(visible, a, 0.0)
            acc_ref[rows, sl] += jnp.dot(a.astype(BF16), v, preferred_element_type=F32)
            run_ref[h, rows] = run + jnp.sum(drop, axis=1, keepdims=True)

    for d in reversed(range(r)):
        tile(i * r + d, slice(d * tk, (d + 1) * tk), True)
        if d + 1 < r:
            tile(i * r + d, slice((d + 1) * tk, tq), False)

    def body(j, carry):
        tile(i * r - 1 - j, slice(0, tq), False)
        return carry

    lax.fori_loop(0, i * r, body, 0)
    o_ref[...] = acc_ref[...].astype(BF16)


def _sb_prompt(sq, kvb, bsb, batch, seq):
    blk = SB_BLOCK
    nq = seq // blk
    return pl.pallas_call(
        _sb_kernel,
        grid=(batch, nq),
        in_specs=[pl.BlockSpec(memory_space=pltpu.SMEM),
                  pl.BlockSpec((blk, WIDTH), lambda b, i: (b * nq + i, 0)),
                  pl.BlockSpec((seq, 2 * WIDTH), lambda b, i: (b, 0))],
        out_specs=pl.BlockSpec((blk, WIDTH), lambda b, i: (b * nq + i, 0)),
        out_shape=jax.ShapeDtypeStruct((batch * seq, WIDTH), BF16),
        scratch_shapes=[pltpu.VMEM((blk, WIDTH), F32), pltpu.VMEM((HEADS, blk, 1), F32)],
        compiler_params=pltpu.CompilerParams(dimension_semantics=("parallel", "parallel"),
                                             vmem_limit_bytes=VMEM_LIMIT),
        name="sb_prompt",
    )(bsb, sq, kvb)


def _decode_seq(bsb_ref, pm_ref, pc_ref, sq_ref, gi_ref, gf_ref, c0_ref, n0_ref, m0_ref, cv0_ref, bi_ref,
                bf_ref, cw_ref, k_pages, v_pages, ym_ref, yc_ref, ys_ref, c_slots, n_ref, m_ref, cv_ref):
    n_pages = len(k_pages)
    cw = cw_ref[...]
    pw = k_pages[0].shape[0]

    cb = pc_ref[0, :, 0:512].astype(F32)
    xc = pc_ref[0, :, 512:1024].astype(F32) * pc_ref[0, :, 1024:1536].astype(F32)
    prev = cv0_ref[0]
    yc_ref[0] = (cb * (cw[0:1, :] * prev[0:1, :] + cw[1:2, :] * prev[1:2, :] + cw[2:3, :] * xc)).astype(BF16)
    cv_ref[0, 0:1, :] = prev[1:2, :]
    cv_ref[0, 1:2, :] = xc

    ig = gi_ref[0] + bi_ref[...]
    logf = -_softplus(-(gf_ref[0] + bf_ref[...]))
    m0 = m0_ref[0]
    m_t = jnp.maximum(logf + m0, ig)
    w_in = jnp.exp(ig - m_t)
    s_in = jnp.exp(logf + m0 - m_t)
    floor = jnp.exp(-m_t)
    m_ref[0] = m_t
    row128 = lax.broadcasted_iota(jnp.int32, (HEAD_DIM, HEAD_DIM), 0)
    for h in range(HEADS):
        sl = slice(h * HEAD_DIM, (h + 1) * HEAD_DIM)
        q = pm_ref[0, :, sl].astype(F32)
        k = pm_ref[0, :, 512 + h * HEAD_DIM:512 + (h + 1) * HEAD_DIM].astype(F32)
        v = pm_ref[0, :, 1024 + h * HEAD_DIM:1024 + (h + 1) * HEAD_DIM].astype(F32)
        o = pm_ref[0, :, 1536 + h * HEAD_DIM:1536 + (h + 1) * HEAD_DIM].astype(F32)
        qk_rows = jnp.where(row128 == 0, q, jnp.where(row128 == 1, k, 0.0))
        cols = qk_rows.T
        q_col = cols[:, 0:1]
        k_col = cols[:, 1:2]
        c0 = c0_ref[0, h]
        n0 = n0_ref[0, h:h + 1, :]
        sc = s_in[:, h:h + 1]
        wi = w_in[:, h:h + 1]
        w = wi * jnp.sum(q * k, axis=1, keepdims=True)
        num = sc * jnp.sum(q_col * c0, axis=0, keepdims=True) + w * v
        den = sc * jnp.sum(q * n0, axis=1, keepdims=True) + w
        hc = num * (1.0 / jnp.maximum(jnp.abs(den), floor[:, h:h + 1]))
        ym_ref[0, :, sl] = (jax.nn.sigmoid(o) * hc).astype(BF16)
        c_new = sc * c0 + wi * (k_col * v)
        for slot in c_slots:
            slot[0, h] = c_new
        n_ref[0, h:h + 1, :] = sc * n0 + wi * k

    r8 = lax.broadcasted_iota(jnp.int32, (8, HEAD_DIM), 0)
    rcol = lax.broadcasted_iota(jnp.int32, (8, 1), 0)
    sq = sq_ref[0].astype(F32)
    q_rows = jnp.zeros((8, HEAD_DIM), F32)
    bias = jnp.zeros((8, 1), F32)
    for h in range(HEADS):
        q_rows = jnp.where(r8 == h, sq[:, h * HEAD_DIM:(h + 1) * HEAD_DIM], q_rows)
        bias = jnp.where(rcol == h, bsb_ref[h] * LOG2E, bias)
    q_rows = q_rows.astype(BF16)
    own = (lax.broadcasted_iota(jnp.int32, (8, pw), 1) % HEADS) == lax.broadcasted_iota(jnp.int32, (8, pw), 0)
    r_i = lax.broadcasted_iota(jnp.int32, (pw, pw), 0)
    c_i = lax.broadcasted_iota(jnp.int32, (pw, pw), 1)
    after = jnp.logical_and(r_i // HEADS > c_i // HEADS, r_i % HEADS == c_i % HEADS).astype(BF16)
    k_all = jnp.concatenate([kp[...].astype(BF16) for kp in k_pages], axis=0)
    z = lax.dot_general(q_rows, k_all, (((1,), (1,)), ((), ())), preferred_element_type=F32) + bias
    drop, log_beta = _stick_terms(z)
    own_all = jnp.concatenate([own] * n_pages, axis=1)
    drop = jnp.where(own_all, drop, 0.0)
    drop_rows = jnp.concatenate([drop[:, j * pw:(j + 1) * pw] for j in range(n_pages)], axis=0)
    within = jnp.dot(drop_rows.astype(BF16), after, preferred_element_type=F32)
    totals = jnp.sum(drop_rows, axis=1, keepdims=True)
    run = jnp.zeros((8, 1), F32)
    laters = [None] * n_pages
    for j in reversed(range(n_pages)):
        laters[j] = within[8 * j:8 * j + 8, :] + run
        run = run + totals[8 * j:8 * j + 8, :]
    later = jnp.concatenate(laters, axis=1)
    a = jnp.where(own_all, jnp.exp2(log_beta - later), 0.0)
    v_all = jnp.concatenate([vp[...].astype(BF16) for vp in v_pages], axis=0)
    out = jnp.dot(a.astype(BF16), v_all, preferred_element_type=F32)
    for h in range(HEADS):
        ys_ref[0, :, h * HEAD_DIM:(h + 1) * HEAD_DIM] = out[h:h + 1, :].astype(BF16)


def _post_kernel(final, ym_ref, yc_ref, ys_ref, gt_ref, h_ref, p_ref, wb_ref, wo_ref, gmlp_ref, wup_ref,
                 wdn_ref, gple_ref, wpg_ref, wple_ref, gfin_ref, o_ref):
    mix = None
    for n, y_ref in enumerate((ym_ref, yc_ref, ys_ref)):
        pb = jnp.dot(y_ref[...], wb_ref[n], preferred_element_type=F32)
        gate = jax.nn.sigmoid(gt_ref[:, n * D_MODEL:(n + 1) * D_MODEL].astype(F32))
        mix = gate * pb if mix is None else mix + gate * pb
    h = h_ref[...] + jnp.dot(mix.astype(BF16), wo_ref[...], preferred_element_type=F32)

    xn = _rms(h, gmlp_ref[...]).astype(BF16)
    chunk = 1024
    for c in range(FFN_DIM // chunk):
        u = jnp.dot(xn, wup_ref[:, c * chunk:(c + 1) * chunk], preferred_element_type=F32)
        r = jnp.square(jnp.maximum(u, 0.0)).astype(BF16)
        h = h + jnp.dot(r, wdn_ref[c * chunk:(c + 1) * chunk, :], preferred_element_type=F32)

    gate = jax.nn.sigmoid(jnp.dot(_rms(h, gple_ref[...]).astype(BF16), wpg_ref[...],
                                  preferred_element_type=F32))
    h = h + jnp.dot(p_ref[...].astype(BF16), wple_ref[...], preferred_element_type=F32) * gate
    o_ref[...] = _rms(h, gfin_ref[...]) if final else h


def _post(final, layer, ym, yc, ys, gt, h, p, wb, wo, gmlp, wup, wdn, gple, wpg, wple, gfin, tm):
    m = h.shape[0]
    row = lambda width: pl.BlockSpec((tm, width), lambda i: (i, 0))
    p_spec = pl.BlockSpec((None, tm, PLE_DIM), lambda i: (layer, i, 0))
    return pl.pallas_call(
        functools.partial(_post_kernel, final),
        grid=(m // tm,),
        in_specs=[row(WIDTH), row(WIDTH), row(WIDTH), row(3 * D_MODEL), row(D_MODEL), p_spec,
                  _resident((3, WIDTH, D_MODEL)), _resident((D_MODEL, D_MODEL)), _resident((1, D_MODEL)),
                  _resident((D_MODEL, FFN_DIM)), _resident((FFN_DIM, D_MODEL)), _resident((1, D_MODEL)),
                  _resident((D_MODEL, D_MODEL)), _resident((PLE_DIM, D_MODEL)), _resident((1, D_MODEL))],
        out_specs=row(D_MODEL),
        out_shape=jax.ShapeDtypeStruct((m, D_MODEL), F32),
        compiler_params=pltpu.CompilerParams(dimension_semantics=("parallel",),
                                             vmem_limit_bytes=VMEM_LIMIT),
        name="post",
    )(ym, yc, ys, gt, h, p, wb, wo, gmlp, wup, wdn, gple, wpg, wple, gfin)


N_POST_IN = 15
N_DECODE_IN = 12


def _post_decode_kernel(final, n_pages, first_layer, pt_ref, bsb_ref, *refs):
    del pt_ref
    (ym_ref, yc_ref, ys_ref, gt_ref, h_ref, p_ref, wb_ref, wo_ref, gmlp_ref, wup_ref, wdn_ref, gple_ref,
     wpg_ref, wple_ref, gfin_ref) = refs[:N_POST_IN]
    dec_in = refs[N_POST_IN:N_POST_IN + N_DECODE_IN]
    k_pages = refs[N_POST_IN + N_DECODE_IN:N_POST_IN + N_DECODE_IN + n_pages]
    v_pages = refs[N_POST_IN + N_DECODE_IN + n_pages:N_POST_IN + N_DECODE_IN + 2 * n_pages]
    o_ref, ym_s_ref, yc_s_ref, ys_s_ref, c_ref, n_ref, m_ref, cv_ref, acc_ref, xn_ref = refs[-10:]
    c_slots = [c_ref.at[l] for l in range(c_ref.shape[0])] if first_layer else [c_ref]
    half = FFN_DIM // 2
    chunk = 1024

    def decode():
        _decode_seq(bsb_ref, *dec_in, k_pages, v_pages, ym_s_ref, yc_s_ref, ys_s_ref, c_slots, n_ref, m_ref,
                    cv_ref)

    def mlp_half(xn, h, base):
        for c in range(half // chunk):
            lo = base + c * chunk
            u = jnp.dot(xn, wup_ref[:, lo:lo + chunk], preferred_element_type=F32)
            r = jnp.square(jnp.maximum(u, 0.0)).astype(BF16)
            h = h + jnp.dot(r, wdn_ref[lo:lo + chunk, :], preferred_element_type=F32)
        return h

    parity = pl.program_id(0) % 2

    @pl.when(parity == 0)
    def _():
        decode()
        mix = None
        for n, y_ref in enumerate((ym_ref, yc_ref, ys_ref)):
            pb = jnp.dot(y_ref[...], wb_ref[n], preferred_element_type=F32)
            gate = jax.nn.sigmoid(gt_ref[:, n * D_MODEL:(n + 1) * D_MODEL].astype(F32))
            mix = gate * pb if mix is None else mix + gate * pb
        h = h_ref[...] + jnp.dot(mix.astype(BF16), wo_ref[...], preferred_element_type=F32)
        xn = _rms(h, gmlp_ref[...]).astype(BF16)
        xn_ref[...] = xn
        acc_ref[...] = mlp_half(xn, h, 0)

    @pl.when(parity == 1)
    def _():
        decode()
        h = mlp_half(xn_ref[...], acc_ref[...], half)
        gate = jax.nn.sigmoid(jnp.dot(_rms(h, gple_ref[...]).astype(BF16), wpg_ref[...],
                                      preferred_element_type=F32))
        h = h + jnp.dot(p_ref[...].astype(BF16), wple_ref[...], preferred_element_type=F32) * gate
        o_ref[...] = _rms(h, gfin_ref[...]) if final else h


def _post_decode(final, layer, depth, ym, yc, ys, gt, h, p, post_w, pm, pc, sq, gi, gf, state_c, state_n, m0,
                 state_conv, bi, bf, cw, bsb, page_table, cache_k, cache_v, c_prev):
    nseq, n_pages = page_table.shape
    pw = cache_k.shape[2]
    m = h.shape[0]
    tm = 2 * m // nseq
    assert tm * nseq == 2 * m and tm % 16 == 0
    first = c_prev is None
    prow = lambda width: pl.BlockSpec((tm, width), lambda t, pt: (t // 2, 0))
    fixed = lambda shape: pl.BlockSpec(shape, lambda t, pt: (0,) * len(shape), pipeline_mode=pl.Buffered(1))
    srow = lambda width: pl.BlockSpec((1, 1, width), lambda t, pt: (t, 0, 0))
    state4 = pl.BlockSpec((None, 1, HEADS, HEAD_DIM, HEAD_DIM), lambda t, pt: (layer, t, 0, 0, 0))
    state3 = pl.BlockSpec((None, 1, HEADS, HEAD_DIM), lambda t, pt: (layer, t, 0, 0))
    conv3 = pl.BlockSpec((None, 1, 2, WIDTH), lambda t, pt: (layer, t, 0, 0))

    def page_spec(j):
        return pl.BlockSpec((None, None, pw, HEAD_DIM), lambda t, pt: (layer, pt[t, j], 0, 0))

    pages = [page_spec(j) for j in range(n_pages)]
    post_specs = [prow(WIDTH), prow(WIDTH), prow(WIDTH), prow(3 * D_MODEL), prow(D_MODEL),
                  pl.BlockSpec((None, tm, PLE_DIM), lambda t, pt: (layer, t // 2, 0)),
                  fixed((3, WIDTH, D_MODEL)), fixed((D_MODEL, D_MODEL)), fixed((1, D_MODEL)),
                  fixed((D_MODEL, FFN_DIM)), fixed((FFN_DIM, D_MODEL)), fixed((1, D_MODEL)),
                  fixed((D_MODEL, D_MODEL)), fixed((PLE_DIM, D_MODEL)), fixed((1, D_MODEL))]
    dec_specs = [srow(2048), srow(1536), srow(WIDTH), srow(LANES), srow(LANES), state4, state3, srow(LANES),
                 conv3, fixed((1, LANES)), fixed((1, LANES)), fixed((3, WIDTH))]
    assert len(post_specs) == N_POST_IN and len(dec_specs) == N_DECODE_IN
    in_specs = [pl.BlockSpec(memory_space=pltpu.SMEM)] + post_specs + dec_specs + pages * 2
    args = ([page_table, bsb, ym, yc, ys, gt, h, p, *post_w]
            + [a.reshape(nseq, 1, a.shape[-1]) for a in (pm, pc, sq, gi, gf)]
            + [state_c, state_n, m0, state_conv, bi, bf, cw] + [cache_k] * n_pages + [cache_v] * n_pages)
    if first:
        c_spec = pl.BlockSpec((depth, 1, HEADS, HEAD_DIM, HEAD_DIM), lambda t, pt: (0, t, 0, 0, 0))
        aliases = {}
    else:
        c_spec = pl.BlockSpec((None, 1, HEADS, HEAD_DIM, HEAD_DIM), lambda t, pt: (layer, t, 0, 0, 0))
        in_specs.append(pl.BlockSpec(memory_space=pl.ANY))
        args.append(c_prev)
        aliases = {len(args) - 1: 4}
    out_specs = [prow(D_MODEL), srow(WIDTH), srow(WIDTH), srow(WIDTH), c_spec,
                 pl.BlockSpec((1, HEADS, HEAD_DIM), lambda t, pt: (t, 0, 0)),
                 srow(LANES),
                 pl.BlockSpec((1, 2, WIDTH), lambda t, pt: (t, 0, 0))]
    out_shape = [jax.ShapeDtypeStruct((m, D_MODEL), F32)] + [jax.ShapeDtypeStruct((nseq, 1, WIDTH), BF16)] * 3 + [
        jax.ShapeDtypeStruct((depth, nseq, HEADS, HEAD_DIM, HEAD_DIM), F32),
        jax.ShapeDtypeStruct((nseq, HEADS, HEAD_DIM), F32),
        jax.ShapeDtypeStruct((nseq, 1, LANES), F32),
        jax.ShapeDtypeStruct((nseq, 2, WIDTH), F32)]
    return pl.pallas_call(
        functools.partial(_post_decode_kernel, final, n_pages, first),
        grid_spec=pltpu.PrefetchScalarGridSpec(
            num_scalar_prefetch=1, grid=(nseq,), in_specs=in_specs, out_specs=out_specs,
            scratch_shapes=[pltpu.VMEM((tm, D_MODEL), F32), pltpu.VMEM((tm, D_MODEL), BF16)]),
        out_shape=out_shape,
        input_output_aliases=aliases,
        compiler_params=pltpu.CompilerParams(dimension_semantics=("arbitrary",),
                                             vmem_limit_bytes=VMEM_LIMIT),
        name="post_decode",
    )(*args)


def _pad_lanes(x):
    return jnp.pad(x, ((0, 0), (0, LANES - x.shape[-1])))


def _prep_w_in(w):
    gates0 = 4 * WIDTH
    zpad = jnp.zeros((w.shape[0], LANES - HEADS), w.dtype)
    wif = jnp.concatenate([w[:, gates0:gates0 + HEADS], zpad, w[:, gates0 + HEADS:gates0 + 2 * HEADS], zpad], axis=1)
    return w[:, :gates0].astype(BF16), w[:, gates0 + 2 * HEADS:].astype(BF16), wif.astype(BF16)


def kernel(x_prompt, x_sample, cache_k, cache_v, state_mlstm_C, state_mlstm_n, state_mlstm_m, state_conv,
           page_table, p_prompt, p_sample, g_mix, w_in, b_if, b_sb, conv_w, w_branch, w_out, g_mlp, w_up,
           w_down, g_ple, w_ple_gate, w_ple, g_final):
    depth = w_in.shape[0]
    batch, seq, _ = x_prompt.shape
    nseq = x_sample.shape[0]
    n_phys, page = cache_k.shape[1], cache_k.shape[2]
    cache_k = cache_k.reshape(depth, n_phys, page * HEADS, HEAD_DIM)
    cache_v = cache_v.reshape(depth, n_phys, page * HEADS, HEAD_DIM)
    hp = x_prompt.reshape(batch * seq, D_MODEL)
    hs = x_sample.reshape(nseq, D_MODEL)
    pp = p_prompt.reshape(depth, batch * seq, PLE_DIM)
    ps = p_sample.reshape(depth, nseq, PLE_DIM)
    gfin = g_final.reshape(1, D_MODEL)
    tm_p = 512
    tm_s = nseq

    outs_p, outs_s = [], []
    kv_p = kv_s = c_s = None
    for li in range(depth):
        final = li == depth - 1
        wh, wt, wif = _prep_w_in(w_in[li])
        gmix = g_mix[li].reshape(1, D_MODEL)
        bi = _pad_lanes(b_if[li, :HEADS].reshape(1, HEADS))
        bf = _pad_lanes(b_if[li, HEADS:].reshape(1, HEADS))
        post_w = (w_branch[li].astype(BF16), w_out[li].astype(BF16), g_mlp[li].reshape(1, D_MODEL),
                  w_up[li].astype(BF16), w_down[li].astype(BF16), g_ple[li].reshape(1, D_MODEL),
                  w_ple_gate[li].astype(BF16), w_ple[li].astype(BF16), gfin)

        pm, pc, sq, kf, vf, kvb, gt, gi, gf = _proj(hp, gmix, wh, wt, wif, tm_p, li, depth, kv_p)
        kv_p = (kf, vf)
        ym, yc, s_p, m_p, cv_p = _mlstm_conv(pm, pc, gi, gf, bi, bf, conv_w[li], batch, seq)
        ys = _sb_prompt(sq, kvb, b_sb[li], batch, seq)
        outs_p.append((s_p[..., :HEAD_DIM], s_p[..., HEAD_DIM], m_p[:, 0, :HEADS], cv_p))
        pm_s, pc_s, sq_s, kf, vf, _, gt_s, gi_s, gf_s = _proj(hs, gmix, wh, wt, wif, tm_s, li, depth, kv_s)
        kv_s = (kf, vf)
        m0 = _pad_lanes(state_mlstm_m[li]).reshape(nseq, 1, LANES)

        hp, ym_s, yc_s, ys_s, c_s, n_s, m_s, cv_s = _post_decode(
            final, li, depth, ym, yc, ys, gt, hp, pp, post_w, pm_s, pc_s, sq_s, gi_s, gf_s, state_mlstm_C,
            state_mlstm_n, m0, state_conv, bi, bf, conv_w[li], b_sb[li], page_table, cache_k, cache_v, c_s)
        hs = _post(final, li, ym_s.reshape(nseq, WIDTH), yc_s.reshape(nseq, WIDTH), ys_s.reshape(nseq, WIDTH),
                   gt_s, hs, ps, *post_w, tm_s)
        outs_s.append((n_s, m_s[:, 0, :HEADS], cv_s))

    stack = lambda outs, j: jnp.stack([o[j] for o in outs])
    k_p, v_p = (a.reshape(depth, batch, seq, HEADS, HEAD_DIM) for a in kv_p)
    k_s, v_s = (a.reshape(depth, nseq, 1, HEADS, HEAD_DIM) for a in kv_s)
    return ((hp.reshape(batch, seq, D_MODEL), hs.reshape(nseq, 1, D_MODEL), k_p, v_p)
            + tuple(stack(outs_p, j) for j in range(4)) + (k_s, v_s, c_s)
            + tuple(stack(outs_s, j) for j in range(3)))
```

```python
import functools

import jax
import jax.numpy as jnp
from jax import lax
from jax.experimental import pallas as pl
from jax.experimental.pallas import tpu as pltpu

F32 = jnp.float32
BF16 = jnp.bfloat16

D_MODEL = 1024
WIDTH = 512
HEADS = 4
HEAD_DIM = 128
FFN_DIM = 4 * D_MODEL
PLE_DIM = 256
RMS_EPS = 1e-6
LOG2E = 1.4426950408889634
LANES = 128

T_PC = 0
T_SQ = 1536
T_SK = 2048
T_SV = 2560
T_GT = 3072
N_HEAD_COLS = 2048
N_TAIL_COLS = 6144

MLSTM_CHUNK = 256
SB_BLOCK = 512
SB_K_BLOCK = 256
VMEM_LIMIT = 56 * 1024 * 1024


def _softplus(x):
    return jnp.maximum(x, 0.0) + jnp.log(1.0 + jnp.exp(-jnp.abs(x)))


def _stick_terms(z):
    drop = jnp.maximum(z, 0.0) + jnp.log2(1.0 + jnp.exp2(-jnp.abs(z)))
    return drop, z - drop


def _rms(x, g):
    return x * lax.rsqrt(jnp.mean(x * x, axis=-1, keepdims=True) + RMS_EPS) * g


def _resident(shape):
    nd = len(shape)
    return pl.BlockSpec(shape, lambda *_: (0,) * nd, pipeline_mode=pl.Buffered(1))


def _proj_kernel(first_layer, x_ref, g_ref, wh_ref, wt_ref, wif_ref, *refs):
    pm_ref, pc_ref, sq_ref, kf_ref, vf_ref, kvb_ref, gt_ref, gi_ref, gf_ref = refs[-9:]
    xn = _rms(x_ref[...], g_ref[...]).astype(BF16)

    def mm(w_ref, c0, width):
        return jnp.dot(xn, w_ref[:, c0:c0 + width], preferred_element_type=F32)

    scale = HEAD_DIM ** -0.5
    pm_ref[:, 0:512] = mm(wh_ref, 0, 512).astype(BF16)
    pm_ref[:, 512:1024] = (mm(wh_ref, 512, 512) * scale).astype(BF16)
    pm_ref[:, 1024:1536] = mm(wh_ref, 1024, 512).astype(BF16)
    pm_ref[:, 1536:2048] = mm(wh_ref, 1536, 512).astype(BF16)
    for j in range(3):
        pc_ref[:, j * 512:(j + 1) * 512] = mm(wt_ref, T_PC + j * 512, 512).astype(BF16)
    sq_ref[...] = (mm(wt_ref, T_SQ, 512) * (scale * LOG2E)).astype(BF16)
    k = mm(wt_ref, T_SK, 512)
    v = mm(wt_ref, T_SV, 512)
    tm = k.shape[0]
    slots = [kf_ref.at[l] for l in range(kf_ref.shape[0])] if first_layer else [kf_ref]
    vslots = [vf_ref.at[l] for l in range(vf_ref.shape[0])] if first_layer else [vf_ref]
    for h in range(HEADS):
        for kslot, vslot in zip(slots, vslots):
            kslot[pl.ds(h, tm, stride=HEADS), :] = k[:, h * HEAD_DIM:(h + 1) * HEAD_DIM]
            vslot[pl.ds(h, tm, stride=HEADS), :] = v[:, h * HEAD_DIM:(h + 1) * HEAD_DIM]
    kvb_ref[:, 0:512] = k.astype(BF16)
    kvb_ref[:, 512:1024] = v.astype(BF16)
    for j in range(6):
        gt_ref[:, j * 512:(j + 1) * 512] = mm(wt_ref, T_GT + j * 512, 512).astype(BF16)
    gi_ref[...] = mm(wif_ref, 0, LANES)
    gf_ref[...] = mm(wif_ref, LANES, LANES)


def _proj(x2d, g, wh, wt, wif, tm, layer, depth, kv_prev):
    m = x2d.shape[0]
    first = kv_prev is None
    row = lambda width: pl.BlockSpec((tm, width), lambda i: (i, 0))
    if first:
        kv_spec = pl.BlockSpec((depth, tm * HEADS, HEAD_DIM), lambda i: (0, i, 0))
    else:
        kv_spec = pl.BlockSpec((None, tm * HEADS, HEAD_DIM), lambda i: (layer, i, 0))
    widths = (2048, 1536, 512, None, None, 1024, 3072, LANES, LANES)
    dtypes = (BF16, BF16, BF16, F32, F32, BF16, BF16, F32, F32)
    out_specs = [row(w) if w else kv_spec for w in widths]
    out_shape = [jax.ShapeDtypeStruct((m, w) if w else (depth, m * HEADS, HEAD_DIM), d)
                 for w, d in zip(widths, dtypes)]
    in_specs = [row(D_MODEL), _resident((1, D_MODEL)), _resident((D_MODEL, N_HEAD_COLS)),
                _resident((D_MODEL, N_TAIL_COLS)), _resident((D_MODEL, 2 * LANES))]
    args = [x2d, g, wh, wt, wif]
    aliases = {}
    if not first:
        in_specs += [pl.BlockSpec(memory_space=pl.ANY)] * 2
        args += list(kv_prev)
        aliases = {5: 3, 6: 4}
    return pl.pallas_call(
        functools.partial(_proj_kernel, first),
        grid=(m // tm,),
        in_specs=in_specs,
        out_specs=out_specs,
        out_shape=out_shape,
        input_output_aliases=aliases,
        compiler_params=pltpu.CompilerParams(dimension_semantics=("parallel",),
                                             vmem_limit_bytes=VMEM_LIMIT),
        name="proj",
    )(*args)


def _scan_rows(x, op, fill):
    n = x.shape[0]
    row = lax.broadcasted_iota(jnp.int32, x.shape, 0)
    k = 1
    while k < n:
        shifted = pltpu.roll(x, k, axis=0)
        x = op(x, jnp.where(row >= k, shifted, fill(x)))
        k *= 2
    return x


def _mlstm_conv_kernel(pm_ref, pc_ref, gi_ref, gf_ref, bi_ref, bf_ref, cw_ref,
                       ym_ref, yc_ref, s_ref, m_ref, cs_ref):
    c = pl.program_id(1)
    L = pm_ref.shape[0]

    @pl.when(c == 0)
    def _():
        s_ref[...] = jnp.zeros_like(s_ref)
        m_ref[...] = jnp.zeros_like(m_ref)
        cs_ref[...] = jnp.zeros_like(cs_ref)

    cb = pc_ref[:, 0:512].astype(F32)
    xc = pc_ref[:, 512:1024].astype(F32) * pc_ref[:, 1024:1536].astype(F32)
    prev = cs_ref[...]
    row = lax.broadcasted_iota(jnp.int32, xc.shape, 0)
    xc1 = jnp.where(row == 0, prev[1:2, :], pltpu.roll(xc, 1, axis=0))
    xc2 = jnp.where(row == 0, prev[0:1, :], jnp.where(row == 1, prev[1:2, :], pltpu.roll(xc, 2, axis=0)))
    cw = cw_ref[...]
    yc_ref[...] = (cb * (cw[0:1, :] * xc2 + cw[1:2, :] * xc1 + cw[2:3, :] * xc)).astype(BF16)
    cs_ref[...] = xc[L - 2:L, :]

    m_prev = m_ref[...]
    ig = gi_ref[...] + bi_ref[...]
    logf = -_softplus(-(gf_ref[...] + bf_ref[...]))
    b = _scan_rows(logf, jnp.add, jnp.zeros_like)
    a = ig - b
    mx = jnp.maximum(m_prev, _scan_rows(a, jnp.maximum, lambda x: x))
    m_t = b + mx
    a_t = a.T
    s_in = jnp.exp(m_prev - mx)
    floor = jnp.exp(-m_t)
    mx_last = mx[L - 1:L, :]
    wk = jnp.exp(a - mx_last)
    decay = jnp.exp(m_prev - mx_last)
    m_ref[...] = m_t[L - 1:L, :]

    r_i = lax.broadcasted_iota(jnp.int32, (L, L), 0)
    c_i = lax.broadcasted_iota(jnp.int32, (L, L), 1)
    causal = c_i <= r_i
    one_col = (lax.broadcasted_iota(jnp.int32, (L, HEAD_DIM), 1) == 0).astype(BF16)
    for h in range(HEADS):
        sl = slice(h * HEAD_DIM, (h + 1) * HEAD_DIM)
        q = pm_ref[:, sl]
        k = pm_ref[:, 512 + h * HEAD_DIM:512 + (h + 1) * HEAD_DIM]
        v = pm_ref[:, 1024 + h * HEAD_DIM:1024 + (h + 1) * HEAD_DIM]
        o = pm_ref[:, 1536 + h * HEAD_DIM:1536 + (h + 1) * HEAD_DIM]
        vext = jnp.concatenate([v, one_col], axis=1)
        qk = lax.dot_general(q, k, (((1,), (1,)), ((), ())), preferred_element_type=F32)
        e = jnp.exp(a_t[h:h + 1, :] - mx[:, h:h + 1])
        w = jnp.where(causal, e, 0.0) * qk
        intra = jnp.dot(w.astype(BF16), vext, preferred_element_type=F32)
        state = s_ref[h]
        inter = jnp.dot(q, state.astype(BF16), preferred_element_type=F32)
        sc = s_in[:, h:h + 1]
        num = sc * inter[:, :HEAD_DIM] + intra[:, :HEAD_DIM]
        den = sc * inter[:, HEAD_DIM:HEAD_DIM + 1] + intra[:, HEAD_DIM:HEAD_DIM + 1]
        inv = 1.0 / jnp.maximum(jnp.abs(den), floor[:, h:h + 1])
        ym_ref[:, sl] = (jax.nn.sigmoid(o.astype(F32)) * (num * inv)).astype(BF16)
        kw_t = (k.astype(F32) * wk[:, h:h + 1]).T.astype(BF16)
        s_ref[h] = decay[:, h:h + 1] * state + jnp.dot(kw_t, vext, preferred_element_type=F32)


def _mlstm_conv(pm, pc, gi, gf, bi, bf, cw, batch, seq):
    L = MLSTM_CHUNK
    nc = seq // L
    row = lambda width: pl.BlockSpec((L, width), lambda b, c: (b * nc + c, 0))
    m = batch * seq
    return pl.pallas_call(
        _mlstm_conv_kernel,
        grid=(batch, nc),
        in_specs=[row(2048), row(1536), row(LANES), row(LANES),
                  _resident((1, LANES)), _resident((1, LANES)), _resident((3, WIDTH))],
        out_specs=[row(WIDTH), row(WIDTH),
                   pl.BlockSpec((None, HEADS, HEAD_DIM, 2 * HEAD_DIM), lambda b, c: (b, 0, 0, 0)),
                   pl.BlockSpec((None, 1, LANES), lambda b, c: (b, 0, 0)),
                   pl.BlockSpec((None, 2, WIDTH), lambda b, c: (b, 0, 0))],
        out_shape=[jax.ShapeDtypeStruct((m, WIDTH), BF16), jax.ShapeDtypeStruct((m, WIDTH), BF16),
                   jax.ShapeDtypeStruct((batch, HEADS, HEAD_DIM, 2 * HEAD_DIM), F32),
                   jax.ShapeDtypeStruct((batch, 1, LANES), F32),
                   jax.ShapeDtypeStruct((batch, 2, WIDTH), F32)],
        compiler_params=pltpu.CompilerParams(dimension_semantics=("parallel", "arbitrary"),
                                             vmem_limit_bytes=VMEM_LIMIT),
        name="mlstm_conv",
    )(pm, pc, gi, gf, bi, bf, cw)


def _sb_kernel(bsb_ref, q_ref, kv_ref, o_ref, acc_ref, run_ref):
    i = pl.program_id(1)
    tq = q_ref.shape[0]
    tk = SB_K_BLOCK
    r = tq // tk
    r_i = lax.broadcasted_iota(jnp.int32, (tk, tk), 0)
    c_i = lax.broadcasted_iota(jnp.int32, (tk, tk), 1)
    after = (r_i > c_i).astype(BF16)
    visible = c_i < r_i
    acc_ref[...] = jnp.zeros_like(acc_ref)
    run_ref[...] = jnp.zeros_like(run_ref)

    def tile(kb, rows, diagonal):
        k0 = pl.multiple_of(kb * tk, tk)
        for h in range(HEADS):
            sl = slice(h * HEAD_DIM, (h + 1) * HEAD_DIM)
            k = kv_ref[pl.ds(k0, tk), sl]
            v = kv_ref[pl.ds(k0, tk), WIDTH + h * HEAD_DIM:WIDTH + (h + 1) * HEAD_DIM]
            z = (lax.dot_general(q_ref[rows, sl], k, (((1,), (1,)), ((), ())), preferred_element_type=F32)
                 + bsb_ref[h] * LOG2E)
            drop, log_beta = _stick_terms(z)
            if diagonal:
                drop = jnp.where(visible, drop, 0.0)
            run = run_ref[h, rows]
            later = run + jnp.dot(drop.astype(BF16), after, preferred_element_type=F32)
            a = jnp.exp2(log_beta - later)
            if diagonal:
                a = jnp.where(visible, a, 0.0)
            acc_ref[rows, sl] += jnp.dot(a.astype(BF16), v, preferred_element_type=F32)
            run_ref[h, rows] = run + jnp.sum(drop, axis=1, keepdims=True)

    for d in reversed(range(r)):
        tile(i * r + d, slice(d * tk, (d + 1) * tk), True)
        if d + 1 < r:
            tile(i * r + d, slice((d + 1) * tk, tq), False)

    def body(j, carry):
        tile(i * r - 1 - j, slice(0, tq), False)
        return carry

    lax.fori_loop(0, i * r, body, 0)
    o_ref[...] = acc_ref[...].astype(BF16)


def _sb_prompt(sq, kvb, bsb, batch, seq):
    blk = SB_BLOCK
    nq = seq // blk
    return pl.pallas_call(
        _sb_kernel,
        grid=(batch, nq),
        in_specs=[pl.BlockSpec(memory_space=pltpu.SMEM),
                  pl.BlockSpec((blk, WIDTH), lambda b, i: (b * nq + i, 0)),
                  pl.BlockSpec((seq, 2 * WIDTH), lambda b, i: (b, 0))],
        out_specs=pl.BlockSpec((blk, WIDTH), lambda b, i: (b * nq + i, 0)),
        out_shape=jax.ShapeDtypeStruct((batch * seq, WIDTH), BF16),
        scratch_shapes=[pltpu.VMEM((blk, WIDTH), F32), pltpu.VMEM((HEADS, blk, 1), F32)],
        compiler_params=pltpu.CompilerParams(dimension_semantics=("parallel", "parallel"),
                                             vmem_limit_bytes=VMEM_LIMIT),
        name="sb_prompt",
    )(bsb, sq, kvb)


def _decode_seq(bsb_ref, pm_ref, pc_ref, sq_ref, gi_ref, gf_ref, c0_ref, n0_ref, m0_ref, cv0_ref, bi_ref,
                bf_ref, cw_ref, k_pages, v_pages, ym_ref, yc_ref, ys_ref, c_slots, n_ref, m_ref, cv_ref):
    n_pages = len(k_pages)
    cw = cw_ref[...]
    pw = k_pages[0].shape[0]

    cb = pc_ref[0, :, 0:512].astype(F32)
    xc = pc_ref[0, :, 512:1024].astype(F32) * pc_ref[0, :, 1024:1536].astype(F32)
    prev = cv0_ref[0]
    yc_ref[0] = (cb * (cw[0:1, :] * prev[0:1, :] + cw[1:2, :] * prev[1:2, :] + cw[2:3, :] * xc)).astype(BF16)
    cv_ref[0, 0:1, :] = prev[1:2, :]
    cv_ref[0, 1:2, :] = xc

    ig = gi_ref[0] + bi_ref[...]
    logf = -_softplus(-(gf_ref[0] + bf_ref[...]))
    m0 = m0_ref[0]
    m_t = jnp.maximum(logf + m0, ig)
    w_in = jnp.exp(ig - m_t)
    s_in = jnp.exp(logf + m0 - m_t)
    floor = jnp.exp(-m_t)
    m_ref[0] = m_t
    row128 = lax.broadcasted_iota(jnp.int32, (HEAD_DIM, HEAD_DIM), 0)
    for h in range(HEADS):
        sl = slice(h * HEAD_DIM, (h + 1) * HEAD_DIM)
        q = pm_ref[0, :, sl].astype(F32)
        k = pm_ref[0, :, 512 + h * HEAD_DIM:512 + (h + 1) * HEAD_DIM].astype(F32)
        v = pm_ref[0, :, 1024 + h * HEAD_DIM:1024 + (h + 1) * HEAD_DIM].astype(F32)
        o = pm_ref[0, :, 1536 + h * HEAD_DIM:1536 + (h + 1) * HEAD_DIM].astype(F32)
        qk_rows = jnp.where(row128 == 0, q, jnp.where(row128 == 1, k, 0.0))
        cols = qk_rows.T
        q_col = cols[:, 0:1]
        k_col = cols[:, 1:2]
        c0 = c0_ref[0, h]
        n0 = n0_ref[0, h:h + 1, :]
        sc = s_in[:, h:h + 1]
        wi = w_in[:, h:h + 1]
        w = wi * jnp.sum(q * k, axis=1, keepdims=True)
        num = sc * jnp.sum(q_col * c0, axis=0, keepdims=True) + w * v
        den = sc * jnp.sum(q * n0, axis=1, keepdims=True) + w
        hc = num * (1.0 / jnp.maximum(jnp.abs(den), floor[:, h:h + 1]))
        ym_ref[0, :, sl] = (jax.nn.sigmoid(o) * hc).astype(BF16)
        c_new = sc * c0 + wi * (k_col * v)
        for slot in c_slots:
            slot[0, h] = c_new
        n_ref[0, h:h + 1, :] = sc * n0 + wi * k

    r8 = lax.broadcasted_iota(jnp.int32, (8, HEAD_DIM), 0)
    rcol = lax.broadcasted_iota(jnp.int32, (8, 1), 0)
    sq = sq_ref[0].astype(F32)
    q_rows = jnp.zeros((8, HEAD_DIM), F32)
    bias = jnp.zeros((8, 1), F32)
    for h in range(HEADS):
        q_rows = jnp.where(r8 == h, sq[:, h * HEAD_DIM:(h + 1) * HEAD_DIM], q_rows)
        bias = jnp.where(rcol == h, bsb_ref[h] * LOG2E, bias)
    q_rows = q_rows.astype(BF16)
    own = (lax.broadcasted_iota(jnp.int32, (8, pw), 1) % HEADS) == lax.broadcasted_iota(jnp.int32, (8, pw), 0)
    r_i = lax.broadcasted_iota(jnp.int32, (pw, pw), 0)
    c_i = lax.broadcasted_iota(jnp.int32, (pw, pw), 1)
    after = jnp.logical_and(r_i // HEADS > c_i // HEADS, r_i % HEADS == c_i % HEADS).astype(BF16)
    k_all = jnp.concatenate([kp[...].astype(BF16) for kp in k_pages], axis=0)
    z = lax.dot_general(q_rows, k_all, (((1,), (1,)), ((), ())), preferred_element_type=F32) + bias
    drop, log_beta = _stick_terms(z)
    own_all = jnp.concatenate([own] * n_pages, axis=1)
    drop = jnp.where(own_all, drop, 0.0)
    drop_rows = jnp.concatenate([drop[:, j * pw:(j + 1) * pw] for j in range(n_pages)], axis=0)
    within = jnp.dot(drop_rows.astype(BF16), after, preferred_element_type=F32)
    totals = jnp.sum(drop_rows, axis=1, keepdims=True)
    run = jnp.zeros((8, 1), F32)
    laters = [None] * n_pages
    for j in reversed(range(n_pages)):
        laters[j] = within[8 * j:8 * j + 8, :] + run
        run = run + totals[8 * j:8 * j + 8, :]
    later = jnp.concatenate(laters, axis=1)
    a = jnp.where(own_all, jnp.exp2(log_beta - later), 0.0)
    v_all = jnp.concatenate([vp[...].astype(BF16) for vp in v_pages], axis=0)
    out = jnp.dot(a.astype(BF16), v_all, preferred_element_type=F32)
    for h in range(HEADS):
        ys_ref[0, :, h * HEAD_DIM:(h + 1) * HEAD_DIM] = out[h:h + 1, :].astype(BF16)


def _post_kernel(final, ym_ref, yc_ref, ys_ref, gt_ref, h_ref, p_ref, wb_ref, wo_ref, gmlp_ref, wup_ref,
                 wdn_ref, gple_ref, wpg_ref, wple_ref, gfin_ref, o_ref):
    mix = None
    for n, y_ref in enumerate((ym_ref, yc_ref, ys_ref)):
        pb = jnp.dot(y_ref[...], wb_ref[n], preferred_element_type=F32)
        gate = jax.nn.sigmoid(gt_ref[:, n * D_MODEL:(n + 1) * D_MODEL].astype(F32))
        mix = gate * pb if mix is None else mix + gate * pb
    h = h_ref[...] + jnp.dot(mix.astype(BF16), wo_ref[...], preferred_element_type=F32)

    xn = _rms(h, gmlp_ref[...]).astype(BF16)
    chunk = 1024
    for c in range(FFN_DIM // chunk):
        u = jnp.dot(xn, wup_ref[:, c * chunk:(c + 1) * chunk], preferred_element_type=F32)
        r = jnp.square(jnp.maximum(u, 0.0)).astype(BF16)
        h = h + jnp.dot(r, wdn_ref[c * chunk:(c + 1) * chunk, :], preferred_element_type=F32)

    gate = jax.nn.sigmoid(jnp.dot(_rms(h, gple_ref[...]).astype(BF16), wpg_ref[...],
                                  preferred_element_type=F32))
    h = h + jnp.dot(p_ref[...].astype(BF16), wple_ref[...], preferred_element_type=F32) * gate
    o_ref[...] = _rms(h, gfin_ref[...]) if final else h


def _post(final, layer, ym, yc, ys, gt, h, p, wb, wo, gmlp, wup, wdn, gple, wpg, wple, gfin, tm):
    m = h.shape[0]
    row = lambda width: pl.BlockSpec((tm, width), lambda i: (i, 0))
    p_spec = pl.BlockSpec((None, tm, PLE_DIM), lambda i: (layer, i, 0))
    return pl.pallas_call(
        functools.partial(_post_kernel, final),
        grid=(m // tm,),
        in_specs=[row(WIDTH), row(WIDTH), row(WIDTH), row(3 * D_MODEL), row(D_MODEL), p_spec,
                  _resident((3, WIDTH, D_MODEL)), _resident((D_MODEL, D_MODEL)), _resident((1, D_MODEL)),
                  _resident((D_MODEL, FFN_DIM)), _resident((FFN_DIM, D_MODEL)), _resident((1, D_MODEL)),
                  _resident((D_MODEL, D_MODEL)), _resident((PLE_DIM, D_MODEL)), _resident((1, D_MODEL))],
        out_specs=row(D_MODEL),
        out_shape=jax.ShapeDtypeStruct((m, D_MODEL), F32),
        compiler_params=pltpu.CompilerParams(dimension_semantics=("parallel",),
                                             vmem_limit_bytes=VMEM_LIMIT),
        name="post",
    )(ym, yc, ys, gt, h, p, wb, wo, gmlp, wup, wdn, gple, wpg, wple, gfin)


N_POST_IN = 15
N_DECODE_IN = 12


def _post_decode_kernel(final, layer, n_pages, first_layer, pt_ref, bsb_ref, *refs):
    (ym_ref, yc_ref, ys_ref, gt_ref, h_ref, p_ref, wb_ref, wo_ref, gmlp_ref, wup_ref, wdn_ref, gple_ref,
     wpg_ref, wple_ref, gfin_ref) = refs[:N_POST_IN]
    dec_in = refs[N_POST_IN:N_POST_IN + N_DECODE_IN]
    ck_hbm, cv_hbm = refs[N_POST_IN + N_DECODE_IN:N_POST_IN + N_DECODE_IN + 2]
    (o_ref, ym_s_ref, yc_s_ref, ys_s_ref, c_ref, n_ref, m_ref, cv_ref,
     acc_ref, xn_ref, kbuf, vbuf, sem) = refs[-13:]
    t = pl.program_id(0)
    slot = t % 2

    def page_copies(seq, buf):
        copies = []
        for j in range(n_pages):
            pid = pt_ref[seq, j]
            copies.append(pltpu.make_async_copy(ck_hbm.at[layer, pid], kbuf.at[buf, j], sem.at[buf]))
            copies.append(pltpu.make_async_copy(cv_hbm.at[layer, pid], vbuf.at[buf, j], sem.at[buf]))
        return copies

    @pl.when(t == 0)
    def _():
        for cp in page_copies(0, 0):
            cp.start()

    @pl.when(t + 1 < pl.num_programs(0))
    def _():
        for cp in page_copies(t + 1, 1 - slot):
            cp.start()

    for cp in page_copies(t, slot):
        cp.wait()
    k_pages = [kbuf.at[slot, j] for j in range(n_pages)]
    v_pages = [vbuf.at[slot, j] for j in range(n_pages)]
    c_slots = [c_ref.at[l] for l in range(c_ref.shape[0])] if first_layer else [c_ref]
    half = FFN_DIM // 2
    chunk = 1024

    def decode():
        _decode_seq(bsb_ref, *dec_in, k_pages, v_pages, ym_s_ref, yc_s_ref, ys_s_ref, c_slots, n_ref, m_ref,
                    cv_ref)

    def mlp_half(xn, h, base):
        for c in range(half // chunk):
            lo = base + c * chunk
            u = jnp.dot(xn, wup_ref[:, lo:lo + chunk], preferred_element_type=F32)
            r = jnp.square(jnp.maximum(u, 0.0)).astype(BF16)
            h = h + jnp.dot(r, wdn_ref[lo:lo + chunk, :], preferred_element_type=F32)
        return h

    parity = pl.program_id(0) % 2

    @pl.when(parity == 0)
    def _():
        decode()
        mix = None
        for n, y_ref in enumerate((ym_ref, yc_ref, ys_ref)):
            pb = jnp.dot(y_ref[...], wb_ref[n], preferred_element_type=F32)
            gate = jax.nn.sigmoid(gt_ref[:, n * D_MODEL:(n + 1) * D_MODEL].astype(F32))
            mix = gate * pb if mix is None else mix + gate * pb
        h = h_ref[...] + jnp.dot(mix.astype(BF16), wo_ref[...], preferred_element_type=F32)
        xn = _rms(h, gmlp_ref[...]).astype(BF16)
        xn_ref[...] = xn
        acc_ref[...] = mlp_half(xn, h, 0)

    @pl.when(parity == 1)
    def _():
        decode()
        h = mlp_half(xn_ref[...], acc_ref[...], half)
        gate = jax.nn.sigmoid(jnp.dot(_rms(h, gple_ref[...]).astype(BF16), wpg_ref[...],
                                      preferred_element_type=F32))
        h = h + jnp.dot(p_ref[...].astype(BF16), wple_ref[...], preferred_element_type=F32) * gate
        o_ref[...] = _rms(h, gfin_ref[...]) if final else h


def _post_decode(final, layer, depth, ym, yc, ys, gt, h, p, post_w, pm, pc, sq, gi, gf, state_c, state_n, m0,
                 state_conv, bi, bf, cw, bsb, page_table, cache_k, cache_v, c_prev):
    nseq, n_pages = page_table.shape
    pw = cache_k.shape[2]
    m = h.shape[0]
    tm = 2 * m // nseq
    assert tm * nseq == 2 * m and tm % 16 == 0
    first = c_prev is None
    prow = lambda width: pl.BlockSpec((tm, width), lambda t, pt: (t // 2, 0))
    fixed = lambda shape: pl.BlockSpec(shape, lambda t, pt: (0,) * len(shape), pipeline_mode=pl.Buffered(1))
    srow = lambda width: pl.BlockSpec((1, 1, width), lambda t, pt: (t, 0, 0))
    state4 = pl.BlockSpec((None, 1, HEADS, HEAD_DIM, HEAD_DIM), lambda t, pt: (layer, t, 0, 0, 0))
    state3 = pl.BlockSpec((None, 1, HEADS, HEAD_DIM), lambda t, pt: (layer, t, 0, 0))
    conv3 = pl.BlockSpec((None, 1, 2, WIDTH), lambda t, pt: (layer, t, 0, 0))

    post_specs = [prow(WIDTH), prow(WIDTH), prow(WIDTH), prow(3 * D_MODEL), prow(D_MODEL),
                  pl.BlockSpec((None, tm, PLE_DIM), lambda t, pt: (layer, t // 2, 0)),
                  fixed((3, WIDTH, D_MODEL)), fixed((D_MODEL, D_MODEL)), fixed((1, D_MODEL)),
                  fixed((D_MODEL, FFN_DIM)), fixed((FFN_DIM, D_MODEL)), fixed((1, D_MODEL)),
                  fixed((D_MODEL, D_MODEL)), fixed((PLE_DIM, D_MODEL)), fixed((1, D_MODEL))]
    dec_specs = [srow(2048), srow(1536), srow(WIDTH), srow(LANES), srow(LANES), state4, state3, srow(LANES),
                 conv3, fixed((1, LANES)), fixed((1, LANES)), fixed((3, WIDTH))]
    assert len(post_specs) == N_POST_IN and len(dec_specs) == N_DECODE_IN
    in_specs = ([pl.BlockSpec(memory_space=pltpu.SMEM)] + post_specs + dec_specs
                + [pl.BlockSpec(memory_space=pl.ANY)] * 2)
    args = ([page_table, bsb, ym, yc, ys, gt, h, p, *post_w]
            + [a.reshape(nseq, 1, a.shape[-1]) for a in (pm, pc, sq, gi, gf)]
            + [state_c, state_n, m0, state_conv, bi, bf, cw, cache_k, cache_v])
    if first:
        c_spec = pl.BlockSpec((depth, 1, HEADS, HEAD_DIM, HEAD_DIM), lambda t, pt: (0, t, 0, 0, 0))
        aliases = {}
    else:
        c_spec = pl.BlockSpec((None, 1, HEADS, HEAD_DIM, HEAD_DIM), lambda t, pt: (layer, t, 0, 0, 0))
        in_specs.append(pl.BlockSpec(memory_space=pl.ANY))
        args.append(c_prev)
        aliases = {len(args) - 1: 4}
    out_specs = [prow(D_MODEL), srow(WIDTH), srow(WIDTH), srow(WIDTH), c_spec,
                 pl.BlockSpec((1, HEADS, HEAD_DIM), lambda t, pt: (t, 0, 0)),
                 srow(LANES),
                 pl.BlockSpec((1, 2, WIDTH), lambda t, pt: (t, 0, 0))]
    out_shape = [jax.ShapeDtypeStruct((m, D_MODEL), F32)] + [jax.ShapeDtypeStruct((nseq, 1, WIDTH), BF16)] * 3 + [
        jax.ShapeDtypeStruct((depth, nseq, HEADS, HEAD_DIM, HEAD_DIM), F32),
        jax.ShapeDtypeStruct((nseq, HEADS, HEAD_DIM), F32),
        jax.ShapeDtypeStruct((nseq, 1, LANES), F32),
        jax.ShapeDtypeStruct((nseq, 2, WIDTH), F32)]
    return pl.pallas_call(
        functools.partial(_post_decode_kernel, final, layer, n_pages, first),
        grid_spec=pltpu.PrefetchScalarGridSpec(
            num_scalar_prefetch=1, grid=(nseq,), in_specs=in_specs, out_specs=out_specs,
            scratch_shapes=[pltpu.VMEM((tm, D_MODEL), F32), pltpu.VMEM((tm, D_MODEL), BF16),
                            pltpu.VMEM((2, n_pages, pw, HEAD_DIM), F32), pltpu.VMEM((2, n_pages, pw, HEAD_DIM), F32),
                            pltpu.SemaphoreType.DMA((2,))]),
        out_shape=out_shape,
        input_output_aliases=aliases,
        compiler_params=pltpu.CompilerParams(dimension_semantics=("arbitrary",),
                                             vmem_limit_bytes=VMEM_LIMIT),
        name="post_decode",
    )(*args)


def _pad_lanes(x):
    return jnp.pad(x, ((0, 0), (0, LANES - x.shape[-1])))


def _prep_w_in(w):
    gates0 = 4 * WIDTH
    zpad = jnp.zeros((w.shape[0], LANES - HEADS), w.dtype)
    wif = jnp.concatenate([w[:, gates0:gates0 + HEADS], zpad, w[:, gates0 + HEADS:gates0 + 2 * HEADS], zpad], axis=1)
    return w[:, :gates0].astype(BF16), w[:, gates0 + 2 * HEADS:].astype(BF16), wif.astype(BF16)


def kernel(x_prompt, x_sample, cache_k, cache_v, state_mlstm_C, state_mlstm_n, state_mlstm_m, state_conv,
           page_table, p_prompt, p_sample, g_mix, w_in, b_if, b_sb, conv_w, w_branch, w_out, g_mlp, w_up,
           w_down, g_ple, w_ple_gate, w_ple, g_final):
    depth = w_in.shape[0]
    batch, seq, _ = x_prompt.shape
    nseq = x_sample.shape[0]
    n_phys, page = cache_k.shape[1], cache_k.shape[2]
    cache_k = cache_k.reshape(depth, n_phys, page * HEADS, HEAD_DIM)
    cache_v = cache_v.reshape(depth, n_phys, page * HEADS, HEAD_DIM)
    hp = x_prompt.reshape(batch * seq, D_MODEL)
    hs = x_sample.reshape(nseq, D_MODEL)
    pp = p_prompt.reshape(depth, batch * seq, PLE_DIM)
    ps = p_sample.reshape(depth, nseq, PLE_DIM)
    gfin = g_final.reshape(1, D_MODEL)
    tm_p = 512
    tm_s = nseq

    outs_p, outs_s = [], []
    kv_p = kv_s = c_s = None
    for li in range(depth):
        final = li == depth - 1
        wh, wt, wif = _prep_w_in(w_in[li])
        gmix = g_mix[li].reshape(1, D_MODEL)
        bi = _pad_lanes(b_if[li, :HEADS].reshape(1, HEADS))
        bf = _pad_lanes(b_if[li, HEADS:].reshape(1, HEADS))
        post_w = (w_branch[li].astype(BF16), w_out[li].astype(BF16), g_mlp[li].reshape(1, D_MODEL),
                  w_up[li].astype(BF16), w_down[li].astype(BF16), g_ple[li].reshape(1, D_MODEL),
                  w_ple_gate[li].astype(BF16), w_ple[li].astype(BF16), gfin)

        pm, pc, sq, kf, vf, kvb, gt, gi, gf = _proj(hp, gmix, wh, wt, wif, tm_p, li, depth, kv_p)
        kv_p = (kf, vf)
        ym, yc, s_p, m_p, cv_p = _mlstm_conv(pm, pc, gi, gf, bi, bf, conv_w[li], batch, seq)
        ys = _sb_prompt(sq, kvb, b_sb[li], batch, seq)
        outs_p.append((s_p[..., :HEAD_DIM], s_p[..., HEAD_DIM], m_p[:, 0, :HEADS], cv_p))
        pm_s, pc_s, sq_s, kf, vf, _, gt_s, gi_s, gf_s = _proj(hs, gmix, wh, wt, wif, tm_s, li, depth, kv_s)
        kv_s = (kf, vf)
        m0 = _pad_lanes(state_mlstm_m[li]).reshape(nseq, 1, LANES)

        hp, ym_s, yc_s, ys_s, c_s, n_s, m_s, cv_s = _post_decode(
            final, li, depth, ym, yc, ys, gt, hp, pp, post_w, pm_s, pc_s, sq_s, gi_s, gf_s, state_mlstm_C,
            state_mlstm_n, m0, state_conv, bi, bf, conv_w[li], b_sb[li], page_table, cache_k, cache_v, c_s)
        hs = _post(final, li, ym_s.reshape(nseq, WIDTH), yc_s.reshape(nseq, WIDTH), ys_s.reshape(nseq, WIDTH),
                   gt_s, hs, ps, *post_w, tm_s)
        outs_s.append((n_s, m_s[:, 0, :HEADS], cv_s))

    stack = lambda outs, j: jnp.stack([o[j] for o in outs])
    k_p, v_p = (a.reshape(depth, batch, seq, HEADS, HEAD_DIM) for a in kv_p)
    k_s, v_s = (a.reshape(depth, nseq, 1, HEADS, HEAD_DIM) for a in kv_s)
    return ((hp.reshape(batch, seq, D_MODEL), hs.reshape(nseq, 1, D_MODEL), k_p, v_p)
            + tuple(stack(outs_p, j) for j in range(4)) + (k_s, v_s, c_s)
            + tuple(stack(outs_s, j) for j in range(3)))
```

```python
import functools

import jax
import jax.numpy as jnp
from jax import lax
from jax.experimental import pallas as pl
from jax.experimental.pallas import tpu as pltpu

F32 = jnp.float32
BF16 = jnp.bfloat16

D_MODEL = 1024
WIDTH = 512
HEADS = 4
HEAD_DIM = 128
FFN_DIM = 4 * D_MODEL
PLE_DIM = 256
RMS_EPS = 1e-6
LOG2E = 1.4426950408889634
LANES = 128

T_PC = 0
T_SQ = 1536
T_SK = 2048
T_SV = 2560
T_GT = 3072
N_HEAD_COLS = 2048
N_TAIL_COLS = 6144

MLSTM_CHUNK = 256
SB_BLOCK = 512
SB_K_BLOCK = 256
VMEM_LIMIT = 56 * 1024 * 1024


def _softplus(x):
    return jnp.maximum(x, 0.0) + jnp.log(1.0 + jnp.exp(-jnp.abs(x)))


def _stick_terms(z):
    drop = jnp.maximum(z, 0.0) + jnp.log2(1.0 + jnp.exp2(-jnp.abs(z)))
    return drop, z - drop


def _rms(x, g):
    return x * lax.rsqrt(jnp.mean(x * x, axis=-1, keepdims=True) + RMS_EPS) * g


def _resident(shape, layer=None):
    nd = len(shape)
    if layer is None:
        return pl.BlockSpec(shape, lambda *_: (0,) * nd, pipeline_mode=pl.Buffered(1))
    return pl.BlockSpec((None,) + tuple(shape), lambda *_: (layer,) + (0,) * nd, pipeline_mode=pl.Buffered(1))


def _proj_kernel(first_layer, x_ref, g_ref, wh_ref, wt_ref, wif_ref, *refs):
    pm_ref, pc_ref, sq_ref, kf_ref, vf_ref, kvb_ref, gt_ref, gi_ref, gf_ref = refs[-9:]
    xn = _rms(x_ref[...], g_ref[...]).astype(BF16)

    def mm(w_ref, c0, width):
        return jnp.dot(xn, w_ref[:, c0:c0 + width], preferred_element_type=F32)

    scale = HEAD_DIM ** -0.5
    pm_ref[:, 0:512] = mm(wh_ref, 0, 512).astype(BF16)
    pm_ref[:, 512:1024] = (mm(wh_ref, 512, 512) * scale).astype(BF16)
    pm_ref[:, 1024:1536] = mm(wh_ref, 1024, 512).astype(BF16)
    pm_ref[:, 1536:2048] = mm(wh_ref, 1536, 512).astype(BF16)
    for j in range(3):
        pc_ref[:, j * 512:(j + 1) * 512] = mm(wt_ref, T_PC + j * 512, 512).astype(BF16)
    sq_ref[...] = (mm(wt_ref, T_SQ, 512) * (scale * LOG2E)).astype(BF16)
    k = mm(wt_ref, T_SK, 512)
    v = mm(wt_ref, T_SV, 512)
    tm = k.shape[0]
    slots = [kf_ref.at[l] for l in range(kf_ref.shape[0])] if first_layer else [kf_ref]
    vslots = [vf_ref.at[l] for l in range(vf_ref.shape[0])] if first_layer else [vf_ref]
    for h in range(HEADS):
        for kslot, vslot in zip(slots, vslots):
            kslot[pl.ds(h, tm, stride=HEADS), :] = k[:, h * HEAD_DIM:(h + 1) * HEAD_DIM]
            vslot[pl.ds(h, tm, stride=HEADS), :] = v[:, h * HEAD_DIM:(h + 1) * HEAD_DIM]
    kvb_ref[:, 0:512] = k.astype(BF16)
    kvb_ref[:, 512:1024] = v.astype(BF16)
    for j in range(6):
        gt_ref[:, j * 512:(j + 1) * 512] = mm(wt_ref, T_GT + j * 512, 512).astype(BF16)
    gi_ref[...] = mm(wif_ref, 0, LANES)
    gf_ref[...] = mm(wif_ref, LANES, LANES)


def _proj(x2d, g, wh, wt, wif, tm, layer, depth, kv_prev):
    m = x2d.shape[0]
    first = kv_prev is None
    row = lambda width: pl.BlockSpec((tm, width), lambda i: (i, 0))
    if first:
        kv_spec = pl.BlockSpec((depth, tm * HEADS, HEAD_DIM), lambda i: (0, i, 0))
    else:
        kv_spec = pl.BlockSpec((None, tm * HEADS, HEAD_DIM), lambda i: (layer, i, 0))
    widths = (2048, 1536, 512, None, None, 1024, 3072, LANES, LANES)
    dtypes = (BF16, BF16, BF16, F32, F32, BF16, BF16, F32, F32)
    out_specs = [row(w) if w else kv_spec for w in widths]
    out_shape = [jax.ShapeDtypeStruct((m, w) if w else (depth, m * HEADS, HEAD_DIM), d)
                 for w, d in zip(widths, dtypes)]
    in_specs = [row(D_MODEL), _resident((1, D_MODEL), layer), _resident((D_MODEL, N_HEAD_COLS), layer),
                _resident((D_MODEL, N_TAIL_COLS), layer), _resident((D_MODEL, 2 * LANES), layer)]
    args = [x2d, g, wh, wt, wif]
    aliases = {}
    if not first:
        in_specs += [pl.BlockSpec(memory_space=pl.ANY)] * 2
        args += list(kv_prev)
        aliases = {5: 3, 6: 4}
    return pl.pallas_call(
        functools.partial(_proj_kernel, first),
        grid=(m // tm,),
        in_specs=in_specs,
        out_specs=out_specs,
        out_shape=out_shape,
        input_output_aliases=aliases,
        compiler_params=pltpu.CompilerParams(dimension_semantics=("parallel",),
                                             vmem_limit_bytes=VMEM_LIMIT),
        name="proj",
    )(*args)


def _scan_rows(x, op, fill):
    n = x.shape[0]
    row = lax.broadcasted_iota(jnp.int32, x.shape, 0)
    k = 1
    while k < n:
        shifted = pltpu.roll(x, k, axis=0)
        x = op(x, jnp.where(row >= k, shifted, fill(x)))
        k *= 2
    return x


def _mlstm_conv_kernel(pm_ref, pc_ref, gi_ref, gf_ref, bi_ref, bf_ref, cw_ref,
                       ym_ref, yc_ref, s_ref, m_ref, cs_ref):
    c = pl.program_id(1)
    L = pm_ref.shape[0]

    @pl.when(c == 0)
    def _():
        s_ref[...] = jnp.zeros_like(s_ref)
        m_ref[...] = jnp.zeros_like(m_ref)
        cs_ref[...] = jnp.zeros_like(cs_ref)

    cb = pc_ref[:, 0:512].astype(F32)
    xc = pc_ref[:, 512:1024].astype(F32) * pc_ref[:, 1024:1536].astype(F32)
    prev = cs_ref[...]
    row = lax.broadcasted_iota(jnp.int32, xc.shape, 0)
    xc1 = jnp.where(row == 0, prev[1:2, :], pltpu.roll(xc, 1, axis=0))
    xc2 = jnp.where(row == 0, prev[0:1, :], jnp.where(row == 1, prev[1:2, :], pltpu.roll(xc, 2, axis=0)))
    cw = cw_ref[...]
    yc_ref[...] = (cb * (cw[0:1, :] * xc2 + cw[1:2, :] * xc1 + cw[2:3, :] * xc)).astype(BF16)
    cs_ref[...] = xc[L - 2:L, :]

    m_prev = m_ref[...]
    ig = gi_ref[...] + bi_ref[...]
    logf = -_softplus(-(gf_ref[...] + bf_ref[...]))
    b = _scan_rows(logf, jnp.add, jnp.zeros_like)
    a = ig - b
    mx = jnp.maximum(m_prev, _scan_rows(a, jnp.maximum, lambda x: x))
    m_t = b + mx
    a_t = a.T
    s_in = jnp.exp(m_prev - mx)
    floor = jnp.exp(-m_t)
    mx_last = mx[L - 1:L, :]
    wk = jnp.exp(a - mx_last)
    decay = jnp.exp(m_prev - mx_last)
    m_ref[...] = m_t[L - 1:L, :]

    r_i = lax.broadcasted_iota(jnp.int32, (L, L), 0)
    c_i = lax.broadcasted_iota(jnp.int32, (L, L), 1)
    causal = c_i <= r_i
    one_col = (lax.broadcasted_iota(jnp.int32, (L, HEAD_DIM), 1) == 0).astype(BF16)
    for h in range(HEADS):
        sl = slice(h * HEAD_DIM, (h + 1) * HEAD_DIM)
        q = pm_ref[:, sl]
        k = pm_ref[:, 512 + h * HEAD_DIM:512 + (h + 1) * HEAD_DIM]
        v = pm_ref[:, 1024 + h * HEAD_DIM:1024 + (h + 1) * HEAD_DIM]
        o = pm_ref[:, 1536 + h * HEAD_DIM:1536 + (h + 1) * HEAD_DIM]
        vext = jnp.concatenate([v, one_col], axis=1)
        qk = lax.dot_general(q, k, (((1,), (1,)), ((), ())), preferred_element_type=F32)
        e = jnp.exp(a_t[h:h + 1, :] - mx[:, h:h + 1])
        w = jnp.where(causal, e, 0.0) * qk
        intra = jnp.dot(w.astype(BF16), vext, preferred_element_type=F32)
        state = s_ref[h]
        inter = jnp.dot(q, state.astype(BF16), preferred_element_type=F32)
        sc = s_in[:, h:h + 1]
        num = sc * inter[:, :HEAD_DIM] + intra[:, :HEAD_DIM]
        den = sc * inter[:, HEAD_DIM:HEAD_DIM + 1] + intra[:, HEAD_DIM:HEAD_DIM + 1]
        inv = 1.0 / jnp.maximum(jnp.abs(den), floor[:, h:h + 1])
        ym_ref[:, sl] = (jax.nn.sigmoid(o.astype(F32)) * (num * inv)).astype(BF16)
        kw_t = (k.astype(F32) * wk[:, h:h + 1]).T.astype(BF16)
        s_ref[h] = decay[:, h:h + 1] * state + jnp.dot(kw_t, vext, preferred_element_type=F32)


def _mlstm_conv(layer, pm, pc, gi, gf, bi, bf, cw, batch, seq):
    L = MLSTM_CHUNK
    nc = seq // L
    row = lambda width: pl.BlockSpec((L, width), lambda b, c: (b * nc + c, 0))
    m = batch * seq
    return pl.pallas_call(
        _mlstm_conv_kernel,
        grid=(batch, nc),
        in_specs=[row(2048), row(1536), row(LANES), row(LANES),
                  _resident((1, LANES), layer), _resident((1, LANES), layer), _resident((3, WIDTH), layer)],
        out_specs=[row(WIDTH), row(WIDTH),
                   pl.BlockSpec((None, HEADS, HEAD_DIM, 2 * HEAD_DIM), lambda b, c: (b, 0, 0, 0)),
                   pl.BlockSpec((None, 1, LANES), lambda b, c: (b, 0, 0)),
                   pl.BlockSpec((None, 2, WIDTH), lambda b, c: (b, 0, 0))],
        out_shape=[jax.ShapeDtypeStruct((m, WIDTH), BF16), jax.ShapeDtypeStruct((m, WIDTH), BF16),
                   jax.ShapeDtypeStruct((batch, HEADS, HEAD_DIM, 2 * HEAD_DIM), F32),
                   jax.ShapeDtypeStruct((batch, 1, LANES), F32),
                   jax.ShapeDtypeStruct((batch, 2, WIDTH), F32)],
        compiler_params=pltpu.CompilerParams(dimension_semantics=("parallel", "arbitrary"),
                                             vmem_limit_bytes=VMEM_LIMIT),
        name="mlstm_conv",
    )(pm, pc, gi, gf, bi, bf, cw)


def _sb_kernel(layer, bsb_ref, q_ref, kv_ref, o_ref, acc_ref, run_ref):
    i = pl.program_id(1)
    tq = q_ref.shape[0]
    tk = SB_K_BLOCK
    r = tq // tk
    r_i = lax.broadcasted_iota(jnp.int32, (tk, tk), 0)
    c_i = lax.broadcasted_iota(jnp.int32, (tk, tk), 1)
    after = (r_i > c_i).astype(BF16)
    visible = c_i < r_i
    acc_ref[...] = jnp.zeros_like(acc_ref)
    run_ref[...] = jnp.zeros_like(run_ref)

    def tile(kb, rows, diagonal):
        k0 = pl.multiple_of(kb * tk, tk)
        for h in range(HEADS):
            sl = slice(h * HEAD_DIM, (h + 1) * HEAD_DIM)
            k = kv_ref[pl.ds(k0, tk), sl]
            v = kv_ref[pl.ds(k0, tk), WIDTH + h * HEAD_DIM:WIDTH + (h + 1) * HEAD_DIM]
            z = (lax.dot_general(q_ref[rows, sl], k, (((1,), (1,)), ((), ())), preferred_element_type=F32)
                 + bsb_ref[layer, h] * LOG2E)
            drop, log_beta = _stick_terms(z)
            if diagonal:
                drop = jnp.where(visible, drop, 0.0)
            run = run_ref[h, rows]
            later = run + jnp.dot(drop.astype(BF16), after, preferred_element_type=F32)
            a = jnp.exp2(log_beta - later)
            if diagonal:
                a = jnp.where(visible, a, 0.0)
            acc_ref[rows, sl] += jnp.dot(a.astype(BF16), v, preferred_element_type=F32)
            run_ref[h, rows] = run + jnp.sum(drop, axis=1, keepdims=True)

    for d in reversed(range(r)):
        tile(i * r + d, slice(d * tk, (d + 1) * tk), True)
        if d + 1 < r:
            tile(i * r + d, slice((d + 1) * tk, tq), False)

    def body(j, carry):
        for u in range(r):
            tile(i * r - 1 - r * j - u, slice(0, tq), False)
        return carry

    lax.fori_loop(0, i, body, 0)
    o_ref[...] = acc_ref[...].astype(BF16)


def _sb_prompt(layer, sq, kvb, bsb, batch, seq):
    blk = SB_BLOCK
    nq = seq // blk
    return pl.pallas_call(
        functools.partial(_sb_kernel, layer),
        grid=(batch, nq),
        in_specs=[pl.BlockSpec(memory_space=pltpu.SMEM),
                  pl.BlockSpec((blk, WIDTH), lambda b, i: (b * nq + i, 0)),
                  pl.BlockSpec((seq, 2 * WIDTH), lambda b, i: (b, 0))],
        out_specs=pl.BlockSpec((blk, WIDTH), lambda b, i: (b * nq + i, 0)),
        out_shape=jax.ShapeDtypeStruct((batch * seq, WIDTH), BF16),
        scratch_shapes=[pltpu.VMEM((blk, WIDTH), F32), pltpu.VMEM((HEADS, blk, 1), F32)],
        compiler_params=pltpu.CompilerParams(dimension_semantics=("parallel", "parallel"),
                                             vmem_limit_bytes=VMEM_LIMIT),
        name="sb_prompt",
    )(bsb, sq, kvb)


def _decode_seq(layer, bsb_ref, pm_ref, pc_ref, sq_ref, gi_ref, gf_ref, c0_ref, n0_ref, m0_ref, cv0_ref, bi_ref,
                bf_ref, cw_ref, k_pages, v_pages, ym_ref, yc_ref, ys_ref, c_slots, n_ref, m_ref, cv_ref):
    n_pages = len(k_pages)
    cw = cw_ref[...]
    pw = k_pages[0].shape[0]

    cb = pc_ref[0, :, 0:512].astype(F32)
    xc = pc_ref[0, :, 512:1024].astype(F32) * pc_ref[0, :, 1024:1536].astype(F32)
    prev = cv0_ref[0]
    yc_ref[0] = (cb * (cw[0:1, :] * prev[0:1, :] + cw[1:2, :] * prev[1:2, :] + cw[2:3, :] * xc)).astype(BF16)
    cv_ref[0, 0:1, :] = prev[1:2, :]
    cv_ref[0, 1:2, :] = xc

    ig = gi_ref[0] + bi_ref[...]
    logf = -_softplus(-(gf_ref[0] + bf_ref[...]))
    m0 = m0_ref[0]
    m_t = jnp.maximum(logf + m0, ig)
    w_in = jnp.exp(ig - m_t)
    s_in = jnp.exp(logf + m0 - m_t)
    floor = jnp.exp(-m_t)
    m_ref[0] = m_t
    row128 = lax.broadcasted_iota(jnp.int32, (HEAD_DIM, HEAD_DIM), 0)
    for h in range(HEADS):
        sl = slice(h * HEAD_DIM, (h + 1) * HEAD_DIM)
        q = pm_ref[0, :, sl].astype(F32)
        k = pm_ref[0, :, 512 + h * HEAD_DIM:512 + (h + 1) * HEAD_DIM].astype(F32)
        v = pm_ref[0, :, 1024 + h * HEAD_DIM:1024 + (h + 1) * HEAD_DIM].astype(F32)
        o = pm_ref[0, :, 1536 + h * HEAD_DIM:1536 + (h + 1) * HEAD_DIM].astype(F32)
        qk_rows = jnp.where(row128 == 0, q, jnp.where(row128 == 1, k, 0.0))
        cols = qk_rows.T
        q_col = cols[:, 0:1]
        k_col = cols[:, 1:2]
        c0 = c0_ref[0, h]
        n0 = n0_ref[0, h:h + 1, :]
        sc = s_in[:, h:h + 1]
        wi = w_in[:, h:h + 1]
        w = wi * jnp.sum(q * k, axis=1, keepdims=True)
        num = sc * jnp.sum(q_col * c0, axis=0, keepdims=True) + w * v
        den = sc * jnp.sum(q * n0, axis=1, keepdims=True) + w
        hc = num * (1.0 / jnp.maximum(jnp.abs(den), floor[:, h:h + 1]))
        ym_ref[0, :, sl] = (jax.nn.sigmoid(o) * hc).astype(BF16)
        c_new = sc * c0 + wi * (k_col * v)
        for slot in c_slots:
            slot[0, h] = c_new
        n_ref[0, h:h + 1, :] = sc * n0 + wi * k

    r8 = lax.broadcasted_iota(jnp.int32, (8, HEAD_DIM), 0)
    rcol = lax.broadcasted_iota(jnp.int32, (8, 1), 0)
    sq = sq_ref[0].astype(F32)
    q_rows = jnp.zeros((8, HEAD_DIM), F32)
    bias = jnp.zeros((8, 1), F32)
    for h in range(HEADS):
        q_rows = jnp.where(r8 == h, sq[:, h * HEAD_DIM:(h + 1) * HEAD_DIM], q_rows)
        bias = jnp.where(rcol == h, bsb_ref[layer, h] * LOG2E, bias)
    q_rows = q_rows.astype(BF16)
    own = (lax.broadcasted_iota(jnp.int32, (8, pw), 1) % HEADS) == lax.broadcasted_iota(jnp.int32, (8, pw), 0)
    r_i = lax.broadcasted_iota(jnp.int32, (pw, pw), 0)
    c_i = lax.broadcasted_iota(jnp.int32, (pw, pw), 1)
    after = jnp.logical_and(r_i // HEADS > c_i // HEADS, r_i % HEADS == c_i % HEADS).astype(BF16)
    k_all = jnp.concatenate([kp[...].astype(BF16) for kp in k_pages], axis=0)
    z = lax.dot_general(q_rows, k_all, (((1,), (1,)), ((), ())), preferred_element_type=F32) + bias
    drop, log_beta = _stick_terms(z)
    own_all = jnp.concatenate([own] * n_pages, axis=1)
    drop = jnp.where(own_all, drop, 0.0)
    drop_rows = jnp.concatenate([drop[:, j * pw:(j + 1) * pw] for j in range(n_pages)], axis=0)
    within = jnp.dot(drop_rows.astype(BF16), after, preferred_element_type=F32)
    totals = jnp.sum(drop_rows, axis=1, keepdims=True)
    run = jnp.zeros((8, 1), F32)
    laters = [None] * n_pages
    for j in reversed(range(n_pages)):
        laters[j] = within[8 * j:8 * j + 8, :] + run
        run = run + totals[8 * j:8 * j + 8, :]
    later = jnp.concatenate(laters, axis=1)
    a = jnp.where(own_all, jnp.exp2(log_beta - later), 0.0)
    v_all = jnp.concatenate([vp[...].astype(BF16) for vp in v_pages], axis=0)
    out = jnp.dot(a.astype(BF16), v_all, preferred_element_type=F32)
    for h in range(HEADS):
        ys_ref[0, :, h * HEAD_DIM:(h + 1) * HEAD_DIM] = out[h:h + 1, :].astype(BF16)


def _post_kernel(final, ym_ref, yc_ref, ys_ref, gt_ref, h_ref, p_ref, wb_ref, wo_ref, gmlp_ref, wup_ref,
                 wdn_ref, gple_ref, wpg_ref, wple_ref, gfin_ref, o_ref):
    mix = None
    for n, y_ref in enumerate((ym_ref, yc_ref, ys_ref)):
        pb = jnp.dot(y_ref[...], wb_ref[n], preferred_element_type=F32)
        gate = jax.nn.sigmoid(gt_ref[:, n * D_MODEL:(n + 1) * D_MODEL].astype(F32))
        mix = gate * pb if mix is None else mix + gate * pb
    h = h_ref[...] + jnp.dot(mix.astype(BF16), wo_ref[...], preferred_element_type=F32)

    xn = _rms(h, gmlp_ref[...]).astype(BF16)
    chunk = 1024
    for c in range(FFN_DIM // chunk):
        u = jnp.dot(xn, wup_ref[:, c * chunk:(c + 1) * chunk], preferred_element_type=F32)
        r = jnp.square(jnp.maximum(u, 0.0)).astype(BF16)
        h = h + jnp.dot(r, wdn_ref[c * chunk:(c + 1) * chunk, :], preferred_element_type=F32)

    gate = jax.nn.sigmoid(jnp.dot(_rms(h, gple_ref[...]).astype(BF16), wpg_ref[...],
                                  preferred_element_type=F32))
    h = h + jnp.dot(p_ref[...].astype(BF16), wple_ref[...], preferred_element_type=F32) * gate
    o_ref[...] = _rms(h, gfin_ref[...]) if final else h


def _post(final, layer, ym, yc, ys, gt, h, p, wb, wo, gmlp, wup, wdn, gple, wpg, wple, gfin, tm):
    m = h.shape[0]
    row = lambda width: pl.BlockSpec((tm, width), lambda i: (i, 0))
    p_spec = pl.BlockSpec((None, tm, PLE_DIM), lambda i: (layer, i, 0))
    return pl.pallas_call(
        functools.partial(_post_kernel, final),
        grid=(m // tm,),
        in_specs=[row(WIDTH), row(WIDTH), row(WIDTH), row(3 * D_MODEL), row(D_MODEL), p_spec,
                  _resident((3, WIDTH, D_MODEL), layer), _resident((D_MODEL, D_MODEL), layer),
                  _resident((1, D_MODEL), layer), _resident((D_MODEL, FFN_DIM), layer),
                  _resident((FFN_DIM, D_MODEL), layer), _resident((1, D_MODEL), layer),
                  _resident((D_MODEL, D_MODEL), layer), _resident((PLE_DIM, D_MODEL), layer),
                  _resident((1, D_MODEL))],
        out_specs=row(D_MODEL),
        out_shape=jax.ShapeDtypeStruct((m, D_MODEL), F32),
        compiler_params=pltpu.CompilerParams(dimension_semantics=("parallel",),
                                             vmem_limit_bytes=VMEM_LIMIT),
        name="post",
    )(ym, yc, ys, gt, h, p, wb, wo, gmlp, wup, wdn, gple, wpg, wple, gfin)


N_POST_IN = 15
N_DECODE_IN = 12


def _post_decode_kernel(final, layer, n_pages, first_layer, pt_ref, bsb_ref, *refs):
    (ym_ref, yc_ref, ys_ref, gt_ref, h_ref, p_ref, wb_ref, wo_ref, gmlp_ref, wup_ref, wdn_ref, gple_ref,
     wpg_ref, wple_ref, gfin_ref) = refs[:N_POST_IN]
    dec_in = refs[N_POST_IN:N_POST_IN + N_DECODE_IN]
    ck_hbm, cv_hbm = refs[N_POST_IN + N_DECODE_IN:N_POST_IN + N_DECODE_IN + 2]
    (o_ref, ym_s_ref, yc_s_ref, ys_s_ref, c_ref, n_ref, m_ref, cv_ref,
     acc_ref, xn_ref, kbuf, vbuf, sem) = refs[-13:]
    t = pl.program_id(0)
    slot = t % 2

    def page_copies(seq, buf):
        copies = []
        for j in range(n_pages):
            pid = pt_ref[seq, j]
            copies.append(pltpu.make_async_copy(ck_hbm.at[layer, pid], kbuf.at[buf, j], sem.at[buf]))
            copies.append(pltpu.make_async_copy(cv_hbm.at[layer, pid], vbuf.at[buf, j], sem.at[buf]))
        return copies

    @pl.when(t == 0)
    def _():
        for cp in page_copies(0, 0):
            cp.start()

    @pl.when(t + 1 < pl.num_programs(0))
    def _():
        for cp in page_copies(t + 1, 1 - slot):
            cp.start()

    for cp in page_copies(t, slot):
        cp.wait()
    k_pages = [kbuf.at[slot, j] for j in range(n_pages)]
    v_pages = [vbuf.at[slot, j] for j in range(n_pages)]
    c_slots = [c_ref.at[l] for l in range(c_ref.shape[0])] if first_layer else [c_ref]
    half = FFN_DIM // 2
    chunk = 1024

    def decode():
        _decode_seq(layer, bsb_ref, *dec_in, k_pages, v_pages, ym_s_ref, yc_s_ref, ys_s_ref, c_slots, n_ref, m_ref,
                    cv_ref)

    def mlp_half(xn, h, base):
        for c in range(half // chunk):
            lo = base + c * chunk
            u = jnp.dot(xn, wup_ref[:, lo:lo + chunk], preferred_element_type=F32)
            r = jnp.square(jnp.maximum(u, 0.0)).astype(BF16)
            h = h + jnp.dot(r, wdn_ref[lo:lo + chunk, :], preferred_element_type=F32)
        return h

    parity = pl.program_id(0) % 2

    @pl.when(parity == 0)
    def _():
        decode()
        mix = None
        for n, y_ref in enumerate((ym_ref, yc_ref, ys_ref)):
            pb = jnp.dot(y_ref[...], wb_ref[n], preferred_element_type=F32)
            gate = jax.nn.sigmoid(gt_ref[:, n * D_MODEL:(n + 1) * D_MODEL].astype(F32))
            mix = gate * pb if mix is None else mix + gate * pb
        h = h_ref[...] + jnp.dot(mix.astype(BF16), wo_ref[...], preferred_element_type=F32)
        xn = _rms(h, gmlp_ref[...]).astype(BF16)
        xn_ref[...] = xn
        acc_ref[...] = mlp_half(xn, h, 0)

    @pl.when(parity == 1)
    def _():
        decode()
        h = mlp_half(xn_ref[...], acc_ref[...], half)
        gate = jax.nn.sigmoid(jnp.dot(_rms(h, gple_ref[...]).astype(BF16), wpg_ref[...],
                                      preferred_element_type=F32))
        h = h + jnp.dot(p_ref[...].astype(BF16), wple_ref[...], preferred_element_type=F32) * gate
        o_ref[...] = _rms(h, gfin_ref[...]) if final else h


def _post_decode(final, layer, depth, ym, yc, ys, gt, h, p, post_w, pm, pc, sq, gi, gf, state_c, state_n, m0,
                 state_conv, bi, bf, cw, bsb, page_table, cache_k, cache_v, c_prev):
    nseq, n_pages = page_table.shape
    pw = cache_k.shape[2]
    m = h.shape[0]
    tm = 2 * m // nseq
    assert tm * nseq == 2 * m and tm % 16 == 0
    first = c_prev is None
    prow = lambda width: pl.BlockSpec((tm, width), lambda t, pt: (t // 2, 0))
    fixed = lambda shape: _resident(shape, layer)
    srow = lambda width: pl.BlockSpec((1, 1, width), lambda t, pt: (t, 0, 0))
    state4 = pl.BlockSpec((None, 1, HEADS, HEAD_DIM, HEAD_DIM), lambda t, pt: (layer, t, 0, 0, 0))
    state3 = pl.BlockSpec((None, 1, HEADS, HEAD_DIM), lambda t, pt: (layer, t, 0, 0))
    conv3 = pl.BlockSpec((None, 1, 2, WIDTH), lambda t, pt: (layer, t, 0, 0))

    post_specs = [prow(WIDTH), prow(WIDTH), prow(WIDTH), prow(3 * D_MODEL), prow(D_MODEL),
                  pl.BlockSpec((None, tm, PLE_DIM), lambda t, pt: (layer, t // 2, 0)),
                  fixed((3, WIDTH, D_MODEL)), fixed((D_MODEL, D_MODEL)), fixed((1, D_MODEL)),
                  fixed((D_MODEL, FFN_DIM)), fixed((FFN_DIM, D_MODEL)), fixed((1, D_MODEL)),
                  fixed((D_MODEL, D_MODEL)), fixed((PLE_DIM, D_MODEL)), _resident((1, D_MODEL))]
    dec_specs = [srow(2048), srow(1536), srow(WIDTH), srow(LANES), srow(LANES), state4, state3, srow(LANES),
                 conv3, fixed((1, LANES)), fixed((1, LANES)), fixed((3, WIDTH))]
    assert len(post_specs) == N_POST_IN and len(dec_specs) == N_DECODE_IN
    in_specs = ([pl.BlockSpec(memory_space=pltpu.SMEM)] + post_specs + dec_specs
                + [pl.BlockSpec(memory_space=pl.ANY)] * 2)
    args = ([page_table, bsb, ym, yc, ys, gt, h, p, *post_w]
            + [a.reshape(nseq, 1, a.shape[-1]) for a in (pm, pc, sq, gi, gf)]
            + [state_c, state_n, m0, state_conv, bi, bf, cw, cache_k, cache_v])
    if first:
        c_spec = pl.BlockSpec((depth, 1, HEADS, HEAD_DIM, HEAD_DIM), lambda t, pt: (0, t, 0, 0, 0))
        aliases = {}
    else:
        c_spec = pl.BlockSpec((None, 1, HEADS, HEAD_DIM, HEAD_DIM), lambda t, pt: (layer, t, 0, 0, 0))
        in_specs.append(pl.BlockSpec(memory_space=pl.ANY))
        args.append(c_prev)
        aliases = {len(args) - 1: 4}
    out_specs = [prow(D_MODEL), srow(WIDTH), srow(WIDTH), srow(WIDTH), c_spec,
                 pl.BlockSpec((1, HEADS, HEAD_DIM), lambda t, pt: (t, 0, 0)),
                 srow(LANES),
                 pl.BlockSpec((1, 2, WIDTH), lambda t, pt: (t, 0, 0))]
    out_shape = [jax.ShapeDtypeStruct((m, D_MODEL), F32)] + [jax.ShapeDtypeStruct((nseq, 1, WIDTH), BF16)] * 3 + [
        jax.ShapeDtypeStruct((depth, nseq, HEADS, HEAD_DIM, HEAD_DIM), F32),
        jax.ShapeDtypeStruct((nseq, HEADS, HEAD_DIM), F32),
        jax.ShapeDtypeStruct((nseq, 1, LANES), F32),
        jax.ShapeDtypeStruct((nseq, 2, WIDTH), F32)]
    return pl.pallas_call(
        functools.partial(_post_decode_kernel, final, layer, n_pages, first),
        grid_spec=pltpu.PrefetchScalarGridSpec(
            num_scalar_prefetch=1, grid=(nseq,), in_specs=in_specs, out_specs=out_specs,
            scratch_shapes=[pltpu.VMEM((tm, D_MODEL), F32), pltpu.VMEM((tm, D_MODEL), BF16),
                            pltpu.VMEM((2, n_pages, pw, HEAD_DIM), F32), pltpu.VMEM((2, n_pages, pw, HEAD_DIM), F32),
                            pltpu.SemaphoreType.DMA((2,))]),
        out_shape=out_shape,
        input_output_aliases=aliases,
        compiler_params=pltpu.CompilerParams(dimension_semantics=("arbitrary",),
                                             vmem_limit_bytes=VMEM_LIMIT),
        name="post_decode",
    )(*args)


def kernel(x_prompt, x_sample, cache_k, cache_v, state_mlstm_C, state_mlstm_n, state_mlstm_m, state_conv,
           page_table, p_prompt, p_sample, g_mix, w_in, b_if, b_sb, conv_w, w_branch, w_out, g_mlp, w_up,
           w_down, g_ple, w_ple_gate, w_ple, g_final):
    depth = w_in.shape[0]
    batch, seq, _ = x_prompt.shape
    nseq = x_sample.shape[0]
    n_phys, page = cache_k.shape[1], cache_k.shape[2]
    cache_k = cache_k.reshape(depth, n_phys, page * HEADS, HEAD_DIM)
    cache_v = cache_v.reshape(depth, n_phys, page * HEADS, HEAD_DIM)
    hp = x_prompt.reshape(batch * seq, D_MODEL)
    hs = x_sample.reshape(nseq, D_MODEL)
    pp = p_prompt.reshape(depth, batch * seq, PLE_DIM)
    ps = p_sample.reshape(depth, nseq, PLE_DIM)
    gfin = g_final.reshape(1, D_MODEL)
    tm_p = 512
    tm_s = nseq

    vec = lambda g: g.reshape(depth, 1, D_MODEL)
    gates0 = 4 * WIDTH
    wh = w_in[:, :, :gates0].astype(BF16)
    wt = w_in[:, :, gates0 + 2 * HEADS:].astype(BF16)
    zpad = jnp.zeros((depth, D_MODEL, LANES - HEADS), w_in.dtype)
    wif = jnp.concatenate([w_in[:, :, gates0:gates0 + HEADS], zpad,
                           w_in[:, :, gates0 + HEADS:gates0 + 2 * HEADS], zpad], axis=2).astype(BF16)
    gmix = vec(g_mix)
    bi = jnp.pad(b_if[:, :HEADS], ((0, 0), (0, LANES - HEADS))).reshape(depth, 1, LANES)
    bf = jnp.pad(b_if[:, HEADS:], ((0, 0), (0, LANES - HEADS))).reshape(depth, 1, LANES)
    post_w = (w_branch.astype(BF16), w_out.astype(BF16), vec(g_mlp), w_up.astype(BF16), w_down.astype(BF16),
              vec(g_ple), w_ple_gate.astype(BF16), w_ple.astype(BF16), gfin)
    m0_all = jnp.pad(state_mlstm_m, ((0, 0), (0, 0), (0, LANES - HEADS)))

    outs_p, outs_s = [], []
    kv_p = kv_s = c_s = None
    for li in range(depth):
        final = li == depth - 1

        pm, pc, sq, kf, vf, kvb, gt, gi, gf = _proj(hp, gmix, wh, wt, wif, tm_p, li, depth, kv_p)
        kv_p = (kf, vf)
        ym, yc, s_p, m_p, cv_p = _mlstm_conv(li, pm, pc, gi, gf, bi, bf, conv_w, batch, seq)
        ys = _sb_prompt(li, sq, kvb, b_sb, batch, seq)
        outs_p.append((s_p[..., :HEAD_DIM], s_p[..., HEAD_DIM], m_p[:, 0, :HEADS], cv_p))
        pm_s, pc_s, sq_s, kf, vf, _, gt_s, gi_s, gf_s = _proj(hs, gmix, wh, wt, wif, tm_s, li, depth, kv_s)
        kv_s = (kf, vf)
        m0 = m0_all[li].reshape(nseq, 1, LANES)

        hp, ym_s, yc_s, ys_s, c_s, n_s, m_s, cv_s = _post_decode(
            final, li, depth, ym, yc, ys, gt, hp, pp, post_w, pm_s, pc_s, sq_s, gi_s, gf_s, state_mlstm_C,
            state_mlstm_n, m0, state_conv, bi, bf, conv_w, b_sb, page_table, cache_k, cache_v, c_s)
        hs = _post(final, li, ym_s.reshape(nseq, WIDTH), yc_s.reshape(nseq, WIDTH), ys_s.reshape(nseq, WIDTH),
                   gt_s, hs, ps, *post_w, tm_s)
        outs_s.append((n_s, m_s[:, 0, :HEADS], cv_s))

    stack = lambda outs, j: jnp.stack([o[j] for o in outs])
    k_p, v_p = (a.reshape(depth, batch, seq, HEADS, HEAD_DIM) for a in kv_p)
    k_s, v_s = (a.reshape(depth, nseq, 1, HEADS, HEAD_DIM) for a in kv_s)
    return ((hp.reshape(batch, seq, D_MODEL), hs.reshape(nseq, 1, D_MODEL), k_p, v_p)
            + tuple(stack(outs_p, j) for j in range(4)) + (k_s, v_s, c_s)
            + tuple(stack(outs_s, j) for j in range(3)))
```

```python
import functools

import jax
import jax.numpy as jnp
from jax import lax
from jax.experimental import pallas as pl
from jax.experimental.pallas import tpu as pltpu

F32 = jnp.float32
BF16 = jnp.bfloat16

D_MODEL = 1024
WIDTH = 512
HEADS = 4
HEAD_DIM = 128
FFN_DIM = 4 * D_MODEL
PLE_DIM = 256
RMS_EPS = 1e-6
LOG2E = 1.4426950408889634
LANES = 128
SUBLANES = 8

T_PC = 0
T_SQ = 3 * WIDTH
T_SK = 4 * WIDTH
T_SV = 5 * WIDTH
T_GT = 6 * WIDTH
N_HEAD_COLS = 4 * WIDTH
N_TAIL_COLS = 6 * WIDTH + 3 * D_MODEL
FFN_CHUNK = 1024

MLSTM_CHUNK = 256
SB_BLOCK = 512
SB_K_BLOCK = 256
VMEM_LIMIT = 56 * 1024 * 1024


def _softplus(x):
    return jnp.maximum(x, 0.0) + jnp.log(1.0 + jnp.exp(-jnp.abs(x)))


def _stick_terms(z):
    drop = jnp.maximum(z, 0.0) + jnp.log2(1.0 + jnp.exp2(-jnp.abs(z)))
    return drop, z - drop


def _rms(x, g):
    return x * lax.rsqrt(jnp.mean(x * x, axis=-1, keepdims=True) + RMS_EPS) * g


def _resident(shape, layer=None):
    nd = len(shape)
    if layer is None:
        return pl.BlockSpec(shape, lambda *_: (0,) * nd, pipeline_mode=pl.Buffered(1))
    return pl.BlockSpec((None,) + tuple(shape), lambda *_: (layer,) + (0,) * nd, pipeline_mode=pl.Buffered(1))


def _proj_kernel(first_layer, x_ref, g_ref, wh_ref, wt_ref, wif_ref, *refs):
    pm_ref, pc_ref, sq_ref, kf_ref, vf_ref, kvb_ref, gt_ref, gi_ref, gf_ref = refs[-9:]
    xn = _rms(x_ref[...], g_ref[...]).astype(BF16)

    def mm(w_ref, c0, width):
        return lax.dot_general(xn, w_ref[c0:c0 + width, :], (((1,), (1,)), ((), ())), preferred_element_type=F32)

    scale = HEAD_DIM ** -0.5
    for j in range(4):
        part = mm(wh_ref, j * WIDTH, WIDTH)
        pm_ref[:, j * WIDTH:(j + 1) * WIDTH] = (part * scale if j == 1 else part).astype(BF16)
    for j in range(3):
        pc_ref[:, j * WIDTH:(j + 1) * WIDTH] = mm(wt_ref, T_PC + j * WIDTH, WIDTH).astype(BF16)
    sq_ref[...] = (mm(wt_ref, T_SQ, WIDTH) * (scale * LOG2E)).astype(BF16)
    k = mm(wt_ref, T_SK, WIDTH)
    v = mm(wt_ref, T_SV, WIDTH)
    tm = k.shape[0]
    slots = [kf_ref.at[l] for l in range(kf_ref.shape[0])] if first_layer else [kf_ref]
    vslots = [vf_ref.at[l] for l in range(vf_ref.shape[0])] if first_layer else [vf_ref]
    for h in range(HEADS):
        for kslot, vslot in zip(slots, vslots):
            kslot[pl.ds(h, tm, stride=HEADS), :] = k[:, h * HEAD_DIM:(h + 1) * HEAD_DIM]
            vslot[pl.ds(h, tm, stride=HEADS), :] = v[:, h * HEAD_DIM:(h + 1) * HEAD_DIM]
    kvb_ref[:, 0:WIDTH] = k.astype(BF16)
    kvb_ref[:, WIDTH:2 * WIDTH] = v.astype(BF16)
    for j in range(3 * D_MODEL // WIDTH):
        gt_ref[:, j * WIDTH:(j + 1) * WIDTH] = mm(wt_ref, T_GT + j * WIDTH, WIDTH).astype(BF16)
    gi_ref[...] = mm(wif_ref, 0, LANES)
    gf_ref[...] = mm(wif_ref, LANES, LANES)


def _proj(x2d, g, wh, wt, wif, tm, layer, depth, kv_prev):
    m = x2d.shape[0]
    first = kv_prev is None
    row = lambda width: pl.BlockSpec((tm, width), lambda i: (i, 0))
    if first:
        kv_spec = pl.BlockSpec((depth, tm * HEADS, HEAD_DIM), lambda i: (0, i, 0))
    else:
        kv_spec = pl.BlockSpec((None, tm * HEADS, HEAD_DIM), lambda i: (layer, i, 0))
    widths = (4 * WIDTH, 3 * WIDTH, WIDTH, None, None, 2 * WIDTH, 3 * D_MODEL, LANES, LANES)
    dtypes = (BF16, BF16, BF16, F32, F32, BF16, BF16, F32, F32)
    out_specs = [row(w) if w else kv_spec for w in widths]
    out_shape = [jax.ShapeDtypeStruct((m, w) if w else (depth, m * HEADS, HEAD_DIM), d)
                 for w, d in zip(widths, dtypes)]
    in_specs = [row(D_MODEL), _resident((1, D_MODEL), layer), _resident((N_HEAD_COLS, D_MODEL), layer),
                _resident((N_TAIL_COLS, D_MODEL), layer), _resident((2 * LANES, D_MODEL), layer)]
    args = [x2d, g, wh, wt, wif]
    aliases = {}
    if not first:
        in_specs += [pl.BlockSpec(memory_space=pl.ANY)] * 2
        args += list(kv_prev)
        aliases = {5: 3, 6: 4}
    return pl.pallas_call(
        functools.partial(_proj_kernel, first),
        grid=(m // tm,),
        in_specs=in_specs,
        out_specs=out_specs,
        out_shape=out_shape,
        input_output_aliases=aliases,
        compiler_params=pltpu.CompilerParams(dimension_semantics=("parallel",),
                                             vmem_limit_bytes=VMEM_LIMIT),
        name="proj",
    )(*args)


def _scan_rows(x, op, fill):
    n = x.shape[0]
    row = lax.broadcasted_iota(jnp.int32, x.shape, 0)
    k = 1
    while k < n:
        shifted = pltpu.roll(x, k, axis=0)
        x = op(x, jnp.where(row >= k, shifted, fill(x)))
        k *= 2
    return x


def _mlstm_conv_kernel(pm_ref, pc_ref, gi_ref, gf_ref, bi_ref, bf_ref, cw_ref,
                       ym_ref, yc_ref, s_ref, m_ref, cs_ref):
    c = pl.program_id(1)
    L = pm_ref.shape[0]

    @pl.when(c == 0)
    def _():
        s_ref[...] = jnp.zeros_like(s_ref)
        m_ref[...] = jnp.zeros_like(m_ref)
        cs_ref[...] = jnp.zeros_like(cs_ref)

    cb = pc_ref[:, 0:WIDTH].astype(F32)
    xc = pc_ref[:, WIDTH:2 * WIDTH].astype(F32) * pc_ref[:, 2 * WIDTH:3 * WIDTH].astype(F32)
    prev = cs_ref[...]
    row = lax.broadcasted_iota(jnp.int32, xc.shape, 0)
    xc1 = jnp.where(row == 0, prev[1:2, :], pltpu.roll(xc, 1, axis=0))
    xc2 = jnp.where(row == 0, prev[0:1, :], jnp.where(row == 1, prev[1:2, :], pltpu.roll(xc, 2, axis=0)))
    cw = cw_ref[...]
    yc_ref[...] = (cb * (cw[0:1, :] * xc2 + cw[1:2, :] * xc1 + cw[2:3, :] * xc)).astype(BF16)
    cs_ref[...] = xc[L - 2:L, :]

    m_prev = m_ref[...]
    ig = gi_ref[...] + bi_ref[...]
    logf = -_softplus(-(gf_ref[...] + bf_ref[...]))
    b = _scan_rows(logf, jnp.add, jnp.zeros_like)
    a = ig - b
    mx = jnp.maximum(m_prev, _scan_rows(a, jnp.maximum, lambda x: x))
    m_t = b + mx
    a_t = a.T
    s_in = jnp.exp(m_prev - mx)
    floor = jnp.exp(-m_t)
    mx_last = mx[L - 1:L, :]
    wk = jnp.exp(a - mx_last)
    decay = jnp.exp(m_prev - mx_last)
    m_ref[...] = m_t[L - 1:L, :]

    r_i = lax.broadcasted_iota(jnp.int32, (L, L), 0)
    c_i = lax.broadcasted_iota(jnp.int32, (L, L), 1)
    causal = c_i <= r_i
    one_col = (lax.broadcasted_iota(jnp.int32, (L, HEAD_DIM), 1) == 0).astype(BF16)
    for h in range(HEADS):
        sl = slice(h * HEAD_DIM, (h + 1) * HEAD_DIM)
        q = pm_ref[:, sl]
        k = pm_ref[:, WIDTH + h * HEAD_DIM:WIDTH + (h + 1) * HEAD_DIM]
        v = pm_ref[:, 2 * WIDTH + h * HEAD_DIM:2 * WIDTH + (h + 1) * HEAD_DIM]
        o = pm_ref[:, 3 * WIDTH + h * HEAD_DIM:3 * WIDTH + (h + 1) * HEAD_DIM]
        vext = jnp.concatenate([v, one_col], axis=1)
        qk = lax.dot_general(q, k, (((1,), (1,)), ((), ())), preferred_element_type=F32)
        e = jnp.exp(a_t[h:h + 1, :] - mx[:, h:h + 1])
        w = jnp.where(causal, e, 0.0) * qk
        intra = jnp.dot(w.astype(BF16), vext, preferred_element_type=F32)
        state = s_ref[h]
        inter = jnp.dot(q, state.astype(BF16), preferred_element_type=F32)
        sc = s_in[:, h:h + 1]
        num = sc * inter[:, :HEAD_DIM] + intra[:, :HEAD_DIM]
        den = sc * inter[:, HEAD_DIM:HEAD_DIM + 1] + intra[:, HEAD_DIM:HEAD_DIM + 1]
        inv = 1.0 / jnp.maximum(jnp.abs(den), floor[:, h:h + 1])
        ym_ref[:, sl] = (jax.nn.sigmoid(o.astype(F32)) * (num * inv)).astype(BF16)
        kw_t = (k.astype(F32) * wk[:, h:h + 1]).T.astype(BF16)
        s_ref[h] = decay[:, h:h + 1] * state + jnp.dot(kw_t, vext, preferred_element_type=F32)


def _mlstm_conv(layer, pm, pc, gi, gf, bi, bf, cw, batch, seq):
    L = MLSTM_CHUNK
    nc = seq // L
    row = lambda width: pl.BlockSpec((L, width), lambda b, c: (b * nc + c, 0))
    m = batch * seq
    return pl.pallas_call(
        _mlstm_conv_kernel,
        grid=(batch, nc),
        in_specs=[row(4 * WIDTH), row(3 * WIDTH), row(LANES), row(LANES),
                  _resident((1, LANES), layer), _resident((1, LANES), layer), _resident((3, WIDTH), layer)],
        out_specs=[row(WIDTH), row(WIDTH),
                   pl.BlockSpec((None, HEADS, HEAD_DIM, 2 * HEAD_DIM), lambda b, c: (b, 0, 0, 0)),
                   pl.BlockSpec((None, 1, LANES), lambda b, c: (b, 0, 0)),
                   pl.BlockSpec((None, 2, WIDTH), lambda b, c: (b, 0, 0))],
        out_shape=[jax.ShapeDtypeStruct((m, WIDTH), BF16), jax.ShapeDtypeStruct((m, WIDTH), BF16),
                   jax.ShapeDtypeStruct((batch, HEADS, HEAD_DIM, 2 * HEAD_DIM), F32),
                   jax.ShapeDtypeStruct((batch, 1, LANES), F32),
                   jax.ShapeDtypeStruct((batch, 2, WIDTH), F32)],
        compiler_params=pltpu.CompilerParams(dimension_semantics=("parallel", "arbitrary"),
                                             vmem_limit_bytes=VMEM_LIMIT),
        name="mlstm_conv",
    )(pm, pc, gi, gf, bi, bf, cw)


def _sb_kernel(layer, bsb_ref, q_ref, kv_ref, o_ref, acc_ref, run_ref):
    i = pl.program_id(1)
    tq = q_ref.shape[0]
    tk = SB_K_BLOCK
    r = tq // tk
    r_i = lax.broadcasted_iota(jnp.int32, (tk, tk), 0)
    c_i = lax.broadcasted_iota(jnp.int32, (tk, tk), 1)
    after = (r_i > c_i).astype(BF16)
    visible = c_i < r_i
    acc_ref[...] = jnp.zeros_like(acc_ref)
    run_ref[...] = jnp.zeros_like(run_ref)

    def tile(kb, rows, diagonal):
        k0 = pl.multiple_of(kb * tk, tk)
        for h in range(HEADS):
            sl = slice(h * HEAD_DIM, (h + 1) * HEAD_DIM)
            k = kv_ref[pl.ds(k0, tk), sl]
            v = kv_ref[pl.ds(k0, tk), WIDTH + h * HEAD_DIM:WIDTH + (h + 1) * HEAD_DIM]
            z = (lax.dot_general(q_ref[rows, sl], k, (((1,), (1,)), ((), ())), preferred_element_type=F32)
                 + bsb_ref[layer, h] * LOG2E)
            drop, log_beta = _stick_terms(z)
            if diagonal:
                drop = jnp.where(visible, drop, 0.0)
            run = run_ref[h, rows]
            later = run + jnp.dot(drop.astype(BF16), after, preferred_element_type=F32)
            a = jnp.exp2(log_beta - later)
            if diagonal:
                a = jnp.where(visible, a, 0.0)
            acc_ref[rows, sl] += jnp.dot(a.astype(BF16), v, preferred_element_type=F32)
            run_ref[h, rows] = run + jnp.sum(drop, axis=1, keepdims=True)

    for d in reversed(range(r)):
        tile(i * r + d, slice(d * tk, (d + 1) * tk), True)
        if d + 1 < r:
            tile(i * r + d, slice((d + 1) * tk, tq), False)

    def body(j, carry):
        for u in range(r):
            tile(i * r - 1 - r * j - u, slice(0, tq), False)
        return carry

    lax.fori_loop(0, i, body, 0)
    o_ref[...] = acc_ref[...].astype(BF16)


def _sb_prompt(layer, sq, kvb, bsb, batch, seq):
    blk = SB_BLOCK
    nq = seq // blk
    return pl.pallas_call(
        functools.partial(_sb_kernel, layer),
        grid=(batch, nq),
        in_specs=[pl.BlockSpec(memory_space=pltpu.SMEM),
                  pl.BlockSpec((blk, WIDTH), lambda b, i: (b * nq + i, 0)),
                  pl.BlockSpec((seq, 2 * WIDTH), lambda b, i: (b, 0))],
        out_specs=pl.BlockSpec((blk, WIDTH), lambda b, i: (b * nq + i, 0)),
        out_shape=jax.ShapeDtypeStruct((batch * seq, WIDTH), BF16),
        scratch_shapes=[pltpu.VMEM((blk, WIDTH), F32), pltpu.VMEM((HEADS, blk, 1), F32)],
        compiler_params=pltpu.CompilerParams(dimension_semantics=("parallel", "parallel"),
                                             vmem_limit_bytes=VMEM_LIMIT),
        name="sb_prompt",
    )(bsb, sq, kvb)


def _decode_seq(layer, bsb_ref, pm_ref, pc_ref, sq_ref, gi_ref, gf_ref, c0_ref, n0_ref, m0_ref, cv0_ref, bi_ref,
                bf_ref, cw_ref, k_pages, v_pages, ym_ref, yc_ref, ys_ref, c_slots, n_ref, m_ref, cv_ref):
    n_pages = len(k_pages)
    cw = cw_ref[...]
    pw = k_pages[0].shape[0]

    cb = pc_ref[0, :, 0:WIDTH].astype(F32)
    xc = pc_ref[0, :, WIDTH:2 * WIDTH].astype(F32) * pc_ref[0, :, 2 * WIDTH:3 * WIDTH].astype(F32)
    prev = cv0_ref[0]
    yc_ref[0] = (cb * (cw[0:1, :] * prev[0:1, :] + cw[1:2, :] * prev[1:2, :] + cw[2:3, :] * xc)).astype(BF16)
    cv_ref[0, 0:1, :] = prev[1:2, :]
    cv_ref[0, 1:2, :] = xc

    ig = gi_ref[0] + bi_ref[...]
    logf = -_softplus(-(gf_ref[0] + bf_ref[...]))
    m0 = m0_ref[0]
    m_t = jnp.maximum(logf + m0, ig)
    w_in = jnp.exp(ig - m_t)
    s_in = jnp.exp(logf + m0 - m_t)
    floor = jnp.exp(-m_t)
    m_ref[0] = m_t
    row128 = lax.broadcasted_iota(jnp.int32, (HEAD_DIM, HEAD_DIM), 0)
    for h in range(HEADS):
        sl = slice(h * HEAD_DIM, (h + 1) * HEAD_DIM)
        q = pm_ref[0, :, sl].astype(F32)
        k = pm_ref[0, :, WIDTH + h * HEAD_DIM:WIDTH + (h + 1) * HEAD_DIM].astype(F32)
        v = pm_ref[0, :, 2 * WIDTH + h * HEAD_DIM:2 * WIDTH + (h + 1) * HEAD_DIM].astype(F32)
        o = pm_ref[0, :, 3 * WIDTH + h * HEAD_DIM:3 * WIDTH + (h + 1) * HEAD_DIM].astype(F32)
        qk_rows = jnp.where(row128 == 0, q, jnp.where(row128 == 1, k, 0.0))
        cols = qk_rows.T
        q_col = cols[:, 0:1]
        k_col = cols[:, 1:2]
        c0 = c0_ref[0, h]
        n0 = n0_ref[0, h:h + 1, :]
        sc = s_in[:, h:h + 1]
        wi = w_in[:, h:h + 1]
        w = wi * jnp.sum(q * k, axis=1, keepdims=True)
        num = sc * jnp.sum(q_col * c0, axis=0, keepdims=True) + w * v
        den = sc * jnp.sum(q * n0, axis=1, keepdims=True) + w
        hc = num * (1.0 / jnp.maximum(jnp.abs(den), floor[:, h:h + 1]))
        ym_ref[0, :, sl] = (jax.nn.sigmoid(o) * hc).astype(BF16)
        c_new = sc * c0 + wi * (k_col * v)
        for slot in c_slots:
            slot[0, h] = c_new
        n_ref[0, h:h + 1, :] = sc * n0 + wi * k

    r8 = lax.broadcasted_iota(jnp.int32, (SUBLANES, HEAD_DIM), 0)
    rcol = lax.broadcasted_iota(jnp.int32, (SUBLANES, 1), 0)
    sq = sq_ref[0].astype(F32)
    q_rows = jnp.zeros((SUBLANES, HEAD_DIM), F32)
    bias = jnp.zeros((SUBLANES, 1), F32)
    for h in range(HEADS):
        q_rows = jnp.where(r8 == h, sq[:, h * HEAD_DIM:(h + 1) * HEAD_DIM], q_rows)
        bias = jnp.where(rcol == h, bsb_ref[layer, h] * LOG2E, bias)
    q_rows = q_rows.astype(BF16)
    own = (lax.broadcasted_iota(jnp.int32, (SUBLANES, pw), 1) % HEADS
           == lax.broadcasted_iota(jnp.int32, (SUBLANES, pw), 0))
    r_i = lax.broadcasted_iota(jnp.int32, (pw, pw), 0)
    c_i = lax.broadcasted_iota(jnp.int32, (pw, pw), 1)
    after = jnp.logical_and(r_i // HEADS > c_i // HEADS, r_i % HEADS == c_i % HEADS).astype(BF16)
    k_all = jnp.concatenate([kp[...].astype(BF16) for kp in k_pages], axis=0)
    z = lax.dot_general(q_rows, k_all, (((1,), (1,)), ((), ())), preferred_element_type=F32) + bias
    drop, log_beta = _stick_terms(z)
    own_all = jnp.concatenate([own] * n_pages, axis=1)
    drop = jnp.where(own_all, drop, 0.0)
    drop_rows = jnp.concatenate([drop[:, j * pw:(j + 1) * pw] for j in range(n_pages)], axis=0)
    within = jnp.dot(drop_rows.astype(BF16), after, preferred_element_type=F32)
    totals = jnp.sum(drop_rows, axis=1, keepdims=True)
    run = jnp.zeros((SUBLANES, 1), F32)
    laters = [None] * n_pages
    for j in reversed(range(n_pages)):
        rows = slice(SUBLANES * j, SUBLANES * (j + 1))
        laters[j] = within[rows, :] + run
        run = run + totals[rows, :]
    later = jnp.concatenate(laters, axis=1)
    a = jnp.where(own_all, jnp.exp2(log_beta - later), 0.0)
    v_all = jnp.concatenate([vp[...].astype(BF16) for vp in v_pages], axis=0)
    out = jnp.dot(a.astype(BF16), v_all, preferred_element_type=F32)
    for h in range(HEADS):
        ys_ref[0, :, h * HEAD_DIM:(h + 1) * HEAD_DIM] = out[h:h + 1, :].astype(BF16)


def _post_kernel(final, ym_ref, yc_ref, ys_ref, gt_ref, h_ref, p_ref, wb_ref, wo_ref, gmlp_ref, wup_ref,
                 wdn_ref, gple_ref, wpg_ref, wple_ref, gfin_ref, o_ref):
    mix = None
    for n, y_ref in enumerate((ym_ref, yc_ref, ys_ref)):
        pb = jnp.dot(y_ref[...], wb_ref[n], preferred_element_type=F32)
        gate = jax.nn.sigmoid(gt_ref[:, n * D_MODEL:(n + 1) * D_MODEL].astype(F32))
        mix = gate * pb if mix is None else mix + gate * pb
    h = h_ref[...] + jnp.dot(mix.astype(BF16), wo_ref[...], preferred_element_type=F32)

    xn = _rms(h, gmlp_ref[...]).astype(BF16)
    for c in range(FFN_DIM // FFN_CHUNK):
        u = jnp.dot(xn, wup_ref[:, c * FFN_CHUNK:(c + 1) * FFN_CHUNK], preferred_element_type=F32)
        r = jnp.square(jnp.maximum(u, 0.0)).astype(BF16)
        h = h + jnp.dot(r, wdn_ref[c * FFN_CHUNK:(c + 1) * FFN_CHUNK, :], preferred_element_type=F32)

    gate = jax.nn.sigmoid(jnp.dot(_rms(h, gple_ref[...]).astype(BF16), wpg_ref[...],
                                  preferred_element_type=F32))
    h = h + jnp.dot(p_ref[...].astype(BF16), wple_ref[...], preferred_element_type=F32) * gate
    o_ref[...] = _rms(h, gfin_ref[...]) if final else h


def _post(final, layer, ym, yc, ys, gt, h, p, wb, wo, gmlp, wup, wdn, gple, wpg, wple, gfin, tm):
    m = h.shape[0]
    row = lambda width: pl.BlockSpec((tm, width), lambda i: (i, 0))
    p_spec = pl.BlockSpec((None, tm, PLE_DIM), lambda i: (layer, i, 0))
    return pl.pallas_call(
        functools.partial(_post_kernel, final),
        grid=(m // tm,),
        in_specs=[row(WIDTH), row(WIDTH), row(WIDTH), row(3 * D_MODEL), row(D_MODEL), p_spec,
                  _resident((3, WIDTH, D_MODEL), layer), _resident((D_MODEL, D_MODEL), layer),
                  _resident((1, D_MODEL), layer), _resident((D_MODEL, FFN_DIM), layer),
                  _resident((FFN_DIM, D_MODEL), layer), _resident((1, D_MODEL), layer),
                  _resident((D_MODEL, D_MODEL), layer), _resident((PLE_DIM, D_MODEL), layer),
                  _resident((1, D_MODEL))],
        out_specs=row(D_MODEL),
        out_shape=jax.ShapeDtypeStruct((m, D_MODEL), F32),
        compiler_params=pltpu.CompilerParams(dimension_semantics=("parallel",),
                                             vmem_limit_bytes=VMEM_LIMIT),
        name="post",
    )(ym, yc, ys, gt, h, p, wb, wo, gmlp, wup, wdn, gple, wpg, wple, gfin)


N_POST_IN = 15
N_DECODE_IN = 12


def _post_decode_kernel(final, layer, n_pages, first_layer, pt_ref, bsb_ref, *refs):
    (ym_ref, yc_ref, ys_ref, gt_ref, h_ref, p_ref, wb_ref, wo_ref, gmlp_ref, wup_ref, wdn_ref, gple_ref,
     wpg_ref, wple_ref, gfin_ref) = refs[:N_POST_IN]
    dec_in = refs[N_POST_IN:N_POST_IN + N_DECODE_IN]
    ck_hbm, cv_hbm = refs[N_POST_IN + N_DECODE_IN:N_POST_IN + N_DECODE_IN + 2]
    (o_ref, ym_s_ref, yc_s_ref, ys_s_ref, c_ref, n_ref, m_ref, cv_ref,
     acc_ref, xn_ref, kbuf, vbuf, sem) = refs[-13:]
    t = pl.program_id(0)
    slot = t % 2

    def page_copies(seq, buf):
        copies = []
        for j in range(n_pages):
            pid = pt_ref[seq, j]
            copies.append(pltpu.make_async_copy(ck_hbm.at[layer, pid], kbuf.at[buf, j], sem.at[buf]))
            copies.append(pltpu.make_async_copy(cv_hbm.at[layer, pid], vbuf.at[buf, j], sem.at[buf]))
        return copies

    @pl.when(t == 0)
    def _():
        for cp in page_copies(0, 0):
            cp.start()

    @pl.when(t + 1 < pl.num_programs(0))
    def _():
        for cp in page_copies(t + 1, 1 - slot):
            cp.start()

    for cp in page_copies(t, slot):
        cp.wait()
    k_pages = [kbuf.at[slot, j] for j in range(n_pages)]
    v_pages = [vbuf.at[slot, j] for j in range(n_pages)]
    c_slots = [c_ref.at[l] for l in range(c_ref.shape[0])] if first_layer else [c_ref]
    half = FFN_DIM // 2

    def decode():
        _decode_seq(layer, bsb_ref, *dec_in, k_pages, v_pages, ym_s_ref, yc_s_ref, ys_s_ref, c_slots, n_ref, m_ref,
                    cv_ref)

    def mlp_half(xn, h, base):
        for c in range(half // FFN_CHUNK):
            lo = base + c * FFN_CHUNK
            u = jnp.dot(xn, wup_ref[:, lo:lo + FFN_CHUNK], preferred_element_type=F32)
            r = jnp.square(jnp.maximum(u, 0.0)).astype(BF16)
            h = h + jnp.dot(r, wdn_ref[lo:lo + FFN_CHUNK, :], preferred_element_type=F32)
        return h

    parity = pl.program_id(0) % 2

    @pl.when(parity == 0)
    def _():
        decode()
        mix = None
        for n, y_ref in enumerate((ym_ref, yc_ref, ys_ref)):
            pb = jnp.dot(y_ref[...], wb_ref[n], preferred_element_type=F32)
            gate = jax.nn.sigmoid(gt_ref[:, n * D_MODEL:(n + 1) * D_MODEL].astype(F32))
            mix = gate * pb if mix is None else mix + gate * pb
        h = h_ref[...] + jnp.dot(mix.astype(BF16), wo_ref[...], preferred_element_type=F32)
        xn = _rms(h, gmlp_ref[...]).astype(BF16)
        xn_ref[...] = xn
        acc_ref[...] = mlp_half(xn, h, 0)

    @pl.when(parity == 1)
    def _():
        decode()
        h = mlp_half(xn_ref[...], acc_ref[...], half)
        gate = jax.nn.sigmoid(jnp.dot(_rms(h, gple_ref[...]).astype(BF16), wpg_ref[...],
                                      preferred_element_type=F32))
        h = h + jnp.dot(p_ref[...].astype(BF16), wple_ref[...], preferred_element_type=F32) * gate
        o_ref[...] = _rms(h, gfin_ref[...]) if final else h


def _post_decode(final, layer, depth, ym, yc, ys, gt, h, p, post_w, pm, pc, sq, gi, gf, state_c, state_n, m0,
                 state_conv, bi, bf, cw, bsb, page_table, cache_k, cache_v, c_prev):
    nseq, n_pages = page_table.shape
    pw = cache_k.shape[2]
    m = h.shape[0]
    tm = 2 * m // nseq
    assert tm * nseq == 2 * m and tm % 16 == 0
    first = c_prev is None
    prow = lambda width: pl.BlockSpec((tm, width), lambda t, pt: (t // 2, 0))
    fixed = lambda shape: _resident(shape, layer)
    srow = lambda width: pl.BlockSpec((1, 1, width), lambda t, pt: (t, 0, 0))
    state4 = pl.BlockSpec((None, 1, HEADS, HEAD_DIM, HEAD_DIM), lambda t, pt: (layer, t, 0, 0, 0))
    state3 = pl.BlockSpec((None, 1, HEADS, HEAD_DIM), lambda t, pt: (layer, t, 0, 0))
    conv3 = pl.BlockSpec((None, 1, 2, WIDTH), lambda t, pt: (layer, t, 0, 0))

    post_specs = [prow(WIDTH), prow(WIDTH), prow(WIDTH), prow(3 * D_MODEL), prow(D_MODEL),
                  pl.BlockSpec((None, tm, PLE_DIM), lambda t, pt: (layer, t // 2, 0)),
                  fixed((3, WIDTH, D_MODEL)), fixed((D_MODEL, D_MODEL)), fixed((1, D_MODEL)),
                  fixed((D_MODEL, FFN_DIM)), fixed((FFN_DIM, D_MODEL)), fixed((1, D_MODEL)),
                  fixed((D_MODEL, D_MODEL)), fixed((PLE_DIM, D_MODEL)), _resident((1, D_MODEL))]
    dec_specs = [srow(4 * WIDTH), srow(3 * WIDTH), srow(WIDTH), srow(LANES), srow(LANES), state4, state3, srow(LANES),
                 conv3, fixed((1, LANES)), fixed((1, LANES)), fixed((3, WIDTH))]
    assert len(post_specs) == N_POST_IN and len(dec_specs) == N_DECODE_IN
    in_specs = ([pl.BlockSpec(memory_space=pltpu.SMEM)] + post_specs + dec_specs
                + [pl.BlockSpec(memory_space=pl.ANY)] * 2)
    args = ([page_table, bsb, ym, yc, ys, gt, h, p, *post_w]
            + [a.reshape(nseq, 1, a.shape[-1]) for a in (pm, pc, sq, gi, gf)]
            + [state_c, state_n, m0, state_conv, bi, bf, cw, cache_k, cache_v])
    if first:
        c_spec = pl.BlockSpec((depth, 1, HEADS, HEAD_DIM, HEAD_DIM), lambda t, pt: (0, t, 0, 0, 0))
        aliases = {}
    else:
        c_spec = pl.BlockSpec((None, 1, HEADS, HEAD_DIM, HEAD_DIM), lambda t, pt: (layer, t, 0, 0, 0))
        in_specs.append(pl.BlockSpec(memory_space=pl.ANY))
        args.append(c_prev)
        aliases = {len(args) - 1: 4}
    out_specs = [prow(D_MODEL), srow(WIDTH), srow(WIDTH), srow(WIDTH), c_spec,
                 pl.BlockSpec((1, HEADS, HEAD_DIM), lambda t, pt: (t, 0, 0)),
                 srow(LANES),
                 pl.BlockSpec((1, 2, WIDTH), lambda t, pt: (t, 0, 0))]
    out_shape = [jax.ShapeDtypeStruct((m, D_MODEL), F32)] + [jax.ShapeDtypeStruct((nseq, 1, WIDTH), BF16)] * 3 + [
        jax.ShapeDtypeStruct((depth, nseq, HEADS, HEAD_DIM, HEAD_DIM), F32),
        jax.ShapeDtypeStruct((nseq, HEADS, HEAD_DIM), F32),
        jax.ShapeDtypeStruct((nseq, 1, LANES), F32),
        jax.ShapeDtypeStruct((nseq, 2, WIDTH), F32)]
    return pl.pallas_call(
        functools.partial(_post_decode_kernel, final, layer, n_pages, first),
        grid_spec=pltpu.PrefetchScalarGridSpec(
            num_scalar_prefetch=1, grid=(nseq,), in_specs=in_specs, out_specs=out_specs,
            scratch_shapes=[pltpu.VMEM((tm, D_MODEL), F32), pltpu.VMEM((tm, D_MODEL), BF16),
                            pltpu.VMEM((2, n_pages, pw, HEAD_DIM), F32), pltpu.VMEM((2, n_pages, pw, HEAD_DIM), F32),
                            pltpu.SemaphoreType.DMA((2,))]),
        out_shape=out_shape,
        input_output_aliases=aliases,
        compiler_params=pltpu.CompilerParams(dimension_semantics=("arbitrary",),
                                             vmem_limit_bytes=VMEM_LIMIT),
        name="post_decode",
    )(*args)


def kernel(x_prompt, x_sample, cache_k, cache_v, state_mlstm_C, state_mlstm_n, state_mlstm_m, state_conv,
           page_table, p_prompt, p_sample, g_mix, w_in, b_if, b_sb, conv_w, w_branch, w_out, g_mlp, w_up,
           w_down, g_ple, w_ple_gate, w_ple, g_final):
    depth = w_in.shape[0]
    batch, seq, _ = x_prompt.shape
    nseq = x_sample.shape[0]
    n_phys, page = cache_k.shape[1], cache_k.shape[2]
    cache_k = cache_k.reshape(depth, n_phys, page * HEADS, HEAD_DIM)
    cache_v = cache_v.reshape(depth, n_phys, page * HEADS, HEAD_DIM)
    hp = x_prompt.reshape(batch * seq, D_MODEL)
    hs = x_sample.reshape(nseq, D_MODEL)
    pp = p_prompt.reshape(depth, batch * seq, PLE_DIM)
    ps = p_sample.reshape(depth, nseq, PLE_DIM)
    gfin = g_final.reshape(1, D_MODEL)
    tm_p = 512
    tm_s = nseq

    vec = lambda g: g.reshape(depth, 1, D_MODEL)
    w_t = jnp.swapaxes(w_in, 1, 2)
    gates0 = N_HEAD_COLS
    wh = w_t[:, :gates0].astype(BF16)
    wt = w_t[:, gates0 + 2 * HEADS:].astype(BF16)
    zpad = jnp.zeros((depth, LANES - HEADS, D_MODEL), w_in.dtype)
    wif = jnp.concatenate([w_t[:, gates0:gates0 + HEADS], zpad,
                           w_t[:, gates0 + HEADS:gates0 + 2 * HEADS], zpad], axis=1).astype(BF16)
    gmix = vec(g_mix)
    bi = jnp.pad(b_if[:, :HEADS], ((0, 0), (0, LANES - HEADS))).reshape(depth, 1, LANES)
    bf = jnp.pad(b_if[:, HEADS:], ((0, 0), (0, LANES - HEADS))).reshape(depth, 1, LANES)
    post_w = (w_branch.astype(BF16), w_out.astype(BF16), vec(g_mlp), w_up.astype(BF16), w_down.astype(BF16),
              vec(g_ple), w_ple_gate.astype(BF16), w_ple.astype(BF16), gfin)
    m0_all = jnp.pad(state_mlstm_m, ((0, 0), (0, 0), (0, LANES - HEADS)))

    outs_p, outs_s = [], []
    kv_p = kv_s = c_s = None
    for li in range(depth):
        final = li == depth - 1

        pm, pc, sq, kf, vf, kvb, gt, gi, gf = _proj(hp, gmix, wh, wt, wif, tm_p, li, depth, kv_p)
        kv_p = (kf, vf)
        ym, yc, s_p, m_p, cv_p = _mlstm_conv(li, pm, pc, gi, gf, bi, bf, conv_w, batch, seq)
        ys = _sb_prompt(li, sq, kvb, b_sb, batch, seq)
        outs_p.append((s_p[..., :HEAD_DIM], s_p[..., HEAD_DIM], m_p[:, 0, :HEADS], cv_p))
        pm_s, pc_s, sq_s, kf, vf, _, gt_s, gi_s, gf_s = _proj(hs, gmix, wh, wt, wif, tm_s, li, depth, kv_s)
        kv_s = (kf, vf)
        m0 = m0_all[li].reshape(nseq, 1, LANES)

        hp, ym_s, yc_s, ys_s, c_s, n_s, m_s, cv_s = _post_decode(
            final, li, depth, ym, yc, ys, gt, hp, pp, post_w, pm_s, pc_s, sq_s, gi_s, gf_s, state_mlstm_C,
            state_mlstm_n, m0, state_conv, bi, bf, conv_w, b_sb, page_table, cache_k, cache_v, c_s)
        hs = _post(final, li, ym_s.reshape(nseq, WIDTH), yc_s.reshape(nseq, WIDTH), ys_s.reshape(nseq, WIDTH),
                   gt_s, hs, ps, *post_w, tm_s)
        outs_s.append((n_s, m_s[:, 0, :HEADS], cv_s))

    stack = lambda outs, j: jnp.stack([o[j] for o in outs])
    k_p, v_p = (a.reshape(depth, batch, seq, HEADS, HEAD_DIM) for a in kv_p)
    k_s, v_s = (a.reshape(depth, nseq, 1, HEADS, HEAD_DIM) for a in kv_s)
    return ((hp.reshape(batch, seq, D_MODEL), hs.reshape(nseq, 1, D_MODEL), k_p, v_p)
            + tuple(stack(outs_p, j) for j in range(4)) + (k_s, v_s, c_s)
            + tuple(stack(outs_s, j) for j in range(3)))
```

```python
import functools

import jax
import jax.numpy as jnp
from jax import lax
from jax.experimental import pallas as pl
from jax.experimental.pallas import tpu as pltpu

F32 = jnp.float32
BF16 = jnp.bfloat16

D_MODEL = 1024
WIDTH = 512
HEADS = 4
HEAD_DIM = 128
FFN_DIM = 4 * D_MODEL
PLE_DIM = 256
RMS_EPS = 1e-6
LOG2E = 1.4426950408889634
LANES = 128
SUBLANES = 8

T_PC = 0
T_SQ = 3 * WIDTH
T_SK = 4 * WIDTH
T_SV = 5 * WIDTH
T_GT = 6 * WIDTH
N_HEAD_COLS = 4 * WIDTH
N_TAIL_COLS = 6 * WIDTH + 3 * D_MODEL
FFN_CHUNK = 1024

MLSTM_CHUNK = 256
SB_BLOCK = 512
SB_K_BLOCK = 256
VMEM_LIMIT = 56 * 1024 * 1024


def _softplus(x):
    return jnp.maximum(x, 0.0) + jnp.log(1.0 + jnp.exp(-jnp.abs(x)))


def _stick_terms(z):
    drop = jnp.maximum(z, 0.0) + jnp.log2(1.0 + jnp.exp2(-jnp.abs(z)))
    return drop, z - drop


def _rms(x, g):
    return x * lax.rsqrt(jnp.mean(x * x, axis=-1, keepdims=True) + RMS_EPS) * g


def _resident(shape, layer=None):
    nd = len(shape)
    if layer is None:
        return pl.BlockSpec(shape, lambda *_: (0,) * nd, pipeline_mode=pl.Buffered(1))
    return pl.BlockSpec((None,) + tuple(shape), lambda *_: (layer,) + (0,) * nd, pipeline_mode=pl.Buffered(1))


def _proj_kernel(first_layer, x_ref, g_ref, wh_ref, wt_ref, wif_ref, *refs):
    pm_ref, pc_ref, sq_ref, kf_ref, vf_ref, kvb_ref, gt_ref, gi_ref, gf_ref = refs[-9:]
    xn = _rms(x_ref[...], g_ref[...]).astype(BF16)

    def mm(w_ref, c0, width):
        return lax.dot_general(xn, w_ref[c0:c0 + width, :], (((1,), (1,)), ((), ())), preferred_element_type=F32)

    scale = HEAD_DIM ** -0.5
    for j in range(4):
        part = mm(wh_ref, j * WIDTH, WIDTH)
        pm_ref[:, j * WIDTH:(j + 1) * WIDTH] = (part * scale if j == 1 else part).astype(BF16)
    for j in range(3):
        pc_ref[:, j * WIDTH:(j + 1) * WIDTH] = mm(wt_ref, T_PC + j * WIDTH, WIDTH).astype(BF16)
    sq_ref[...] = (mm(wt_ref, T_SQ, WIDTH) * (scale * LOG2E)).astype(BF16)
    k = mm(wt_ref, T_SK, WIDTH)
    v = mm(wt_ref, T_SV, WIDTH)
    tm = k.shape[0]
    slots = [kf_ref.at[l] for l in range(kf_ref.shape[0])] if first_layer else [kf_ref]
    vslots = [vf_ref.at[l] for l in range(vf_ref.shape[0])] if first_layer else [vf_ref]
    for h in range(HEADS):
        for kslot, vslot in zip(slots, vslots):
            kslot[pl.ds(h, tm, stride=HEADS), :] = k[:, h * HEAD_DIM:(h + 1) * HEAD_DIM]
            vslot[pl.ds(h, tm, stride=HEADS), :] = v[:, h * HEAD_DIM:(h + 1) * HEAD_DIM]
    kvb_ref[:, 0:WIDTH] = k.astype(BF16)
    kvb_ref[:, WIDTH:2 * WIDTH] = v.astype(BF16)
    for j in range(3 * D_MODEL // WIDTH):
        gt_ref[:, j * WIDTH:(j + 1) * WIDTH] = mm(wt_ref, T_GT + j * WIDTH, WIDTH).astype(BF16)
    gi_ref[...] = mm(wif_ref, 0, LANES)
    gf_ref[...] = mm(wif_ref, LANES, LANES)


def _proj(x2d, g, wh, wt, wif, tm, layer, depth, kv_prev):
    m = x2d.shape[0]
    first = kv_prev is None
    row = lambda width: pl.BlockSpec((tm, width), lambda i: (i, 0))
    if first:
        kv_spec = pl.BlockSpec((depth, tm * HEADS, HEAD_DIM), lambda i: (0, i, 0))
    else:
        kv_spec = pl.BlockSpec((None, tm * HEADS, HEAD_DIM), lambda i: (layer, i, 0))
    widths = (4 * WIDTH, 3 * WIDTH, WIDTH, None, None, 2 * WIDTH, 3 * D_MODEL, LANES, LANES)
    dtypes = (BF16, BF16, BF16, F32, F32, BF16, BF16, F32, F32)
    out_specs = [row(w) if w else kv_spec for w in widths]
    out_shape = [jax.ShapeDtypeStruct((m, w) if w else (depth, m * HEADS, HEAD_DIM), d)
                 for w, d in zip(widths, dtypes)]
    in_specs = [row(D_MODEL), _resident((1, D_MODEL), layer), _resident((N_HEAD_COLS, D_MODEL), layer),
                _resident((N_TAIL_COLS, D_MODEL), layer), _resident((2 * LANES, D_MODEL), layer)]
    args = [x2d, g, wh, wt, wif]
    aliases = {}
    if not first:
        in_specs += [pl.BlockSpec(memory_space=pl.ANY)] * 2
        args += list(kv_prev)
        aliases = {5: 3, 6: 4}
    return pl.pallas_call(
        functools.partial(_proj_kernel, first),
        grid=(m // tm,),
        in_specs=in_specs,
        out_specs=out_specs,
        out_shape=out_shape,
        input_output_aliases=aliases,
        compiler_params=pltpu.CompilerParams(dimension_semantics=("parallel",),
                                             vmem_limit_bytes=VMEM_LIMIT),
        name="proj",
    )(*args)


def _scan_rows(x, op, fill):
    n = x.shape[0]
    row = lax.broadcasted_iota(jnp.int32, x.shape, 0)
    k = 1
    while k < n:
        shifted = pltpu.roll(x, k, axis=0)
        x = op(x, jnp.where(row >= k, shifted, fill(x)))
        k *= 2
    return x


def _mlstm_conv_chunk(rows, pm_ref, pc_ref, gi_ref, gf_ref, bi_ref, bf_ref, cw_ref,
                      ym_ref, yc_ref, s_ref, m_ref, cs_ref):
    L = rows.stop - rows.start

    cb = pc_ref[rows, 0:WIDTH].astype(F32)
    xc = pc_ref[rows, WIDTH:2 * WIDTH].astype(F32) * pc_ref[rows, 2 * WIDTH:3 * WIDTH].astype(F32)
    prev = cs_ref[...]
    row = lax.broadcasted_iota(jnp.int32, xc.shape, 0)
    xc1 = jnp.where(row == 0, prev[1:2, :], pltpu.roll(xc, 1, axis=0))
    xc2 = jnp.where(row == 0, prev[0:1, :], jnp.where(row == 1, prev[1:2, :], pltpu.roll(xc, 2, axis=0)))
    cw = cw_ref[...]
    yc_ref[rows, :] = (cb * (cw[0:1, :] * xc2 + cw[1:2, :] * xc1 + cw[2:3, :] * xc)).astype(BF16)
    cs_ref[...] = xc[L - 2:L, :]

    m_prev = m_ref[...]
    ig = gi_ref[rows, :] + bi_ref[...]
    logf = -_softplus(-(gf_ref[rows, :] + bf_ref[...]))
    b = _scan_rows(logf, jnp.add, jnp.zeros_like)
    a = ig - b
    mx = jnp.maximum(m_prev, _scan_rows(a, jnp.maximum, lambda x: x))
    m_t = b + mx
    a_t = a.T
    s_in = jnp.exp(m_prev - mx)
    floor = jnp.exp(-m_t)
    mx_last = mx[L - 1:L, :]
    wk = jnp.exp(a - mx_last)
    decay = jnp.exp(m_prev - mx_last)
    m_ref[...] = m_t[L - 1:L, :]

    r_i = lax.broadcasted_iota(jnp.int32, (L, L), 0)
    c_i = lax.broadcasted_iota(jnp.int32, (L, L), 1)
    causal = c_i <= r_i
    one_col = (lax.broadcasted_iota(jnp.int32, (L, HEAD_DIM), 1) == 0).astype(BF16)
    for h in range(HEADS):
        sl = slice(h * HEAD_DIM, (h + 1) * HEAD_DIM)
        q = pm_ref[rows, sl]
        k = pm_ref[rows, WIDTH + h * HEAD_DIM:WIDTH + (h + 1) * HEAD_DIM]
        v = pm_ref[rows, 2 * WIDTH + h * HEAD_DIM:2 * WIDTH + (h + 1) * HEAD_DIM]
        o = pm_ref[rows, 3 * WIDTH + h * HEAD_DIM:3 * WIDTH + (h + 1) * HEAD_DIM]
        vext = jnp.concatenate([v, one_col], axis=1)
        qk = lax.dot_general(q, k, (((1,), (1,)), ((), ())), preferred_element_type=F32)
        e = jnp.exp(a_t[h:h + 1, :] - mx[:, h:h + 1])
        w = jnp.where(causal, e, 0.0) * qk
        intra = jnp.dot(w.astype(BF16), vext, preferred_element_type=F32)
        state = s_ref[h]
        inter = jnp.dot(q, state.astype(BF16), preferred_element_type=F32)
        sc = s_in[:, h:h + 1]
        num = sc * inter[:, :HEAD_DIM] + intra[:, :HEAD_DIM]
        den = sc * inter[:, HEAD_DIM:HEAD_DIM + 1] + intra[:, HEAD_DIM:HEAD_DIM + 1]
        inv = 1.0 / jnp.maximum(jnp.abs(den), floor[:, h:h + 1])
        ym_ref[rows, sl] = (jax.nn.sigmoid(o.astype(F32)) * (num * inv)).astype(BF16)
        kw_t = (k.astype(F32) * wk[:, h:h + 1]).T.astype(BF16)
        s_ref[h] = decay[:, h:h + 1] * state + jnp.dot(kw_t, vext, preferred_element_type=F32)


def _sb_block(layer, i, bsb_ref, q_ref, kv_ref, o_ref, acc_ref, run_ref):
    tq = q_ref.shape[0]
    tk = SB_K_BLOCK
    r = tq // tk
    r_i = lax.broadcasted_iota(jnp.int32, (tk, tk), 0)
    c_i = lax.broadcasted_iota(jnp.int32, (tk, tk), 1)
    after = (r_i > c_i).astype(BF16)
    visible = c_i < r_i
    acc_ref[...] = jnp.zeros_like(acc_ref)
    run_ref[...] = jnp.zeros_like(run_ref)

    def tile(kb, rows, diagonal):
        k0 = pl.multiple_of(kb * tk, tk)
        for h in range(HEADS):
            sl = slice(h * HEAD_DIM, (h + 1) * HEAD_DIM)
            k = kv_ref[pl.ds(k0, tk), sl]
            v = kv_ref[pl.ds(k0, tk), WIDTH + h * HEAD_DIM:WIDTH + (h + 1) * HEAD_DIM]
            z = (lax.dot_general(q_ref[rows, sl], k, (((1,), (1,)), ((), ())), preferred_element_type=F32)
                 + bsb_ref[layer, h] * LOG2E)
            drop, log_beta = _stick_terms(z)
            if diagonal:
                drop = jnp.where(visible, drop, 0.0)
            run = run_ref[h, rows]
            later = run + jnp.dot(drop.astype(BF16), after, preferred_element_type=F32)
            a = jnp.exp2(log_beta - later)
            if diagonal:
                a = jnp.where(visible, a, 0.0)
            acc_ref[rows, sl] += jnp.dot(a.astype(BF16), v, preferred_element_type=F32)
            run_ref[h, rows] = run + jnp.sum(drop, axis=1, keepdims=True)

    for d in reversed(range(r)):
        tile(i * r + d, slice(d * tk, (d + 1) * tk), True)
        if d + 1 < r:
            tile(i * r + d, slice((d + 1) * tk, tq), False)

    def body(j, carry):
        for u in range(r):
            tile(i * r - 1 - r * j - u, slice(0, tq), False)
        return carry

    lax.fori_loop(0, i, body, 0)
    o_ref[...] = acc_ref[...].astype(BF16)


def _mixers_kernel(layer, bsb_ref, pm_ref, pc_ref, gi_ref, gf_ref, bi_ref, bf_ref, cw_ref, q_ref, kv_ref,
                   ym_ref, yc_ref, s_ref, m_ref, cs_ref, o_ref, acc_ref, run_ref):
    i = pl.program_id(1)

    @pl.when(i == 0)
    def _():
        s_ref[...] = jnp.zeros_like(s_ref)
        m_ref[...] = jnp.zeros_like(m_ref)
        cs_ref[...] = jnp.zeros_like(cs_ref)

    rows = pm_ref.shape[0]
    for c in range(rows // MLSTM_CHUNK):
        _mlstm_conv_chunk(slice(c * MLSTM_CHUNK, (c + 1) * MLSTM_CHUNK), pm_ref, pc_ref, gi_ref, gf_ref, bi_ref,
                          bf_ref, cw_ref, ym_ref, yc_ref, s_ref, m_ref, cs_ref)
    _sb_block(layer, i, bsb_ref, q_ref, kv_ref, o_ref, acc_ref, run_ref)


def _mixers(layer, pm, pc, gi, gf, bi, bf, cw, sq, kvb, bsb, batch, seq):
    blk = SB_BLOCK
    nq = seq // blk
    m = batch * seq
    row = lambda width: pl.BlockSpec((blk, width), lambda b, i: (b * nq + i, 0))
    return pl.pallas_call(
        functools.partial(_mixers_kernel, layer),
        grid=(batch, nq),
        in_specs=[pl.BlockSpec(memory_space=pltpu.SMEM),
                  row(4 * WIDTH), row(3 * WIDTH), row(LANES), row(LANES),
                  _resident((1, LANES), layer), _resident((1, LANES), layer), _resident((3, WIDTH), layer),
                  row(WIDTH), pl.BlockSpec((seq, 2 * WIDTH), lambda b, i: (b, 0))],
        out_specs=[row(WIDTH), row(WIDTH),
                   pl.BlockSpec((None, HEADS, HEAD_DIM, 2 * HEAD_DIM), lambda b, i: (b, 0, 0, 0)),
                   pl.BlockSpec((None, 1, LANES), lambda b, i: (b, 0, 0)),
                   pl.BlockSpec((None, 2, WIDTH), lambda b, i: (b, 0, 0)),
                   row(WIDTH)],
        out_shape=[jax.ShapeDtypeStruct((m, WIDTH), BF16), jax.ShapeDtypeStruct((m, WIDTH), BF16),
                   jax.ShapeDtypeStruct((batch, HEADS, HEAD_DIM, 2 * HEAD_DIM), F32),
                   jax.ShapeDtypeStruct((batch, 1, LANES), F32),
                   jax.ShapeDtypeStruct((batch, 2, WIDTH), F32),
                   jax.ShapeDtypeStruct((m, WIDTH), BF16)],
        scratch_shapes=[pltpu.VMEM((blk, WIDTH), F32), pltpu.VMEM((HEADS, blk, 1), F32)],
        compiler_params=pltpu.CompilerParams(dimension_semantics=("parallel", "arbitrary"),
                                             vmem_limit_bytes=VMEM_LIMIT),
        name="mixers",
    )(bsb, pm, pc, gi, gf, bi, bf, cw, sq, kvb)


def _decode_seq(layer, bsb_ref, pm_ref, pc_ref, sq_ref, gi_ref, gf_ref, c0_ref, n0_ref, m0_ref, cv0_ref, bi_ref,
                bf_ref, cw_ref, k_pages, v_pages, ym_ref, yc_ref, ys_ref, c_slots, n_ref, m_ref, cv_ref):
    n_pages = len(k_pages)
    cw = cw_ref[...]
    pw = k_pages[0].shape[0]

    cb = pc_ref[0, :, 0:WIDTH].astype(F32)
    xc = pc_ref[0, :, WIDTH:2 * WIDTH].astype(F32) * pc_ref[0, :, 2 * WIDTH:3 * WIDTH].astype(F32)
    prev = cv0_ref[0]
    yc_ref[0] = (cb * (cw[0:1, :] * prev[0:1, :] + cw[1:2, :] * prev[1:2, :] + cw[2:3, :] * xc)).astype(BF16)
    cv_ref[0, 0:1, :] = prev[1:2, :]
    cv_ref[0, 1:2, :] = xc

    ig = gi_ref[0] + bi_ref[...]
    logf = -_softplus(-(gf_ref[0] + bf_ref[...]))
    m0 = m0_ref[0]
    m_t = jnp.maximum(logf + m0, ig)
    w_in = jnp.exp(ig - m_t)
    s_in = jnp.exp(logf + m0 - m_t)
    floor = jnp.exp(-m_t)
    m_ref[0] = m_t
    row128 = lax.broadcasted_iota(jnp.int32, (HEAD_DIM, HEAD_DIM), 0)
    for h in range(HEADS):
        sl = slice(h * HEAD_DIM, (h + 1) * HEAD_DIM)
        q = pm_ref[0, :, sl].astype(F32)
        k = pm_ref[0, :, WIDTH + h * HEAD_DIM:WIDTH + (h + 1) * HEAD_DIM].astype(F32)
        v = pm_ref[0, :, 2 * WIDTH + h * HEAD_DIM:2 * WIDTH + (h + 1) * HEAD_DIM].astype(F32)
        o = pm_ref[0, :, 3 * WIDTH + h * HEAD_DIM:3 * WIDTH + (h + 1) * HEAD_DIM].astype(F32)
        qk_rows = jnp.where(row128 == 0, q, jnp.where(row128 == 1, k, 0.0))
        cols = qk_rows.T
        q_col = cols[:, 0:1]
        k_col = cols[:, 1:2]
        c0 = c0_ref[0, h]
        n0 = n0_ref[0, h:h + 1, :]
        sc = s_in[:, h:h + 1]
        wi = w_in[:, h:h + 1]
        w = wi * jnp.sum(q * k, axis=1, keepdims=True)
        num = sc * jnp.sum(q_col * c0, axis=0, keepdims=True) + w * v
        den = sc * jnp.sum(q * n0, axis=1, keepdims=True) + w
        hc = num * (1.0 / jnp.maximum(jnp.abs(den), floor[:, h:h + 1]))
        ym_ref[0, :, sl] = (jax.nn.sigmoid(o) * hc).astype(BF16)
        c_new = sc * c0 + wi * (k_col * v)
        for slot in c_slots:
            slot[0, h] = c_new
        n_ref[0, h:h + 1, :] = sc * n0 + wi * k

    r8 = lax.broadcasted_iota(jnp.int32, (SUBLANES, HEAD_DIM), 0)
    rcol = lax.broadcasted_iota(jnp.int32, (SUBLANES, 1), 0)
    sq = sq_ref[0].astype(F32)
    q_rows = jnp.zeros((SUBLANES, HEAD_DIM), F32)
    bias = jnp.zeros((SUBLANES, 1), F32)
    for h in range(HEADS):
        q_rows = jnp.where(r8 == h, sq[:, h * HEAD_DIM:(h + 1) * HEAD_DIM], q_rows)
        bias = jnp.where(rcol == h, bsb_ref[layer, h] * LOG2E, bias)
    q_rows = q_rows.astype(BF16)
    own = (lax.broadcasted_iota(jnp.int32, (SUBLANES, pw), 1) % HEADS
           == lax.broadcasted_iota(jnp.int32, (SUBLANES, pw), 0))
    r_i = lax.broadcasted_iota(jnp.int32, (pw, pw), 0)
    c_i = lax.broadcasted_iota(jnp.int32, (pw, pw), 1)
    after = jnp.logical_and(r_i // HEADS > c_i // HEADS, r_i % HEADS == c_i % HEADS).astype(BF16)
    k_all = jnp.concatenate([kp[...].astype(BF16) for kp in k_pages], axis=0)
    z = lax.dot_general(q_rows, k_all, (((1,), (1,)), ((), ())), preferred_element_type=F32) + bias
    drop, log_beta = _stick_terms(z)
    own_all = jnp.concatenate([own] * n_pages, axis=1)
    drop = jnp.where(own_all, drop, 0.0)
    drop_rows = jnp.concatenate([drop[:, j * pw:(j + 1) * pw] for j in range(n_pages)], axis=0)
    within = jnp.dot(drop_rows.astype(BF16), after, preferred_element_type=F32)
    totals = jnp.sum(drop_rows, axis=1, keepdims=True)
    run = jnp.zeros((SUBLANES, 1), F32)
    laters = [None] * n_pages
    for j in reversed(range(n_pages)):
        rows = slice(SUBLANES * j, SUBLANES * (j + 1))
        laters[j] = within[rows, :] + run
        run = run + totals[rows, :]
    later = jnp.concatenate(laters, axis=1)
    a = jnp.where(own_all, jnp.exp2(log_beta - later), 0.0)
    v_all = jnp.concatenate([vp[...].astype(BF16) for vp in v_pages], axis=0)
    out = jnp.dot(a.astype(BF16), v_all, preferred_element_type=F32)
    for h in range(HEADS):
        ys_ref[0, :, h * HEAD_DIM:(h + 1) * HEAD_DIM] = out[h:h + 1, :].astype(BF16)


def _post_kernel(final, ym_ref, yc_ref, ys_ref, gt_ref, h_ref, p_ref, wb_ref, wo_ref, gmlp_ref, wup_ref,
                 wdn_ref, gple_ref, wpg_ref, wple_ref, gfin_ref, o_ref):
    mix = None
    for n, y_ref in enumerate((ym_ref, yc_ref, ys_ref)):
        pb = jnp.dot(y_ref[...], wb_ref[n], preferred_element_type=F32)
        gate = jax.nn.sigmoid(gt_ref[:, n * D_MODEL:(n + 1) * D_MODEL].astype(F32))
        mix = gate * pb if mix is None else mix + gate * pb
    h = h_ref[...] + jnp.dot(mix.astype(BF16), wo_ref[...], preferred_element_type=F32)

    xn = _rms(h, gmlp_ref[...]).astype(BF16)
    for c in range(FFN_DIM // FFN_CHUNK):
        u = jnp.dot(xn, wup_ref[:, c * FFN_CHUNK:(c + 1) * FFN_CHUNK], preferred_element_type=F32)
        r = jnp.square(jnp.maximum(u, 0.0)).astype(BF16)
        h = h + jnp.dot(r, wdn_ref[c * FFN_CHUNK:(c + 1) * FFN_CHUNK, :], preferred_element_type=F32)

    gate = jax.nn.sigmoid(jnp.dot(_rms(h, gple_ref[...]).astype(BF16), wpg_ref[...],
                                  preferred_element_type=F32))
    h = h + jnp.dot(p_ref[...].astype(BF16), wple_ref[...], preferred_element_type=F32) * gate
    o_ref[...] = _rms(h, gfin_ref[...]) if final else h


def _post(final, layer, ym, yc, ys, gt, h, p, wb, wo, gmlp, wup, wdn, gple, wpg, wple, gfin, tm):
    m = h.shape[0]
    row = lambda width: pl.BlockSpec((tm, width), lambda i: (i, 0))
    p_spec = pl.BlockSpec((None, tm, PLE_DIM), lambda i: (layer, i, 0))
    return pl.pallas_call(
        functools.partial(_post_kernel, final),
        grid=(m // tm,),
        in_specs=[row(WIDTH), row(WIDTH), row(WIDTH), row(3 * D_MODEL), row(D_MODEL), p_spec,
                  _resident((3, WIDTH, D_MODEL), layer), _resident((D_MODEL, D_MODEL), layer),
                  _resident((1, D_MODEL), layer), _resident((D_MODEL, FFN_DIM), layer),
                  _resident((FFN_DIM, D_MODEL), layer), _resident((1, D_MODEL), layer),
                  _resident((D_MODEL, D_MODEL), layer), _resident((PLE_DIM, D_MODEL), layer),
                  _resident((1, D_MODEL))],
        out_specs=row(D_MODEL),
        out_shape=jax.ShapeDtypeStruct((m, D_MODEL), F32),
        compiler_params=pltpu.CompilerParams(dimension_semantics=("parallel",),
                                             vmem_limit_bytes=VMEM_LIMIT),
        name="post",
    )(ym, yc, ys, gt, h, p, wb, wo, gmlp, wup, wdn, gple, wpg, wple, gfin)


N_POST_IN = 15
N_DECODE_IN = 12


def _post_decode_kernel(final, layer, n_pages, first_layer, pt_ref, bsb_ref, *refs):
    (ym_ref, yc_ref, ys_ref, gt_ref, h_ref, p_ref, wb_ref, wo_ref, gmlp_ref, wup_ref, wdn_ref, gple_ref,
     wpg_ref, wple_ref, gfin_ref) = refs[:N_POST_IN]
    dec_in = refs[N_POST_IN:N_POST_IN + N_DECODE_IN]
    ck_hbm, cv_hbm = refs[N_POST_IN + N_DECODE_IN:N_POST_IN + N_DECODE_IN + 2]
    (o_ref, ym_s_ref, yc_s_ref, ys_s_ref, c_ref, n_ref, m_ref, cv_ref,
     acc_ref, xn_ref, kbuf, vbuf, sem) = refs[-13:]
    t = pl.program_id(0)
    slot = t % 2

    def page_copies(seq, buf):
        copies = []
        for j in range(n_pages):
            pid = pt_ref[seq, j]
            copies.append(pltpu.make_async_copy(ck_hbm.at[layer, pid], kbuf.at[buf, j], sem.at[buf]))
            copies.append(pltpu.make_async_copy(cv_hbm.at[layer, pid], vbuf.at[buf, j], sem.at[buf]))
        return copies

    @pl.when(t == 0)
    def _():
        for cp in page_copies(0, 0):
            cp.start()

    @pl.when(t + 1 < pl.num_programs(0))
    def _():
        for cp in page_copies(t + 1, 1 - slot):
            cp.start()

    for cp in page_copies(t, slot):
        cp.wait()
    k_pages = [kbuf.at[slot, j] for j in range(n_pages)]
    v_pages = [vbuf.at[slot, j] for j in range(n_pages)]
    c_slots = [c_ref.at[l] for l in range(c_ref.shape[0])] if first_layer else [c_ref]
    half = FFN_DIM // 2

    def decode():
        _decode_seq(layer, bsb_ref, *dec_in, k_pages, v_pages, ym_s_ref, yc_s_ref, ys_s_ref, c_slots, n_ref, m_ref,
                    cv_ref)

    def mlp_half(xn, h, base):
        for c in range(half // FFN_CHUNK):
            lo = base + c * FFN_CHUNK
            u = jnp.dot(xn, wup_ref[:, lo:lo + FFN_CHUNK], preferred_element_type=F32)
            r = jnp.square(jnp.maximum(u, 0.0)).astype(BF16)
            h = h + jnp.dot(r, wdn_ref[lo:lo + FFN_CHUNK, :], preferred_element_type=F32)
        return h

    parity = pl.program_id(0) % 2

    @pl.when(parity == 0)
    def _():
        decode()
        mix = None
        for n, y_ref in enumerate((ym_ref, yc_ref, ys_ref)):
            pb = jnp.dot(y_ref[...], wb_ref[n], preferred_element_type=F32)
            gate = jax.nn.sigmoid(gt_ref[:, n * D_MODEL:(n + 1) * D_MODEL].astype(F32))
            mix = gate * pb if mix is None else mix + gate * pb
        h = h_ref[...] + jnp.dot(mix.astype(BF16), wo_ref[...], preferred_element_type=F32)
        xn = _rms(h, gmlp_ref[...]).astype(BF16)
        xn_ref[...] = xn
        acc_ref[...] = mlp_half(xn, h, 0)

    @pl.when(parity == 1)
    def _():
        decode()
        h = mlp_half(xn_ref[...], acc_ref[...], half)
        gate = jax.nn.sigmoid(jnp.dot(_rms(h, gple_ref[...]).astype(BF16), wpg_ref[...],
                                      preferred_element_type=F32))
        h = h + jnp.dot(p_ref[...].astype(BF16), wple_ref[...], preferred_element_type=F32) * gate
        o_ref[...] = _rms(h, gfin_ref[...]) if final else h


def _post_decode(final, layer, depth, ym, yc, ys, gt, h, p, post_w, pm, pc, sq, gi, gf, state_c, state_n, m0,
                 state_conv, bi, bf, cw, bsb, page_table, cache_k, cache_v, c_prev):
    nseq, n_pages = page_table.shape
    pw = cache_k.shape[2]
    m = h.shape[0]
    tm = 2 * m // nseq
    assert tm * nseq == 2 * m and tm % 16 == 0
    first = c_prev is None
    prow = lambda width: pl.BlockSpec((tm, width), lambda t, pt: (t // 2, 0))
    fixed = lambda shape: _resident(shape, layer)
    srow = lambda width: pl.BlockSpec((1, 1, width), lambda t, pt: (t, 0, 0))
    state4 = pl.BlockSpec((None, 1, HEADS, HEAD_DIM, HEAD_DIM), lambda t, pt: (layer, t, 0, 0, 0))
    state3 = pl.BlockSpec((None, 1, HEADS, HEAD_DIM), lambda t, pt: (layer, t, 0, 0))
    conv3 = pl.BlockSpec((None, 1, 2, WIDTH), lambda t, pt: (layer, t, 0, 0))

    post_specs = [prow(WIDTH), prow(WIDTH), prow(WIDTH), prow(3 * D_MODEL), prow(D_MODEL),
                  pl.BlockSpec((None, tm, PLE_DIM), lambda t, pt: (layer, t // 2, 0)),
                  fixed((3, WIDTH, D_MODEL)), fixed((D_MODEL, D_MODEL)), fixed((1, D_MODEL)),
                  fixed((D_MODEL, FFN_DIM)), fixed((FFN_DIM, D_MODEL)), fixed((1, D_MODEL)),
                  fixed((D_MODEL, D_MODEL)), fixed((PLE_DIM, D_MODEL)), _resident((1, D_MODEL))]
    dec_specs = [srow(4 * WIDTH), srow(3 * WIDTH), srow(WIDTH), srow(LANES), srow(LANES), state4, state3, srow(LANES),
                 conv3, fixed((1, LANES)), fixed((1, LANES)), fixed((3, WIDTH))]
    assert len(post_specs) == N_POST_IN and len(dec_specs) == N_DECODE_IN
    in_specs = ([pl.BlockSpec(memory_space=pltpu.SMEM)] + post_specs + dec_specs
                + [pl.BlockSpec(memory_space=pl.ANY)] * 2)
    args = ([page_table, bsb, ym, yc, ys, gt, h, p, *post_w]
            + [a.reshape(nseq, 1, a.shape[-1]) for a in (pm, pc, sq, gi, gf)]
            + [state_c, state_n, m0, state_conv, bi, bf, cw, cache_k, cache_v])
    if first:
        c_spec = pl.BlockSpec((depth, 1, HEADS, HEAD_DIM, HEAD_DIM), lambda t, pt: (0, t, 0, 0, 0))
        aliases = {}
    else:
        c_spec = pl.BlockSpec((None, 1, HEADS, HEAD_DIM, HEAD_DIM), lambda t, pt: (layer, t, 0, 0, 0))
        in_specs.append(pl.BlockSpec(memory_space=pl.ANY))
        args.append(c_prev)
        aliases = {len(args) - 1: 4}
    out_specs = [prow(D_MODEL), srow(WIDTH), srow(WIDTH), srow(WIDTH), c_spec,
                 pl.BlockSpec((1, HEADS, HEAD_DIM), lambda t, pt: (t, 0, 0)),
                 srow(LANES),
                 pl.BlockSpec((1, 2, WIDTH), lambda t, pt: (t, 0, 0))]
    out_shape = [jax.ShapeDtypeStruct((m, D_MODEL), F32)] + [jax.ShapeDtypeStruct((nseq, 1, WIDTH), BF16)] * 3 + [
        jax.ShapeDtypeStruct((depth, nseq, HEADS, HEAD_DIM, HEAD_DIM), F32),
        jax.ShapeDtypeStruct((nseq, HEADS, HEAD_DIM), F32),
        jax.ShapeDtypeStruct((nseq, 1, LANES), F32),
        jax.ShapeDtypeStruct((nseq, 2, WIDTH), F32)]
    return pl.pallas_call(
        functools.partial(_post_decode_kernel, final, layer, n_pages, first),
        grid_spec=pltpu.PrefetchScalarGridSpec(
            num_scalar_prefetch=1, grid=(nseq,), in_specs=in_specs, out_specs=out_specs,
            scratch_shapes=[pltpu.VMEM((tm, D_MODEL), F32), pltpu.VMEM((tm, D_MODEL), BF16),
                            pltpu.VMEM((2, n_pages, pw, HEAD_DIM), F32), pltpu.VMEM((2, n_pages, pw, HEAD_DIM), F32),
                            pltpu.SemaphoreType.DMA((2,))]),
        out_shape=out_shape,
        input_output_aliases=aliases,
        compiler_params=pltpu.CompilerParams(dimension_semantics=("arbitrary",),
                                             vmem_limit_bytes=VMEM_LIMIT),
        name="post_decode",
    )(*args)


def kernel(x_prompt, x_sample, cache_k, cache_v, state_mlstm_C, state_mlstm_n, state_mlstm_m, state_conv,
           page_table, p_prompt, p_sample, g_mix, w_in, b_if, b_sb, conv_w, w_branch, w_out, g_mlp, w_up,
           w_down, g_ple, w_ple_gate, w_ple, g_final):
    depth = w_in.shape[0]
    batch, seq, _ = x_prompt.shape
    nseq = x_sample.shape[0]
    n_phys, page = cache_k.shape[1], cache_k.shape[2]
    cache_k = cache_k.reshape(depth, n_phys, page * HEADS, HEAD_DIM)
    cache_v = cache_v.reshape(depth, n_phys, page * HEADS, HEAD_DIM)
    hp = x_prompt.reshape(batch * seq, D_MODEL)
    hs = x_sample.reshape(nseq, D_MODEL)
    pp = p_prompt.reshape(depth, batch * seq, PLE_DIM)
    ps = p_sample.reshape(depth, nseq, PLE_DIM)
    gfin = g_final.reshape(1, D_MODEL)
    tm_p = 512
    tm_s = nseq

    vec = lambda g: g.reshape(depth, 1, D_MODEL)
    w_t = jnp.swapaxes(w_in, 1, 2)
    gates0 = N_HEAD_COLS
    wh = w_t[:, :gates0].astype(BF16)
    wt = w_t[:, gates0 + 2 * HEADS:].astype(BF16)
    zpad = jnp.zeros((depth, LANES - HEADS, D_MODEL), w_in.dtype)
    wif = jnp.concatenate([w_t[:, gates0:gates0 + HEADS], zpad,
                           w_t[:, gates0 + HEADS:gates0 + 2 * HEADS], zpad], axis=1).astype(BF16)
    gmix = vec(g_mix)
    bi = jnp.pad(b_if[:, :HEADS], ((0, 0), (0, LANES - HEADS))).reshape(depth, 1, LANES)
    bf = jnp.pad(b_if[:, HEADS:], ((0, 0), (0, LANES - HEADS))).reshape(depth, 1, LANES)
    post_w = (w_branch.astype(BF16), w_out.astype(BF16), vec(g_mlp), w_up.astype(BF16), w_down.astype(BF16),
              vec(g_ple), w_ple_gate.astype(BF16), w_ple.astype(BF16), gfin)
    m0_all = jnp.pad(state_mlstm_m, ((0, 0), (0, 0), (0, LANES - HEADS)))

    outs_p, outs_s = [], []
    kv_p = kv_s = c_s = None
    for li in range(depth):
        final = li == depth - 1

        pm, pc, sq, kf, vf, kvb, gt, gi, gf = _proj(hp, gmix, wh, wt, wif, tm_p, li, depth, kv_p)
        kv_p = (kf, vf)
        ym, yc, s_p, m_p, cv_p, ys = _mixers(li, pm, pc, gi, gf, bi, bf, conv_w, sq, kvb, b_sb, batch, seq)
        outs_p.append((s_p[..., :HEAD_DIM], s_p[..., HEAD_DIM], m_p[:, 0, :HEADS], cv_p))
        pm_s, pc_s, sq_s, kf, vf, _, gt_s, gi_s, gf_s = _proj(hs, gmix, wh, wt, wif, tm_s, li, depth, kv_s)
        kv_s = (kf, vf)
        m0 = m0_all[li].reshape(nseq, 1, LANES)

        hp, ym_s, yc_s, ys_s, c_s, n_s, m_s, cv_s = _post_decode(
            final, li, depth, ym, yc, ys, gt, hp, pp, post_w, pm_s, pc_s, sq_s, gi_s, gf_s, state_mlstm_C,
            state_mlstm_n, m0, state_conv, bi, bf, conv_w, b_sb, page_table, cache_k, cache_v, c_s)
        hs = _post(final, li, ym_s.reshape(nseq, WIDTH), yc_s.reshape(nseq, WIDTH), ys_s.reshape(nseq, WIDTH),
                   gt_s, hs, ps, *post_w, tm_s)
        outs_s.append((n_s, m_s[:, 0, :HEADS], cv_s))

    stack = lambda outs, j: jnp.stack([o[j] for o in outs])
    k_p, v_p = (a.reshape(depth, batch, seq, HEADS, HEAD_DIM) for a in kv_p)
    k_s, v_s = (a.reshape(depth, nseq, 1, HEADS, HEAD_DIM) for a in kv_s)
    return ((hp.reshape(batch, seq, D_MODEL), hs.reshape(nseq, 1, D_MODEL), k_p, v_p)
            + tuple(stack(outs_p, j) for j in range(4)) + (k_s, v_s, c_s)
            + tuple(stack(outs_s, j) for j in range(3)))
```

```python
import functools

import jax
import jax.numpy as jnp
from jax import lax
from jax.experimental import pallas as pl
from jax.experimental.pallas import tpu as pltpu

F32 = jnp.float32
BF16 = jnp.bfloat16

D_MODEL = 1024
WIDTH = 512
HEADS = 4
HEAD_DIM = 128
FFN_DIM = 4 * D_MODEL
PLE_DIM = 256
RMS_EPS = 1e-6
LOG2E = 1.4426950408889634
LANES = 128
SUBLANES = 8

T_PC = 0
T_SQ = 3 * WIDTH
T_SK = 4 * WIDTH
T_SV = 5 * WIDTH
T_GT = 6 * WIDTH
N_HEAD_COLS = 4 * WIDTH
N_TAIL_COLS = 6 * WIDTH + 3 * D_MODEL
FFN_CHUNK = 1024

MLSTM_CHUNK = 256
SB_BLOCK = 512
SB_K_BLOCK = 256
VMEM_LIMIT = 56 * 1024 * 1024


def _softplus(x):
    return jnp.maximum(x, 0.0) + jnp.log(1.0 + jnp.exp(-jnp.abs(x)))


def _stick_terms(z):
    drop = jnp.maximum(z, 0.0) + jnp.log2(1.0 + jnp.exp2(-jnp.abs(z)))
    return drop, z - drop


def _rms(x, g):
    return x * lax.rsqrt(jnp.mean(x * x, axis=-1, keepdims=True) + RMS_EPS) * g


def _resident(shape, layer=None):
    nd = len(shape)
    if layer is None:
        return pl.BlockSpec(shape, lambda *_: (0,) * nd, pipeline_mode=pl.Buffered(1))
    return pl.BlockSpec((None,) + tuple(shape), lambda *_: (layer,) + (0,) * nd, pipeline_mode=pl.Buffered(1))


def _param_spec(arr, shape, layer):
    return _resident(shape, layer if arr.ndim == len(shape) + 1 else None)


def _proj_kernel(first_layer, n_cast, x_ref, g_ref, wh_ref, wt_ref, wif_ref, *refs):
    cast_in = refs[:n_cast]
    cast_out = refs[len(refs) - n_cast:]
    pm_ref, pc_ref, sq_ref, kf_ref, vf_ref, kvb_ref, gt_ref, gi_ref, gf_ref = refs[len(refs) - n_cast - 9:len(refs) - n_cast]
    for src, dst in zip(cast_in, cast_out):
        dst[...] = src[...].astype(BF16)
    xn = _rms(x_ref[...], g_ref[...]).astype(BF16)

    def mm(w_ref, c0, width):
        return lax.dot_general(xn, w_ref[c0:c0 + width, :], (((1,), (1,)), ((), ())), preferred_element_type=F32)

    scale = HEAD_DIM ** -0.5
    for j in range(4):
        part = mm(wh_ref, j * WIDTH, WIDTH)
        pm_ref[:, j * WIDTH:(j + 1) * WIDTH] = (part * scale if j == 1 else part).astype(BF16)
    for j in range(3):
        pc_ref[:, j * WIDTH:(j + 1) * WIDTH] = mm(wt_ref, T_PC + j * WIDTH, WIDTH).astype(BF16)
    sq_ref[...] = (mm(wt_ref, T_SQ, WIDTH) * (scale * LOG2E)).astype(BF16)
    k = mm(wt_ref, T_SK, WIDTH)
    v = mm(wt_ref, T_SV, WIDTH)
    tm = k.shape[0]
    slots = [kf_ref.at[l] for l in range(kf_ref.shape[0])] if first_layer else [kf_ref]
    vslots = [vf_ref.at[l] for l in range(vf_ref.shape[0])] if first_layer else [vf_ref]
    for h in range(HEADS):
        for kslot, vslot in zip(slots, vslots):
            kslot[pl.ds(h, tm, stride=HEADS), :] = k[:, h * HEAD_DIM:(h + 1) * HEAD_DIM]
            vslot[pl.ds(h, tm, stride=HEADS), :] = v[:, h * HEAD_DIM:(h + 1) * HEAD_DIM]
    kvb_ref[:, 0:WIDTH] = k.astype(BF16)
    kvb_ref[:, WIDTH:2 * WIDTH] = v.astype(BF16)
    for j in range(3 * D_MODEL // WIDTH):
        gt_ref[:, j * WIDTH:(j + 1) * WIDTH] = mm(wt_ref, T_GT + j * WIDTH, WIDTH).astype(BF16)
    gi_ref[...] = mm(wif_ref, 0, LANES)
    gf_ref[...] = mm(wif_ref, LANES, LANES)


def _proj(x2d, g, wh, wt, wif, tm, layer, depth, kv_prev, cast=()):
    m = x2d.shape[0]
    first = kv_prev is None
    row = lambda width: pl.BlockSpec((tm, width), lambda i: (i, 0))
    if first:
        kv_spec = pl.BlockSpec((depth, tm * HEADS, HEAD_DIM), lambda i: (0, i, 0))
    else:
        kv_spec = pl.BlockSpec((None, tm * HEADS, HEAD_DIM), lambda i: (layer, i, 0))
    widths = (4 * WIDTH, 3 * WIDTH, WIDTH, None, None, 2 * WIDTH, 3 * D_MODEL, LANES, LANES)
    dtypes = (BF16, BF16, BF16, F32, F32, BF16, BF16, F32, F32)
    out_specs = [row(w) if w else kv_spec for w in widths]
    out_shape = [jax.ShapeDtypeStruct((m, w) if w else (depth, m * HEADS, HEAD_DIM), d)
                 for w, d in zip(widths, dtypes)]
    in_specs = [row(D_MODEL), _resident((1, D_MODEL), layer), _resident((N_HEAD_COLS, D_MODEL), layer),
                _resident((N_TAIL_COLS, D_MODEL), layer), _resident((2 * LANES, D_MODEL), layer)]
    steps = m // tm
    for w in cast:
        rows, cols = w.shape[1] // steps, w.shape[2]
        assert rows * steps == w.shape[1] and rows % 16 == 0
        in_specs.append(pl.BlockSpec((None, rows, cols), lambda i: (layer, i, 0)))
        out_specs.append(pl.BlockSpec((rows, cols), lambda i: (i, 0)))
        out_shape.append(jax.ShapeDtypeStruct(w.shape[1:], BF16))
    args = [x2d, g, wh, wt, wif, *cast]
    aliases = {}
    if not first:
        in_specs += [pl.BlockSpec(memory_space=pl.ANY)] * 2
        aliases = {len(args): 3, len(args) + 1: 4}
        args += list(kv_prev)
    return pl.pallas_call(
        functools.partial(_proj_kernel, first, len(cast)),
        grid=(steps,),
        in_specs=in_specs,
        out_specs=out_specs,
        out_shape=out_shape,
        input_output_aliases=aliases,
        compiler_params=pltpu.CompilerParams(dimension_semantics=("parallel",),
                                             vmem_limit_bytes=VMEM_LIMIT),
        name="proj",
    )(*args)


def _scan_rows(x, op, fill):
    n = x.shape[0]
    row = lax.broadcasted_iota(jnp.int32, x.shape, 0)
    k = 1
    while k < n:
        shifted = pltpu.roll(x, k, axis=0)
        x = op(x, jnp.where(row >= k, shifted, fill(x)))
        k *= 2
    return x


def _mlstm_conv_chunk(rows, pm_ref, pc_ref, gi_ref, gf_ref, bi_ref, bf_ref, cw_ref,
                      ym_ref, yc_ref, s_ref, m_ref, cs_ref):
    L = rows.stop - rows.start

    cb = pc_ref[rows, 0:WIDTH].astype(F32)
    xc = pc_ref[rows, WIDTH:2 * WIDTH].astype(F32) * pc_ref[rows, 2 * WIDTH:3 * WIDTH].astype(F32)
    prev = cs_ref[...]
    row = lax.broadcasted_iota(jnp.int32, xc.shape, 0)
    xc1 = jnp.where(row == 0, prev[1:2, :], pltpu.roll(xc, 1, axis=0))
    xc2 = jnp.where(row == 0, prev[0:1, :], jnp.where(row == 1, prev[1:2, :], pltpu.roll(xc, 2, axis=0)))
    cw = cw_ref[...]
    yc_ref[rows, :] = (cb * (cw[0:1, :] * xc2 + cw[1:2, :] * xc1 + cw[2:3, :] * xc)).astype(BF16)
    cs_ref[...] = xc[L - 2:L, :]

    m_prev = m_ref[...]
    ig = gi_ref[rows, :] + bi_ref[...]
    logf = -_softplus(-(gf_ref[rows, :] + bf_ref[...]))
    b = _scan_rows(logf, jnp.add, jnp.zeros_like)
    a = ig - b
    mx = jnp.maximum(m_prev, _scan_rows(a, jnp.maximum, lambda x: x))
    m_t = b + mx
    a_t = a.T
    s_in = jnp.exp(m_prev - mx)
    floor = jnp.exp(-m_t)
    mx_last = mx[L - 1:L, :]
    wk = jnp.exp(a - mx_last)
    decay = jnp.exp(m_prev - mx_last)
    m_ref[...] = m_t[L - 1:L, :]

    r_i = lax.broadcasted_iota(jnp.int32, (L, L), 0)
    c_i = lax.broadcasted_iota(jnp.int32, (L, L), 1)
    causal = c_i <= r_i
    one_col = (lax.broadcasted_iota(jnp.int32, (L, HEAD_DIM), 1) == 0).astype(BF16)
    for h in range(HEADS):
        sl = slice(h * HEAD_DIM, (h + 1) * HEAD_DIM)
        q = pm_ref[rows, sl]
        k = pm_ref[rows, WIDTH + h * HEAD_DIM:WIDTH + (h + 1) * HEAD_DIM]
        v = pm_ref[rows, 2 * WIDTH + h * HEAD_DIM:2 * WIDTH + (h + 1) * HEAD_DIM]
        o = pm_ref[rows, 3 * WIDTH + h * HEAD_DIM:3 * WIDTH + (h + 1) * HEAD_DIM]
        vext = jnp.concatenate([v, one_col], axis=1)
        qk = lax.dot_general(q, k, (((1,), (1,)), ((), ())), preferred_element_type=F32)
        e = jnp.exp(a_t[h:h + 1, :] - mx[:, h:h + 1])
        w = jnp.where(causal, e, 0.0) * qk
        intra = jnp.dot(w.astype(BF16), vext, preferred_element_type=F32)
        state = s_ref[h]
        inter = jnp.dot(q, state.astype(BF16), preferred_element_type=F32)
        sc = s_in[:, h:h + 1]
        num = sc * inter[:, :HEAD_DIM] + intra[:, :HEAD_DIM]
        den = sc * inter[:, HEAD_DIM:HEAD_DIM + 1] + intra[:, HEAD_DIM:HEAD_DIM + 1]
        inv = 1.0 / jnp.maximum(jnp.abs(den), floor[:, h:h + 1])
        ym_ref[rows, sl] = (jax.nn.sigmoid(o.astype(F32)) * (num * inv)).astype(BF16)
        kw_t = (k.astype(F32) * wk[:, h:h + 1]).T.astype(BF16)
        s_ref[h] = decay[:, h:h + 1] * state + jnp.dot(kw_t, vext, preferred_element_type=F32)


def _sb_block(layer, i, bsb_ref, q_ref, kv_ref, o_ref, acc_ref, run_ref):
    tq = q_ref.shape[0]
    tk = SB_K_BLOCK
    r = tq // tk
    r_i = lax.broadcasted_iota(jnp.int32, (tk, tk), 0)
    c_i = lax.broadcasted_iota(jnp.int32, (tk, tk), 1)
    after = (r_i > c_i).astype(BF16)
    visible = c_i < r_i
    acc_ref[...] = jnp.zeros_like(acc_ref)
    run_ref[...] = jnp.zeros_like(run_ref)

    def tile(kb, rows, diagonal):
        k0 = pl.multiple_of(kb * tk, tk)
        for h in range(HEADS):
            sl = slice(h * HEAD_DIM, (h + 1) * HEAD_DIM)
            k = kv_ref[pl.ds(k0, tk), sl]
            v = kv_ref[pl.ds(k0, tk), WIDTH + h * HEAD_DIM:WIDTH + (h + 1) * HEAD_DIM]
            z = (lax.dot_general(q_ref[rows, sl], k, (((1,), (1,)), ((), ())), preferred_element_type=F32)
                 + bsb_ref[layer, h] * LOG2E)
            drop, log_beta = _stick_terms(z)
            if diagonal:
                drop = jnp.where(visible, drop, 0.0)
            run = run_ref[h, rows]
            later = run + jnp.dot(drop.astype(BF16), after, preferred_element_type=F32)
            a = jnp.exp2(log_beta - later)
            if diagonal:
                a = jnp.where(visible, a, 0.0)
            acc_ref[rows, sl] += jnp.dot(a.astype(BF16), v, preferred_element_type=F32)
            run_ref[h, rows] = run + jnp.sum(drop, axis=1, keepdims=True)

    for d in reversed(range(r)):
        tile(i * r + d, slice(d * tk, (d + 1) * tk), True)
        if d + 1 < r:
            tile(i * r + d, slice((d + 1) * tk, tq), False)

    def body(j, carry):
        for u in range(r):
            tile(i * r - 1 - r * j - u, slice(0, tq), False)
        return carry

    lax.fori_loop(0, i, body, 0)
    o_ref[...] = acc_ref[...].astype(BF16)


def _mixers_kernel(layer, bsb_ref, pm_ref, pc_ref, gi_ref, gf_ref, bi_ref, bf_ref, cw_ref, q_ref, kv_ref,
                   ym_ref, yc_ref, s_ref, m_ref, cs_ref, o_ref, acc_ref, run_ref):
    i = pl.program_id(1)

    @pl.when(i == 0)
    def _():
        s_ref[...] = jnp.zeros_like(s_ref)
        m_ref[...] = jnp.zeros_like(m_ref)
        cs_ref[...] = jnp.zeros_like(cs_ref)

    rows = pm_ref.shape[0]
    for c in range(rows // MLSTM_CHUNK):
        _mlstm_conv_chunk(slice(c * MLSTM_CHUNK, (c + 1) * MLSTM_CHUNK), pm_ref, pc_ref, gi_ref, gf_ref, bi_ref,
                          bf_ref, cw_ref, ym_ref, yc_ref, s_ref, m_ref, cs_ref)
    _sb_block(layer, i, bsb_ref, q_ref, kv_ref, o_ref, acc_ref, run_ref)


def _mixers(layer, pm, pc, gi, gf, bi, bf, cw, sq, kvb, bsb, batch, seq):
    blk = SB_BLOCK
    nq = seq // blk
    m = batch * seq
    row = lambda width: pl.BlockSpec((blk, width), lambda b, i: (b * nq + i, 0))
    return pl.pallas_call(
        functools.partial(_mixers_kernel, layer),
        grid=(batch, nq),
        in_specs=[pl.BlockSpec(memory_space=pltpu.SMEM),
                  row(4 * WIDTH), row(3 * WIDTH), row(LANES), row(LANES),
                  _resident((1, LANES), layer), _resident((1, LANES), layer), _resident((3, WIDTH), layer),
                  row(WIDTH), pl.BlockSpec((seq, 2 * WIDTH), lambda b, i: (b, 0))],
        out_specs=[row(WIDTH), row(WIDTH),
                   pl.BlockSpec((None, HEADS, HEAD_DIM, 2 * HEAD_DIM), lambda b, i: (b, 0, 0, 0)),
                   pl.BlockSpec((None, 1, LANES), lambda b, i: (b, 0, 0)),
                   pl.BlockSpec((None, 2, WIDTH), lambda b, i: (b, 0, 0)),
                   row(WIDTH)],
        out_shape=[jax.ShapeDtypeStruct((m, WIDTH), BF16), jax.ShapeDtypeStruct((m, WIDTH), BF16),
                   jax.ShapeDtypeStruct((batch, HEADS, HEAD_DIM, 2 * HEAD_DIM), F32),
                   jax.ShapeDtypeStruct((batch, 1, LANES), F32),
                   jax.ShapeDtypeStruct((batch, 2, WIDTH), F32),
                   jax.ShapeDtypeStruct((m, WIDTH), BF16)],
        scratch_shapes=[pltpu.VMEM((blk, WIDTH), F32), pltpu.VMEM((HEADS, blk, 1), F32)],
        compiler_params=pltpu.CompilerParams(dimension_semantics=("parallel", "arbitrary"),
                                             vmem_limit_bytes=VMEM_LIMIT),
        name="mixers",
    )(bsb, pm, pc, gi, gf, bi, bf, cw, sq, kvb)


def _decode_seq(layer, bsb_ref, pm_ref, pc_ref, sq_ref, gi_ref, gf_ref, c0_ref, n0_ref, m0_ref, cv0_ref, bi_ref,
                bf_ref, cw_ref, k_pages, v_pages, ym_ref, yc_ref, ys_ref, c_slots, n_ref, m_ref, cv_ref):
    n_pages = len(k_pages)
    cw = cw_ref[...]
    pw = k_pages[0].shape[0]

    cb = pc_ref[0, :, 0:WIDTH].astype(F32)
    xc = pc_ref[0, :, WIDTH:2 * WIDTH].astype(F32) * pc_ref[0, :, 2 * WIDTH:3 * WIDTH].astype(F32)
    prev = cv0_ref[0]
    yc_ref[0] = (cb * (cw[0:1, :] * prev[0:1, :] + cw[1:2, :] * prev[1:2, :] + cw[2:3, :] * xc)).astype(BF16)
    cv_ref[0, 0:1, :] = prev[1:2, :]
    cv_ref[0, 1:2, :] = xc

    ig = gi_ref[0] + bi_ref[...]
    logf = -_softplus(-(gf_ref[0] + bf_ref[...]))
    m0 = m0_ref[0]
    m_t = jnp.maximum(logf + m0, ig)
    w_in = jnp.exp(ig - m_t)
    s_in = jnp.exp(logf + m0 - m_t)
    floor = jnp.exp(-m_t)
    m_ref[0] = m_t
    row128 = lax.broadcasted_iota(jnp.int32, (HEAD_DIM, HEAD_DIM), 0)
    for h in range(HEADS):
        sl = slice(h * HEAD_DIM, (h + 1) * HEAD_DIM)
        q = pm_ref[0, :, sl].astype(F32)
        k = pm_ref[0, :, WIDTH + h * HEAD_DIM:WIDTH + (h + 1) * HEAD_DIM].astype(F32)
        v = pm_ref[0, :, 2 * WIDTH + h * HEAD_DIM:2 * WIDTH + (h + 1) * HEAD_DIM].astype(F32)
        o = pm_ref[0, :, 3 * WIDTH + h * HEAD_DIM:3 * WIDTH + (h + 1) * HEAD_DIM].astype(F32)
        qk_rows = jnp.where(row128 == 0, q, jnp.where(row128 == 1, k, 0.0))
        cols = qk_rows.T
        q_col = cols[:, 0:1]
        k_col = cols[:, 1:2]
        c0 = c0_ref[0, h]
        n0 = n0_ref[0, h:h + 1, :]
        sc = s_in[:, h:h + 1]
        wi = w_in[:, h:h + 1]
        w = wi * jnp.sum(q * k, axis=1, keepdims=True)
        num = sc * jnp.sum(q_col * c0, axis=0, keepdims=True) + w * v
        den = sc * jnp.sum(q * n0, axis=1, keepdims=True) + w
        hc = num * (1.0 / jnp.maximum(jnp.abs(den), floor[:, h:h + 1]))
        ym_ref[0, :, sl] = (jax.nn.sigmoid(o) * hc).astype(BF16)
        c_new = sc * c0 + wi * (k_col * v)
        for slot in c_slots:
            slot[0, h] = c_new
        n_ref[0, h:h + 1, :] = sc * n0 + wi * k

    r8 = lax.broadcasted_iota(jnp.int32, (SUBLANES, HEAD_DIM), 0)
    rcol = lax.broadcasted_iota(jnp.int32, (SUBLANES, 1), 0)
    sq = sq_ref[0].astype(F32)
    q_rows = jnp.zeros((SUBLANES, HEAD_DIM), F32)
    bias = jnp.zeros((SUBLANES, 1), F32)
    for h in range(HEADS):
        q_rows = jnp.where(r8 == h, sq[:, h * HEAD_DIM:(h + 1) * HEAD_DIM], q_rows)
        bias = jnp.where(rcol == h, bsb_ref[layer, h] * LOG2E, bias)
    q_rows = q_rows.astype(BF16)
    own = (lax.broadcasted_iota(jnp.int32, (SUBLANES, pw), 1) % HEADS
           == lax.broadcasted_iota(jnp.int32, (SUBLANES, pw), 0))
    r_i = lax.broadcasted_iota(jnp.int32, (pw, pw), 0)
    c_i = lax.broadcasted_iota(jnp.int32, (pw, pw), 1)
    after = jnp.logical_and(r_i // HEADS > c_i // HEADS, r_i % HEADS == c_i % HEADS).astype(BF16)
    k_all = jnp.concatenate([kp[...].astype(BF16) for kp in k_pages], axis=0)
    z = lax.dot_general(q_rows, k_all, (((1,), (1,)), ((), ())), preferred_element_type=F32) + bias
    drop, log_beta = _stick_terms(z)
    own_all = jnp.concatenate([own] * n_pages, axis=1)
    drop = jnp.where(own_all, drop, 0.0)
    drop_rows = jnp.concatenate([drop[:, j * pw:(j + 1) * pw] for j in range(n_pages)], axis=0)
    within = jnp.dot(drop_rows.astype(BF16), after, preferred_element_type=F32)
    totals = jnp.sum(drop_rows, axis=1, keepdims=True)
    run = jnp.zeros((SUBLANES, 1), F32)
    laters = [None] * n_pages
    for j in reversed(range(n_pages)):
        rows = slice(SUBLANES * j, SUBLANES * (j + 1))
        laters[j] = within[rows, :] + run
        run = run + totals[rows, :]
    later = jnp.concatenate(laters, axis=1)
    a = jnp.where(own_all, jnp.exp2(log_beta - later), 0.0)
    v_all = jnp.concatenate([vp[...].astype(BF16) for vp in v_pages], axis=0)
    out = jnp.dot(a.astype(BF16), v_all, preferred_element_type=F32)
    for h in range(HEADS):
        ys_ref[0, :, h * HEAD_DIM:(h + 1) * HEAD_DIM] = out[h:h + 1, :].astype(BF16)


_POST_PARAM_SHAPES = ((3, WIDTH, D_MODEL), (D_MODEL, D_MODEL), (1, D_MODEL), (D_MODEL, FFN_DIM), (FFN_DIM, D_MODEL),
                      (1, D_MODEL), (D_MODEL, D_MODEL), (PLE_DIM, D_MODEL), (1, D_MODEL))


def _post_kernel(final, ym_ref, yc_ref, ys_ref, gt_ref, h_ref, p_ref, wb_ref, wo_ref, gmlp_ref, wup_ref,
                 wdn_ref, gple_ref, wpg_ref, wple_ref, gfin_ref, o_ref):
    mix = None
    for n, y_ref in enumerate((ym_ref, yc_ref, ys_ref)):
        pb = jnp.dot(y_ref[...], wb_ref[n], preferred_element_type=F32)
        gate = jax.nn.sigmoid(gt_ref[:, n * D_MODEL:(n + 1) * D_MODEL].astype(F32))
        mix = gate * pb if mix is None else mix + gate * pb
    h = h_ref[...] + jnp.dot(mix.astype(BF16), wo_ref[...], preferred_element_type=F32)

    xn = _rms(h, gmlp_ref[...]).astype(BF16)
    for c in range(FFN_DIM // FFN_CHUNK):
        u = jnp.dot(xn, wup_ref[:, c * FFN_CHUNK:(c + 1) * FFN_CHUNK], preferred_element_type=F32)
        r = jnp.square(jnp.maximum(u, 0.0)).astype(BF16)
        h = h + jnp.dot(r, wdn_ref[c * FFN_CHUNK:(c + 1) * FFN_CHUNK, :], preferred_element_type=F32)

    gate = jax.nn.sigmoid(jnp.dot(_rms(h, gple_ref[...]).astype(BF16), wpg_ref[...],
                                  preferred_element_type=F32))
    h = h + jnp.dot(p_ref[...].astype(BF16), wple_ref[...], preferred_element_type=F32) * gate
    o_ref[...] = _rms(h, gfin_ref[...]) if final else h


def _post(final, layer, ym, yc, ys, gt, h, p, wb, wo, gmlp, wup, wdn, gple, wpg, wple, gfin, tm):
    m = h.shape[0]
    row = lambda width: pl.BlockSpec((tm, width), lambda i: (i, 0))
    p_spec = pl.BlockSpec((None, tm, PLE_DIM), lambda i: (layer, i, 0))
    return pl.pallas_call(
        functools.partial(_post_kernel, final),
        grid=(m // tm,),
        in_specs=[row(WIDTH), row(WIDTH), row(WIDTH), row(3 * D_MODEL), row(D_MODEL), p_spec,
                  *[_param_spec(w, shape, layer) for w, shape in zip(
                      (wb, wo, gmlp, wup, wdn, gple, wpg, wple, gfin), _POST_PARAM_SHAPES)]],
        out_specs=row(D_MODEL),
        out_shape=jax.ShapeDtypeStruct((m, D_MODEL), F32),
        compiler_params=pltpu.CompilerParams(dimension_semantics=("parallel",),
                                             vmem_limit_bytes=VMEM_LIMIT),
        name="post",
    )(ym, yc, ys, gt, h, p, wb, wo, gmlp, wup, wdn, gple, wpg, wple, gfin)


N_POST_IN = 15
N_DECODE_IN = 12


def _post_decode_kernel(final, layer, n_pages, first_layer, pt_ref, bsb_ref, *refs):
    (ym_ref, yc_ref, ys_ref, gt_ref, h_ref, p_ref, wb_ref, wo_ref, gmlp_ref, wup_ref, wdn_ref, gple_ref,
     wpg_ref, wple_ref, gfin_ref) = refs[:N_POST_IN]
    dec_in = refs[N_POST_IN:N_POST_IN + N_DECODE_IN]
    ck_hbm, cv_hbm = refs[N_POST_IN + N_DECODE_IN:N_POST_IN + N_DECODE_IN + 2]
    (o_ref, ym_s_ref, yc_s_ref, ys_s_ref, c_ref, n_ref, m_ref, cv_ref,
     acc_ref, xn_ref, kbuf, vbuf, sem) = refs[-13:]
    t = pl.program_id(0)
    slot = t % 2

    def page_copies(seq, buf):
        copies = []
        for j in range(n_pages):
            pid = pt_ref[seq, j]
            copies.append(pltpu.make_async_copy(ck_hbm.at[layer, pid], kbuf.at[buf, j], sem.at[buf]))
            copies.append(pltpu.make_async_copy(cv_hbm.at[layer, pid], vbuf.at[buf, j], sem.at[buf]))
        return copies

    @pl.when(t == 0)
    def _():
        for cp in page_copies(0, 0):
            cp.start()

    @pl.when(t + 1 < pl.num_programs(0))
    def _():
        for cp in page_copies(t + 1, 1 - slot):
            cp.start()

    for cp in page_copies(t, slot):
        cp.wait()
    k_pages = [kbuf.at[slot, j] for j in range(n_pages)]
    v_pages = [vbuf.at[slot, j] for j in range(n_pages)]
    c_slots = [c_ref.at[l] for l in range(c_ref.shape[0])] if first_layer else [c_ref]
    half = FFN_DIM // 2

    def decode():
        _decode_seq(layer, bsb_ref, *dec_in, k_pages, v_pages, ym_s_ref, yc_s_ref, ys_s_ref, c_slots, n_ref, m_ref,
                    cv_ref)

    def mlp_half(xn, h, base):
        for c in range(half // FFN_CHUNK):
            lo = base + c * FFN_CHUNK
            u = jnp.dot(xn, wup_ref[:, lo:lo + FFN_CHUNK], preferred_element_type=F32)
            r = jnp.square(jnp.maximum(u, 0.0)).astype(BF16)
            h = h + jnp.dot(r, wdn_ref[lo:lo + FFN_CHUNK, :], preferred_element_type=F32)
        return h

    parity = pl.program_id(0) % 2

    @pl.when(parity == 0)
    def _():
        decode()
        mix = None
        for n, y_ref in enumerate((ym_ref, yc_ref, ys_ref)):
            pb = jnp.dot(y_ref[...], wb_ref[n], preferred_element_type=F32)
            gate = jax.nn.sigmoid(gt_ref[:, n * D_MODEL:(n + 1) * D_MODEL].astype(F32))
            mix = gate * pb if mix is None else mix + gate * pb
        h = h_ref[...] + jnp.dot(mix.astype(BF16), wo_ref[...], preferred_element_type=F32)
        xn = _rms(h, gmlp_ref[...]).astype(BF16)
        xn_ref[...] = xn
        acc_ref[...] = mlp_half(xn, h, 0)

    @pl.when(parity == 1)
    def _():
        decode()
        h = mlp_half(xn_ref[...], acc_ref[...], half)
        gate = jax.nn.sigmoid(jnp.dot(_rms(h, gple_ref[...]).astype(BF16), wpg_ref[...],
                                      preferred_element_type=F32))
        h = h + jnp.dot(p_ref[...].astype(BF16), wple_ref[...], preferred_element_type=F32) * gate
        o_ref[...] = _rms(h, gfin_ref[...]) if final else h


def _post_decode(final, layer, depth, ym, yc, ys, gt, h, p, post_w, pm, pc, sq, gi, gf, state_c, state_n, m0,
                 state_conv, bi, bf, cw, bsb, page_table, cache_k, cache_v, c_prev):
    nseq, n_pages = page_table.shape
    pw = cache_k.shape[2]
    m = h.shape[0]
    tm = 2 * m // nseq
    assert tm * nseq == 2 * m and tm % 16 == 0
    first = c_prev is None
    prow = lambda width: pl.BlockSpec((tm, width), lambda t, pt: (t // 2, 0))
    fixed = lambda shape: _resident(shape, layer)
    srow = lambda width: pl.BlockSpec((1, 1, width), lambda t, pt: (t, 0, 0))
    state4 = pl.BlockSpec((None, 1, HEADS, HEAD_DIM, HEAD_DIM), lambda t, pt: (layer, t, 0, 0, 0))
    state3 = pl.BlockSpec((None, 1, HEADS, HEAD_DIM), lambda t, pt: (layer, t, 0, 0))
    conv3 = pl.BlockSpec((None, 1, 2, WIDTH), lambda t, pt: (layer, t, 0, 0))

    post_specs = [prow(WIDTH), prow(WIDTH), prow(WIDTH), prow(3 * D_MODEL), prow(D_MODEL),
                  pl.BlockSpec((None, tm, PLE_DIM), lambda t, pt: (layer, t // 2, 0)),
                  *[_param_spec(w, shape, layer) for w, shape in zip(post_w, _POST_PARAM_SHAPES)]]
    dec_specs = [srow(4 * WIDTH), srow(3 * WIDTH), srow(WIDTH), srow(LANES), srow(LANES), state4, state3, srow(LANES),
                 conv3, fixed((1, LANES)), fixed((1, LANES)), fixed((3, WIDTH))]
    assert len(post_specs) == N_POST_IN and len(dec_specs) == N_DECODE_IN
    in_specs = ([pl.BlockSpec(memory_space=pltpu.SMEM)] + post_specs + dec_specs
                + [pl.BlockSpec(memory_space=pl.ANY)] * 2)
    args = ([page_table, bsb, ym, yc, ys, gt, h, p, *post_w]
            + [a.reshape(nseq, 1, a.shape[-1]) for a in (pm, pc, sq, gi, gf)]
            + [state_c, state_n, m0, state_conv, bi, bf, cw, cache_k, cache_v])
    if first:
        c_spec = pl.BlockSpec((depth, 1, HEADS, HEAD_DIM, HEAD_DIM), lambda t, pt: (0, t, 0, 0, 0))
        aliases = {}
    else:
        c_spec = pl.BlockSpec((None, 1, HEADS, HEAD_DIM, HEAD_DIM), lambda t, pt: (layer, t, 0, 0, 0))
        in_specs.append(pl.BlockSpec(memory_space=pl.ANY))
        args.append(c_prev)
        aliases = {len(args) - 1: 4}
    out_specs = [prow(D_MODEL), srow(WIDTH), srow(WIDTH), srow(WIDTH), c_spec,
                 pl.BlockSpec((1, HEADS, HEAD_DIM), lambda t, pt: (t, 0, 0)),
                 srow(LANES),
                 pl.BlockSpec((1, 2, WIDTH), lambda t, pt: (t, 0, 0))]
    out_shape = [jax.ShapeDtypeStruct((m, D_MODEL), F32)] + [jax.ShapeDtypeStruct((nseq, 1, WIDTH), BF16)] * 3 + [
        jax.ShapeDtypeStruct((depth, nseq, HEADS, HEAD_DIM, HEAD_DIM), F32),
        jax.ShapeDtypeStruct((nseq, HEADS, HEAD_DIM), F32),
        jax.ShapeDtypeStruct((nseq, 1, LANES), F32),
        jax.ShapeDtypeStruct((nseq, 2, WIDTH), F32)]
    return pl.pallas_call(
        functools.partial(_post_decode_kernel, final, layer, n_pages, first),
        grid_spec=pltpu.PrefetchScalarGridSpec(
            num_scalar_prefetch=1, grid=(nseq,), in_specs=in_specs, out_specs=out_specs,
            scratch_shapes=[pltpu.VMEM((tm, D_MODEL), F32), pltpu.VMEM((tm, D_MODEL), BF16),
                            pltpu.VMEM((2, n_pages, pw, HEAD_DIM), F32), pltpu.VMEM((2, n_pages, pw, HEAD_DIM), F32),
                            pltpu.SemaphoreType.DMA((2,))]),
        out_shape=out_shape,
        input_output_aliases=aliases,
        compiler_params=pltpu.CompilerParams(dimension_semantics=("arbitrary",),
                                             vmem_limit_bytes=VMEM_LIMIT),
        name="post_decode",
    )(*args)


def kernel(x_prompt, x_sample, cache_k, cache_v, state_mlstm_C, state_mlstm_n, state_mlstm_m, state_conv,
           page_table, p_prompt, p_sample, g_mix, w_in, b_if, b_sb, conv_w, w_branch, w_out, g_mlp, w_up,
           w_down, g_ple, w_ple_gate, w_ple, g_final):
    depth = w_in.shape[0]
    batch, seq, _ = x_prompt.shape
    nseq = x_sample.shape[0]
    n_phys, page = cache_k.shape[1], cache_k.shape[2]
    cache_k = cache_k.reshape(depth, n_phys, page * HEADS, HEAD_DIM)
    cache_v = cache_v.reshape(depth, n_phys, page * HEADS, HEAD_DIM)
    hp = x_prompt.reshape(batch * seq, D_MODEL)
    hs = x_sample.reshape(nseq, D_MODEL)
    pp = p_prompt.reshape(depth, batch * seq, PLE_DIM)
    ps = p_sample.reshape(depth, nseq, PLE_DIM)
    gfin = g_final.reshape(1, D_MODEL)
    tm_p = 512
    tm_s = nseq

    vec = lambda g: g.reshape(depth, 1, D_MODEL)
    w_t = jnp.swapaxes(w_in, 1, 2)
    gates0 = N_HEAD_COLS
    wh = w_t[:, :gates0].astype(BF16)
    wt = w_t[:, gates0 + 2 * HEADS:].astype(BF16)
    zpad = jnp.zeros((depth, LANES - HEADS, D_MODEL), w_in.dtype)
    wif = jnp.concatenate([w_t[:, gates0:gates0 + HEADS], zpad,
                           w_t[:, gates0 + HEADS:gates0 + 2 * HEADS], zpad], axis=1).astype(BF16)
    gmix = vec(g_mix)
    bi = jnp.pad(b_if[:, :HEADS], ((0, 0), (0, LANES - HEADS))).reshape(depth, 1, LANES)
    bf = jnp.pad(b_if[:, HEADS:], ((0, 0), (0, LANES - HEADS))).reshape(depth, 1, LANES)
    post_f32 = (w_branch.reshape(depth, 3 * WIDTH, D_MODEL), w_out, w_up, w_down, w_ple_gate)
    wple = w_ple.astype(BF16)
    m0_all = jnp.pad(state_mlstm_m, ((0, 0), (0, 0), (0, LANES - HEADS)))

    outs_p, outs_s = [], []
    kv_p = kv_s = c_s = None
    for li in range(depth):
        final = li == depth - 1

        pm, pc, sq, kf, vf, kvb, gt, gi, gf, wb, wo, wup, wdn, wpg = _proj(
            hp, gmix, wh, wt, wif, tm_p, li, depth, kv_p, post_f32)
        kv_p = (kf, vf)
        post_w = (wb.reshape(3, WIDTH, D_MODEL), wo, vec(g_mlp), wup, wdn, vec(g_ple), wpg, wple, gfin)
        ym, yc, s_p, m_p, cv_p, ys = _mixers(li, pm, pc, gi, gf, bi, bf, conv_w, sq, kvb, b_sb, batch, seq)
        outs_p.append((s_p[..., :HEAD_DIM], s_p[..., HEAD_DIM], m_p[:, 0, :HEADS], cv_p))
        pm_s, pc_s, sq_s, kf, vf, _, gt_s, gi_s, gf_s = _proj(hs, gmix, wh, wt, wif, tm_s, li, depth, kv_s)
        kv_s = (kf, vf)
        m0 = m0_all[li].reshape(nseq, 1, LANES)

        hp, ym_s, yc_s, ys_s, c_s, n_s, m_s, cv_s = _post_decode(
            final, li, depth, ym, yc, ys, gt, hp, pp, post_w, pm_s, pc_s, sq_s, gi_s, gf_s, state_mlstm_C,
            state_mlstm_n, m0, state_conv, bi, bf, conv_w, b_sb, page_table, cache_k, cache_v, c_s)
        hs = _post(final, li, ym_s.reshape(nseq, WIDTH), yc_s.reshape(nseq, WIDTH), ys_s.reshape(nseq, WIDTH),
                   gt_s, hs, ps, *post_w, tm_s)
        outs_s.append((n_s, m_s[:, 0, :HEADS], cv_s))

    stack = lambda outs, j: jnp.stack([o[j] for o in outs])
    k_p, v_p = (a.reshape(depth, batch, seq, HEADS, HEAD_DIM) for a in kv_p)
    k_s, v_s = (a.reshape(depth, nseq, 1, HEADS, HEAD_DIM) for a in kv_s)
    return ((hp.reshape(batch, seq, D_MODEL), hs.reshape(nseq, 1, D_MODEL), k_p, v_p)
            + tuple(stack(outs_p, j) for j in range(4)) + (k_s, v_s, c_s)
            + tuple(stack(outs_s, j) for j in range(3)))
```

```python
import functools

import jax
import jax.numpy as jnp
from jax import lax
from jax.experimental import pallas as pl
from jax.experimental.pallas import tpu as pltpu

F32 = jnp.float32
BF16 = jnp.bfloat16

D_MODEL = 1024
WIDTH = 512
HEADS = 4
HEAD_DIM = 128
FFN_DIM = 4 * D_MODEL
PLE_DIM = 256
RMS_EPS = 1e-6
LOG2E = 1.4426950408889634
LANES = 128
SUBLANES = 8

T_PC = 0
T_SQ = 3 * WIDTH
T_SK = 4 * WIDTH
T_SV = 5 * WIDTH
T_GT = 6 * WIDTH
N_HEAD_COLS = 4 * WIDTH
N_TAIL_COLS = 6 * WIDTH + 3 * D_MODEL
FFN_CHUNK = 1024

MLSTM_CHUNK = 256
SB_BLOCK = 512
SB_K_BLOCK = 256
VMEM_LIMIT = 56 * 1024 * 1024


def _softplus(x):
    return jnp.maximum(x, 0.0) + jnp.log(1.0 + jnp.exp(-jnp.abs(x)))


def _stick_terms(z):
    drop = jnp.maximum(z, 0.0) + jnp.log2(1.0 + jnp.exp2(-jnp.abs(z)))
    return drop, z - drop


def _rms(x, g):
    return x * lax.rsqrt(jnp.mean(x * x, axis=-1, keepdims=True) + RMS_EPS) * g


def _resident(shape, layer=None):
    nd = len(shape)
    if layer is None:
        return pl.BlockSpec(shape, lambda *_: (0,) * nd, pipeline_mode=pl.Buffered(1))
    return pl.BlockSpec((None,) + tuple(shape), lambda *_: (layer,) + (0,) * nd, pipeline_mode=pl.Buffered(1))


def _param_spec(arr, shape, layer):
    return _resident(shape, layer if arr.ndim == len(shape) + 1 else None)


def _proj_kernel(first_layer, n_cast, x_ref, g_ref, wh_ref, wt_ref, wif_ref, *refs):
    cast_in = refs[:n_cast]
    cast_out = refs[len(refs) - n_cast:]
    pm_ref, pc_ref, sq_ref, kf_ref, vf_ref, kvb_ref, gt_ref, gi_ref, gf_ref = refs[len(refs) - n_cast - 9:len(refs) - n_cast]
    for src, dst in zip(cast_in, cast_out):
        dst[...] = src[...].astype(BF16)
    xn = _rms(x_ref[...], g_ref[...]).astype(BF16)

    def mm(w_ref, c0, width):
        return lax.dot_general(xn, w_ref[c0:c0 + width, :], (((1,), (1,)), ((), ())), preferred_element_type=F32)

    scale = HEAD_DIM ** -0.5
    for j in range(4):
        part = mm(wh_ref, j * WIDTH, WIDTH)
        pm_ref[:, j * WIDTH:(j + 1) * WIDTH] = (part * scale if j == 1 else part).astype(pm_ref.dtype)
    for j in range(3):
        pc_ref[:, j * WIDTH:(j + 1) * WIDTH] = mm(wt_ref, T_PC + j * WIDTH, WIDTH).astype(pc_ref.dtype)
    sq_ref[...] = (mm(wt_ref, T_SQ, WIDTH) * (scale * LOG2E)).astype(sq_ref.dtype)
    k = mm(wt_ref, T_SK, WIDTH)
    v = mm(wt_ref, T_SV, WIDTH)
    tm = k.shape[0]
    slots = [kf_ref.at[l] for l in range(kf_ref.shape[0])] if first_layer else [kf_ref]
    vslots = [vf_ref.at[l] for l in range(vf_ref.shape[0])] if first_layer else [vf_ref]
    for h in range(HEADS):
        for kslot, vslot in zip(slots, vslots):
            kslot[pl.ds(h, tm, stride=HEADS), :] = k[:, h * HEAD_DIM:(h + 1) * HEAD_DIM]
            vslot[pl.ds(h, tm, stride=HEADS), :] = v[:, h * HEAD_DIM:(h + 1) * HEAD_DIM]
    kvb_ref[:, 0:WIDTH] = k.astype(BF16)
    kvb_ref[:, WIDTH:2 * WIDTH] = v.astype(BF16)
    for j in range(3 * D_MODEL // WIDTH):
        gt_ref[:, j * WIDTH:(j + 1) * WIDTH] = mm(wt_ref, T_GT + j * WIDTH, WIDTH).astype(gt_ref.dtype)
    gi_ref[...] = mm(wif_ref, 0, LANES)
    gf_ref[...] = mm(wif_ref, LANES, LANES)


def _proj(x2d, g, wh, wt, wif, tm, layer, depth, kv_prev, cast=(), act_dtype=BF16):
    m = x2d.shape[0]
    first = kv_prev is None
    row = lambda width: pl.BlockSpec((tm, width), lambda i: (i, 0))
    if first:
        kv_spec = pl.BlockSpec((depth, tm * HEADS, HEAD_DIM), lambda i: (0, i, 0))
    else:
        kv_spec = pl.BlockSpec((None, tm * HEADS, HEAD_DIM), lambda i: (layer, i, 0))
    widths = (4 * WIDTH, 3 * WIDTH, WIDTH, None, None, 2 * WIDTH, 3 * D_MODEL, LANES, LANES)
    dtypes = (act_dtype, act_dtype, act_dtype, F32, F32, BF16, act_dtype, F32, F32)
    out_specs = [row(w) if w else kv_spec for w in widths]
    out_shape = [jax.ShapeDtypeStruct((m, w) if w else (depth, m * HEADS, HEAD_DIM), d)
                 for w, d in zip(widths, dtypes)]
    in_specs = [row(D_MODEL), _resident((1, D_MODEL), layer), _resident((N_HEAD_COLS, D_MODEL), layer),
                _resident((N_TAIL_COLS, D_MODEL), layer), _resident((2 * LANES, D_MODEL), layer)]
    steps = m // tm
    for w in cast:
        rows, cols = w.shape[1] // steps, w.shape[2]
        assert rows * steps == w.shape[1] and rows % 16 == 0
        in_specs.append(pl.BlockSpec((None, rows, cols), lambda i: (layer, i, 0)))
        out_specs.append(pl.BlockSpec((rows, cols), lambda i: (i, 0)))
        out_shape.append(jax.ShapeDtypeStruct(w.shape[1:], BF16))
    args = [x2d, g, wh, wt, wif, *cast]
    aliases = {}
    if not first:
        in_specs += [pl.BlockSpec(memory_space=pl.ANY)] * 2
        aliases = {len(args): 3, len(args) + 1: 4}
        args += list(kv_prev)
    return pl.pallas_call(
        functools.partial(_proj_kernel, first, len(cast)),
        grid=(steps,),
        in_specs=in_specs,
        out_specs=out_specs,
        out_shape=out_shape,
        input_output_aliases=aliases,
        compiler_params=pltpu.CompilerParams(dimension_semantics=("parallel",),
                                             vmem_limit_bytes=VMEM_LIMIT),
        name="proj",
    )(*args)


def _scan_rows(x, op, fill):
    n = x.shape[0]
    row = lax.broadcasted_iota(jnp.int32, x.shape, 0)
    k = 1
    while k < n:
        shifted = pltpu.roll(x, k, axis=0)
        x = op(x, jnp.where(row >= k, shifted, fill(x)))
        k *= 2
    return x


def _mlstm_conv_chunk(rows, pm_ref, pc_ref, gi_ref, gf_ref, bi_ref, bf_ref, cw_ref,
                      ym_ref, yc_ref, s_ref, m_ref, cs_ref):
    L = rows.stop - rows.start

    cb = pc_ref[rows, 0:WIDTH].astype(F32)
    xc = pc_ref[rows, WIDTH:2 * WIDTH].astype(F32) * pc_ref[rows, 2 * WIDTH:3 * WIDTH].astype(F32)
    prev = cs_ref[...]
    row = lax.broadcasted_iota(jnp.int32, xc.shape, 0)
    xc1 = jnp.where(row == 0, prev[1:2, :], pltpu.roll(xc, 1, axis=0))
    xc2 = jnp.where(row == 0, prev[0:1, :], jnp.where(row == 1, prev[1:2, :], pltpu.roll(xc, 2, axis=0)))
    cw = cw_ref[...]
    yc_ref[rows, :] = (cb * (cw[0:1, :] * xc2 + cw[1:2, :] * xc1 + cw[2:3, :] * xc)).astype(BF16)
    cs_ref[...] = xc[L - 2:L, :]

    m_prev = m_ref[...]
    ig = gi_ref[rows, :] + bi_ref[...]
    logf = -_softplus(-(gf_ref[rows, :] + bf_ref[...]))
    b = _scan_rows(logf, jnp.add, jnp.zeros_like)
    a = ig - b
    mx = jnp.maximum(m_prev, _scan_rows(a, jnp.maximum, lambda x: x))
    m_t = b + mx
    a_t = a.T
    s_in = jnp.exp(m_prev - mx)
    floor = jnp.exp(-m_t)
    mx_last = mx[L - 1:L, :]
    wk = jnp.exp(a - mx_last)
    decay = jnp.exp(m_prev - mx_last)
    m_ref[...] = m_t[L - 1:L, :]

    r_i = lax.broadcasted_iota(jnp.int32, (L, L), 0)
    c_i = lax.broadcasted_iota(jnp.int32, (L, L), 1)
    causal = c_i <= r_i
    one_col = (lax.broadcasted_iota(jnp.int32, (L, HEAD_DIM), 1) == 0).astype(BF16)
    for h in range(HEADS):
        sl = slice(h * HEAD_DIM, (h + 1) * HEAD_DIM)
        q = pm_ref[rows, sl]
        k = pm_ref[rows, WIDTH + h * HEAD_DIM:WIDTH + (h + 1) * HEAD_DIM]
        v = pm_ref[rows, 2 * WIDTH + h * HEAD_DIM:2 * WIDTH + (h + 1) * HEAD_DIM]
        o = pm_ref[rows, 3 * WIDTH + h * HEAD_DIM:3 * WIDTH + (h + 1) * HEAD_DIM]
        vext = jnp.concatenate([v, one_col], axis=1)
        qk = lax.dot_general(q, k, (((1,), (1,)), ((), ())), preferred_element_type=F32)
        e = jnp.exp(a_t[h:h + 1, :] - mx[:, h:h + 1])
        w = jnp.where(causal, e, 0.0) * qk
        intra = jnp.dot(w.astype(BF16), vext, preferred_element_type=F32)
        state = s_ref[h]
        inter = jnp.dot(q, state.astype(BF16), preferred_element_type=F32)
        sc = s_in[:, h:h + 1]
        num = sc * inter[:, :HEAD_DIM] + intra[:, :HEAD_DIM]
        den = sc * inter[:, HEAD_DIM:HEAD_DIM + 1] + intra[:, HEAD_DIM:HEAD_DIM + 1]
        inv = 1.0 / jnp.maximum(jnp.abs(den), floor[:, h:h + 1])
        ym_ref[rows, sl] = (jax.nn.sigmoid(o.astype(F32)) * (num * inv)).astype(BF16)
        kw_t = (k.astype(F32) * wk[:, h:h + 1]).T.astype(BF16)
        s_ref[h] = decay[:, h:h + 1] * state + jnp.dot(kw_t, vext, preferred_element_type=F32)


def _sb_block(layer, i, bsb_ref, q_ref, kv_ref, o_ref, acc_ref, run_ref):
    tq = q_ref.shape[0]
    tk = SB_K_BLOCK
    r = tq // tk
    r_i = lax.broadcasted_iota(jnp.int32, (tk, tk), 0)
    c_i = lax.broadcasted_iota(jnp.int32, (tk, tk), 1)
    after = (r_i > c_i).astype(BF16)
    visible = c_i < r_i
    acc_ref[...] = jnp.zeros_like(acc_ref)
    run_ref[...] = jnp.zeros_like(run_ref)

    def tile(kb, rows, diagonal):
        k0 = pl.multiple_of(kb * tk, tk)
        for h in range(HEADS):
            sl = slice(h * HEAD_DIM, (h + 1) * HEAD_DIM)
            k = kv_ref[pl.ds(k0, tk), sl]
            v = kv_ref[pl.ds(k0, tk), WIDTH + h * HEAD_DIM:WIDTH + (h + 1) * HEAD_DIM]
            z = (lax.dot_general(q_ref[rows, sl], k, (((1,), (1,)), ((), ())), preferred_element_type=F32)
                 + bsb_ref[layer, h] * LOG2E)
            drop, log_beta = _stick_terms(z)
            if diagonal:
                drop = jnp.where(visible, drop, 0.0)
            run = run_ref[h, rows]
            later = run + jnp.dot(drop.astype(BF16), after, preferred_element_type=F32)
            a = jnp.exp2(log_beta - later)
            if diagonal:
                a = jnp.where(visible, a, 0.0)
            acc_ref[rows, sl] += jnp.dot(a.astype(BF16), v, preferred_element_type=F32)
            run_ref[h, rows] = run + jnp.sum(drop, axis=1, keepdims=True)

    for d in reversed(range(r)):
        tile(i * r + d, slice(d * tk, (d + 1) * tk), True)
        if d + 1 < r:
            tile(i * r + d, slice((d + 1) * tk, tq), False)

    def body(j, carry):
        for u in range(r):
            tile(i * r - 1 - r * j - u, slice(0, tq), False)
        return carry

    lax.fori_loop(0, i, body, 0)
    o_ref[...] = acc_ref[...].astype(BF16)


def _mixers_kernel(layer, bsb_ref, pm_ref, pc_ref, gi_ref, gf_ref, bi_ref, bf_ref, cw_ref, q_ref, kv_ref,
                   ym_ref, yc_ref, s_ref, m_ref, cs_ref, o_ref, acc_ref, run_ref):
    i = pl.program_id(1)

    @pl.when(i == 0)
    def _():
        s_ref[...] = jnp.zeros_like(s_ref)
        m_ref[...] = jnp.zeros_like(m_ref)
        cs_ref[...] = jnp.zeros_like(cs_ref)

    rows = pm_ref.shape[0]
    for c in range(rows // MLSTM_CHUNK):
        _mlstm_conv_chunk(slice(c * MLSTM_CHUNK, (c + 1) * MLSTM_CHUNK), pm_ref, pc_ref, gi_ref, gf_ref, bi_ref,
                          bf_ref, cw_ref, ym_ref, yc_ref, s_ref, m_ref, cs_ref)
    _sb_block(layer, i, bsb_ref, q_ref, kv_ref, o_ref, acc_ref, run_ref)


def _mixers(layer, pm, pc, gi, gf, bi, bf, cw, sq, kvb, bsb, batch, seq):
    blk = SB_BLOCK
    nq = seq // blk
    m = batch * seq
    row = lambda width: pl.BlockSpec((blk, width), lambda b, i: (b * nq + i, 0))
    return pl.pallas_call(
        functools.partial(_mixers_kernel, layer),
        grid=(batch, nq),
        in_specs=[pl.BlockSpec(memory_space=pltpu.SMEM),
                  row(4 * WIDTH), row(3 * WIDTH), row(LANES), row(LANES),
                  _resident((1, LANES), layer), _resident((1, LANES), layer), _resident((3, WIDTH), layer),
                  row(WIDTH), pl.BlockSpec((seq, 2 * WIDTH), lambda b, i: (b, 0))],
        out_specs=[row(WIDTH), row(WIDTH),
                   pl.BlockSpec((None, HEADS, HEAD_DIM, 2 * HEAD_DIM), lambda b, i: (b, 0, 0, 0)),
                   pl.BlockSpec((None, 1, LANES), lambda b, i: (b, 0, 0)),
                   pl.BlockSpec((None, 2, WIDTH), lambda b, i: (b, 0, 0)),
                   row(WIDTH)],
        out_shape=[jax.ShapeDtypeStruct((m, WIDTH), BF16), jax.ShapeDtypeStruct((m, WIDTH), BF16),
                   jax.ShapeDtypeStruct((batch, HEADS, HEAD_DIM, 2 * HEAD_DIM), F32),
                   jax.ShapeDtypeStruct((batch, 1, LANES), F32),
                   jax.ShapeDtypeStruct((batch, 2, WIDTH), F32),
                   jax.ShapeDtypeStruct((m, WIDTH), BF16)],
        scratch_shapes=[pltpu.VMEM((blk, WIDTH), F32), pltpu.VMEM((HEADS, blk, 1), F32)],
        compiler_params=pltpu.CompilerParams(dimension_semantics=("parallel", "arbitrary"),
                                             vmem_limit_bytes=VMEM_LIMIT),
        name="mixers",
    )(bsb, pm, pc, gi, gf, bi, bf, cw, sq, kvb)


def _decode_seq(layer, bsb_ref, pm_ref, pc_ref, sq_ref, gi_ref, gf_ref, c0_ref, n0_ref, m0_ref, cv0_ref, bi_ref,
                bf_ref, cw_ref, k_pages, v_pages, ym_ref, yc_ref, ys_ref, c_slots, n_ref, m_ref, cv_ref):
    n_pages = len(k_pages)
    cw = cw_ref[...]
    pw = k_pages[0].shape[0]

    cb = pc_ref[0, :, 0:WIDTH].astype(F32)
    xc = pc_ref[0, :, WIDTH:2 * WIDTH].astype(F32) * pc_ref[0, :, 2 * WIDTH:3 * WIDTH].astype(F32)
    prev = cv0_ref[0]
    yc_ref[0] = (cb * (cw[0:1, :] * prev[0:1, :] + cw[1:2, :] * prev[1:2, :] + cw[2:3, :] * xc)).astype(BF16)
    cv_ref[0, 0:1, :] = prev[1:2, :]
    cv_ref[0, 1:2, :] = xc

    ig = gi_ref[0] + bi_ref[...]
    logf = -_softplus(-(gf_ref[0] + bf_ref[...]))
    m0 = m0_ref[0]
    m_t = jnp.maximum(logf + m0, ig)
    w_in = jnp.exp(ig - m_t)
    s_in = jnp.exp(logf + m0 - m_t)
    floor = jnp.exp(-m_t)
    m_ref[0] = m_t
    row128 = lax.broadcasted_iota(jnp.int32, (HEAD_DIM, HEAD_DIM), 0)
    for h in range(HEADS):
        sl = slice(h * HEAD_DIM, (h + 1) * HEAD_DIM)
        q = pm_ref[0, :, sl].astype(F32)
        k = pm_ref[0, :, WIDTH + h * HEAD_DIM:WIDTH + (h + 1) * HEAD_DIM].astype(F32)
        v = pm_ref[0, :, 2 * WIDTH + h * HEAD_DIM:2 * WIDTH + (h + 1) * HEAD_DIM].astype(F32)
        o = pm_ref[0, :, 3 * WIDTH + h * HEAD_DIM:3 * WIDTH + (h + 1) * HEAD_DIM].astype(F32)
        qk_rows = jnp.where(row128 == 0, q, jnp.where(row128 == 1, k, 0.0))
        cols = qk_rows.T
        q_col = cols[:, 0:1]
        k_col = cols[:, 1:2]
        c0 = c0_ref[0, h]
        n0 = n0_ref[0, h:h + 1, :]
        sc = s_in[:, h:h + 1]
        wi = w_in[:, h:h + 1]
        w = wi * jnp.sum(q * k, axis=1, keepdims=True)
        num = sc * jnp.sum(q_col * c0, axis=0, keepdims=True) + w * v
        den = sc * jnp.sum(q * n0, axis=1, keepdims=True) + w
        hc = num * (1.0 / jnp.maximum(jnp.abs(den), floor[:, h:h + 1]))
        ym_ref[0, :, sl] = (jax.nn.sigmoid(o) * hc).astype(BF16)
        c_new = sc * c0 + wi * (k_col * v)
        for slot in c_slots:
            slot[0, h] = c_new
        n_ref[0, h:h + 1, :] = sc * n0 + wi * k

    r8 = lax.broadcasted_iota(jnp.int32, (SUBLANES, HEAD_DIM), 0)
    rcol = lax.broadcasted_iota(jnp.int32, (SUBLANES, 1), 0)
    sq = sq_ref[0].astype(F32)
    q_rows = jnp.zeros((SUBLANES, HEAD_DIM), F32)
    bias = jnp.zeros((SUBLANES, 1), F32)
    for h in range(HEADS):
        q_rows = jnp.where(r8 == h, sq[:, h * HEAD_DIM:(h + 1) * HEAD_DIM], q_rows)
        bias = jnp.where(rcol == h, bsb_ref[layer, h] * LOG2E, bias)
    q_rows = q_rows.astype(BF16)
    own = (lax.broadcasted_iota(jnp.int32, (SUBLANES, pw), 1) % HEADS
           == lax.broadcasted_iota(jnp.int32, (SUBLANES, pw), 0))
    r_i = lax.broadcasted_iota(jnp.int32, (pw, pw), 0)
    c_i = lax.broadcasted_iota(jnp.int32, (pw, pw), 1)
    after = jnp.logical_and(r_i // HEADS > c_i // HEADS, r_i % HEADS == c_i % HEADS).astype(BF16)
    k_all = jnp.concatenate([kp[...].astype(BF16) for kp in k_pages], axis=0)
    z = lax.dot_general(q_rows, k_all, (((1,), (1,)), ((), ())), preferred_element_type=F32) + bias
    drop, log_beta = _stick_terms(z)
    own_all = jnp.concatenate([own] * n_pages, axis=1)
    drop = jnp.where(own_all, drop, 0.0)
    drop_rows = jnp.concatenate([drop[:, j * pw:(j + 1) * pw] for j in range(n_pages)], axis=0)
    within = jnp.dot(drop_rows.astype(BF16), after, preferred_element_type=F32)
    totals = jnp.sum(drop_rows, axis=1, keepdims=True)
    run = jnp.zeros((SUBLANES, 1), F32)
    laters = [None] * n_pages
    for j in reversed(range(n_pages)):
        rows = slice(SUBLANES * j, SUBLANES * (j + 1))
        laters[j] = within[rows, :] + run
        run = run + totals[rows, :]
    later = jnp.concatenate(laters, axis=1)
    a = jnp.where(own_all, jnp.exp2(log_beta - later), 0.0)
    v_all = jnp.concatenate([vp[...].astype(BF16) for vp in v_pages], axis=0)
    out = jnp.dot(a.astype(BF16), v_all, preferred_element_type=F32)
    for h in range(HEADS):
        ys_ref[0, :, h * HEAD_DIM:(h + 1) * HEAD_DIM] = out[h:h + 1, :].astype(BF16)


_POST_PARAM_SHAPES = ((3, WIDTH, D_MODEL), (D_MODEL, D_MODEL), (1, D_MODEL), (D_MODEL, FFN_DIM), (FFN_DIM, D_MODEL),
                      (1, D_MODEL), (D_MODEL, D_MODEL), (PLE_DIM, D_MODEL), (1, D_MODEL))


def _post_kernel(final, ym_ref, yc_ref, ys_ref, gt_ref, h_ref, p_ref, wb_ref, wo_ref, gmlp_ref, wup_ref,
                 wdn_ref, gple_ref, wpg_ref, wple_ref, gfin_ref, o_ref):
    mix = None
    for n, y_ref in enumerate((ym_ref, yc_ref, ys_ref)):
        pb = jnp.dot(y_ref[...], wb_ref[n], preferred_element_type=F32)
        gate = jax.nn.sigmoid(gt_ref[:, n * D_MODEL:(n + 1) * D_MODEL].astype(F32))
        mix = gate * pb if mix is None else mix + gate * pb
    h = h_ref[...] + jnp.dot(mix.astype(BF16), wo_ref[...], preferred_element_type=F32)

    xn = _rms(h, gmlp_ref[...]).astype(BF16)
    for c in range(FFN_DIM // FFN_CHUNK):
        u = jnp.dot(xn, wup_ref[:, c * FFN_CHUNK:(c + 1) * FFN_CHUNK], preferred_element_type=F32)
        r = jnp.square(jnp.maximum(u, 0.0)).astype(BF16)
        h = h + jnp.dot(r, wdn_ref[c * FFN_CHUNK:(c + 1) * FFN_CHUNK, :], preferred_element_type=F32)

    gate = jax.nn.sigmoid(jnp.dot(_rms(h, gple_ref[...]).astype(BF16), wpg_ref[...],
                                  preferred_element_type=F32))
    h = h + jnp.dot(p_ref[...].astype(BF16), wple_ref[...], preferred_element_type=F32) * gate
    o_ref[...] = _rms(h, gfin_ref[...]) if final else h


def _post(final, layer, ym, yc, ys, gt, h, p, wb, wo, gmlp, wup, wdn, gple, wpg, wple, gfin, tm):
    m = h.shape[0]
    row = lambda width: pl.BlockSpec((tm, width), lambda i: (i, 0))
    p_spec = pl.BlockSpec((None, tm, PLE_DIM), lambda i: (layer, i, 0))
    return pl.pallas_call(
        functools.partial(_post_kernel, final),
        grid=(m // tm,),
        in_specs=[row(WIDTH), row(WIDTH), row(WIDTH), row(3 * D_MODEL), row(D_MODEL), p_spec,
                  *[_param_spec(w, shape, layer) for w, shape in zip(
                      (wb, wo, gmlp, wup, wdn, gple, wpg, wple, gfin), _POST_PARAM_SHAPES)]],
        out_specs=row(D_MODEL),
        out_shape=jax.ShapeDtypeStruct((m, D_MODEL), F32),
        compiler_params=pltpu.CompilerParams(dimension_semantics=("parallel",),
                                             vmem_limit_bytes=VMEM_LIMIT),
        name="post",
    )(ym, yc, ys, gt, h, p, wb, wo, gmlp, wup, wdn, gple, wpg, wple, gfin)


N_POST_IN = 15
N_DECODE_IN = 12


def _post_decode_kernel(final, layer, n_pages, first_layer, pt_ref, bsb_ref, *refs):
    (ym_ref, yc_ref, ys_ref, gt_ref, h_ref, p_ref, wb_ref, wo_ref, gmlp_ref, wup_ref, wdn_ref, gple_ref,
     wpg_ref, wple_ref, gfin_ref) = refs[:N_POST_IN]
    dec_in = refs[N_POST_IN:N_POST_IN + N_DECODE_IN]
    ck_hbm, cv_hbm = refs[N_POST_IN + N_DECODE_IN:N_POST_IN + N_DECODE_IN + 2]
    (o_ref, ym_s_ref, yc_s_ref, ys_s_ref, c_ref, n_ref, m_ref, cv_ref,
     acc_ref, xn_ref, kbuf, vbuf, sem) = refs[-13:]
    t = pl.program_id(0)
    slot = t % 2

    def page_copies(seq, buf):
        copies = []
        for j in range(n_pages):
            pid = pt_ref[seq, j]
            copies.append(pltpu.make_async_copy(ck_hbm.at[layer, pid], kbuf.at[buf, j], sem.at[buf]))
            copies.append(pltpu.make_async_copy(cv_hbm.at[layer, pid], vbuf.at[buf, j], sem.at[buf]))
        return copies

    @pl.when(t == 0)
    def _():
        for cp in page_copies(0, 0):
            cp.start()

    @pl.when(t + 1 < pl.num_programs(0))
    def _():
        for cp in page_copies(t + 1, 1 - slot):
            cp.start()

    for cp in page_copies(t, slot):
        cp.wait()
    k_pages = [kbuf.at[slot, j] for j in range(n_pages)]
    v_pages = [vbuf.at[slot, j] for j in range(n_pages)]
    c_slots = [c_ref.at[l] for l in range(c_ref.shape[0])] if first_layer else [c_ref]
    half = FFN_DIM // 2

    def decode():
        _decode_seq(layer, bsb_ref, *dec_in, k_pages, v_pages, ym_s_ref, yc_s_ref, ys_s_ref, c_slots, n_ref, m_ref,
                    cv_ref)

    def mlp_half(xn, h, base):
        for c in range(half // FFN_CHUNK):
            lo = base + c * FFN_CHUNK
            u = jnp.dot(xn, wup_ref[:, lo:lo + FFN_CHUNK], preferred_element_type=F32)
            r = jnp.square(jnp.maximum(u, 0.0)).astype(BF16)
            h = h + jnp.dot(r, wdn_ref[lo:lo + FFN_CHUNK, :], preferred_element_type=F32)
        return h

    parity = pl.program_id(0) % 2

    @pl.when(parity == 0)
    def _():
        decode()
        mix = None
        for n, y_ref in enumerate((ym_ref, yc_ref, ys_ref)):
            pb = jnp.dot(y_ref[...], wb_ref[n], preferred_element_type=F32)
            gate = jax.nn.sigmoid(gt_ref[:, n * D_MODEL:(n + 1) * D_MODEL].astype(F32))
            mix = gate * pb if mix is None else mix + gate * pb
        h = h_ref[...] + jnp.dot(mix.astype(BF16), wo_ref[...], preferred_element_type=F32)
        xn = _rms(h, gmlp_ref[...]).astype(BF16)
        xn_ref[...] = xn
        acc_ref[...] = mlp_half(xn, h, 0)

    @pl.when(parity == 1)
    def _():
        decode()
        h = mlp_half(xn_ref[...], acc_ref[...], half)
        gate = jax.nn.sigmoid(jnp.dot(_rms(h, gple_ref[...]).astype(BF16), wpg_ref[...],
                                      preferred_element_type=F32))
        h = h + jnp.dot(p_ref[...].astype(BF16), wple_ref[...], preferred_element_type=F32) * gate
        o_ref[...] = _rms(h, gfin_ref[...]) if final else h


def _post_decode(final, layer, depth, ym, yc, ys, gt, h, p, post_w, pm, pc, sq, gi, gf, state_c, state_n, m0,
                 state_conv, bi, bf, cw, bsb, page_table, cache_k, cache_v, c_prev):
    nseq, n_pages = page_table.shape
    pw = cache_k.shape[2]
    m = h.shape[0]
    tm = 2 * m // nseq
    assert tm * nseq == 2 * m and tm % 16 == 0
    first = c_prev is None
    prow = lambda width: pl.BlockSpec((tm, width), lambda t, pt: (t // 2, 0))
    fixed = lambda shape: _resident(shape, layer)
    srow = lambda width: pl.BlockSpec((1, 1, width), lambda t, pt: (t, 0, 0))
    state4 = pl.BlockSpec((None, 1, HEADS, HEAD_DIM, HEAD_DIM), lambda t, pt: (layer, t, 0, 0, 0))
    state3 = pl.BlockSpec((None, 1, HEADS, HEAD_DIM), lambda t, pt: (layer, t, 0, 0))
    conv3 = pl.BlockSpec((None, 1, 2, WIDTH), lambda t, pt: (layer, t, 0, 0))

    post_specs = [prow(WIDTH), prow(WIDTH), prow(WIDTH), prow(3 * D_MODEL), prow(D_MODEL),
                  pl.BlockSpec((None, tm, PLE_DIM), lambda t, pt: (layer, t // 2, 0)),
                  *[_param_spec(w, shape, layer) for w, shape in zip(post_w, _POST_PARAM_SHAPES)]]
    dec_specs = [srow(4 * WIDTH), srow(3 * WIDTH), srow(WIDTH), srow(LANES), srow(LANES), state4, state3, srow(LANES),
                 conv3, fixed((1, LANES)), fixed((1, LANES)), fixed((3, WIDTH))]
    assert len(post_specs) == N_POST_IN and len(dec_specs) == N_DECODE_IN
    in_specs = ([pl.BlockSpec(memory_space=pltpu.SMEM)] + post_specs + dec_specs
                + [pl.BlockSpec(memory_space=pl.ANY)] * 2)
    args = ([page_table, bsb, ym, yc, ys, gt, h, p, *post_w]
            + [a.reshape(nseq, 1, a.shape[-1]) for a in (pm, pc, sq, gi, gf)]
            + [state_c, state_n, m0, state_conv, bi, bf, cw, cache_k, cache_v])
    if first:
        c_spec = pl.BlockSpec((depth, 1, HEADS, HEAD_DIM, HEAD_DIM), lambda t, pt: (0, t, 0, 0, 0))
        aliases = {}
    else:
        c_spec = pl.BlockSpec((None, 1, HEADS, HEAD_DIM, HEAD_DIM), lambda t, pt: (layer, t, 0, 0, 0))
        in_specs.append(pl.BlockSpec(memory_space=pl.ANY))
        args.append(c_prev)
        aliases = {len(args) - 1: 4}
    out_specs = [prow(D_MODEL), srow(WIDTH), srow(WIDTH), srow(WIDTH), c_spec,
                 pl.BlockSpec((1, HEADS, HEAD_DIM), lambda t, pt: (t, 0, 0)),
                 srow(LANES),
                 pl.BlockSpec((1, 2, WIDTH), lambda t, pt: (t, 0, 0))]
    out_shape = [jax.ShapeDtypeStruct((m, D_MODEL), F32)] + [jax.ShapeDtypeStruct((nseq, 1, WIDTH), BF16)] * 3 + [
        jax.ShapeDtypeStruct((depth, nseq, HEADS, HEAD_DIM, HEAD_DIM), F32),
        jax.ShapeDtypeStruct((nseq, HEADS, HEAD_DIM), F32),
        jax.ShapeDtypeStruct((nseq, 1, LANES), F32),
        jax.ShapeDtypeStruct((nseq, 2, WIDTH), F32)]
    return pl.pallas_call(
        functools.partial(_post_decode_kernel, final, layer, n_pages, first),
        grid_spec=pltpu.PrefetchScalarGridSpec(
            num_scalar_prefetch=1, grid=(nseq,), in_specs=in_specs, out_specs=out_specs,
            scratch_shapes=[pltpu.VMEM((tm, D_MODEL), F32), pltpu.VMEM((tm, D_MODEL), BF16),
                            pltpu.VMEM((2, n_pages, pw, HEAD_DIM), F32), pltpu.VMEM((2, n_pages, pw, HEAD_DIM), F32),
                            pltpu.SemaphoreType.DMA((2,))]),
        out_shape=out_shape,
        input_output_aliases=aliases,
        compiler_params=pltpu.CompilerParams(dimension_semantics=("arbitrary",),
                                             vmem_limit_bytes=VMEM_LIMIT),
        name="post_decode",
    )(*args)


def kernel(x_prompt, x_sample, cache_k, cache_v, state_mlstm_C, state_mlstm_n, state_mlstm_m, state_conv,
           page_table, p_prompt, p_sample, g_mix, w_in, b_if, b_sb, conv_w, w_branch, w_out, g_mlp, w_up,
           w_down, g_ple, w_ple_gate, w_ple, g_final):
    depth = w_in.shape[0]
    batch, seq, _ = x_prompt.shape
    nseq = x_sample.shape[0]
    n_phys, page = cache_k.shape[1], cache_k.shape[2]
    cache_k = cache_k.reshape(depth, n_phys, page * HEADS, HEAD_DIM)
    cache_v = cache_v.reshape(depth, n_phys, page * HEADS, HEAD_DIM)
    hp = x_prompt.reshape(batch * seq, D_MODEL)
    hs = x_sample.reshape(nseq, D_MODEL)
    pp = p_prompt.reshape(depth, batch * seq, PLE_DIM)
    ps = p_sample.reshape(depth, nseq, PLE_DIM)
    gfin = g_final.reshape(1, D_MODEL)
    tm_p = 512
    tm_s = nseq

    vec = lambda g: g.reshape(depth, 1, D_MODEL)
    w_t = jnp.swapaxes(w_in, 1, 2)
    gates0 = N_HEAD_COLS
    wh = w_t[:, :gates0].astype(BF16)
    wt = w_t[:, gates0 + 2 * HEADS:].astype(BF16)
    zpad = jnp.zeros((depth, LANES - HEADS, D_MODEL), w_in.dtype)
    wif = jnp.concatenate([w_t[:, gates0:gates0 + HEADS], zpad,
                           w_t[:, gates0 + HEADS:gates0 + 2 * HEADS], zpad], axis=1).astype(BF16)
    gmix = vec(g_mix)
    bi = jnp.pad(b_if[:, :HEADS], ((0, 0), (0, LANES - HEADS))).reshape(depth, 1, LANES)
    bf = jnp.pad(b_if[:, HEADS:], ((0, 0), (0, LANES - HEADS))).reshape(depth, 1, LANES)
    post_f32 = (w_branch.reshape(depth, 3 * WIDTH, D_MODEL), w_out, w_up, w_down, w_ple_gate)
    wple = w_ple.astype(BF16)
    m0_all = jnp.pad(state_mlstm_m, ((0, 0), (0, 0), (0, LANES - HEADS)))

    outs_p, outs_s = [], []
    kv_p = kv_s = c_s = None
    for li in range(depth):
        final = li == depth - 1

        pm, pc, sq, kf, vf, kvb, gt, gi, gf, wb, wo, wup, wdn, wpg = _proj(
            hp, gmix, wh, wt, wif, tm_p, li, depth, kv_p, post_f32)
        kv_p = (kf, vf)
        post_w = (wb.reshape(3, WIDTH, D_MODEL), wo, vec(g_mlp), wup, wdn, vec(g_ple), wpg, wple, gfin)
        ym, yc, s_p, m_p, cv_p, ys = _mixers(li, pm, pc, gi, gf, bi, bf, conv_w, sq, kvb, b_sb, batch, seq)
        outs_p.append((s_p[..., :HEAD_DIM], s_p[..., HEAD_DIM], m_p[:, 0, :HEADS], cv_p))
        pm_s, pc_s, sq_s, kf, vf, _, gt_s, gi_s, gf_s = _proj(hs, gmix, wh, wt, wif, tm_s, li, depth, kv_s,
                                                              act_dtype=F32)
        kv_s = (kf, vf)
        m0 = m0_all[li].reshape(nseq, 1, LANES)

        hp, ym_s, yc_s, ys_s, c_s, n_s, m_s, cv_s = _post_decode(
            final, li, depth, ym, yc, ys, gt, hp, pp, post_w, pm_s, pc_s, sq_s, gi_s, gf_s, state_mlstm_C,
            state_mlstm_n, m0, state_conv, bi, bf, conv_w, b_sb, page_table, cache_k, cache_v, c_s)
        hs = _post(final, li, ym_s.reshape(nseq, WIDTH), yc_s.reshape(nseq, WIDTH), ys_s.reshape(nseq, WIDTH),
                   gt_s, hs, ps, *post_w, tm_s)
        outs_s.append((n_s, m_s[:, 0, :HEADS], cv_s))

    stack = lambda outs, j: jnp.stack([o[j] for o in outs])
    k_p, v_p = (a.reshape(depth, batch, seq, HEADS, HEAD_DIM) for a in kv_p)
    k_s, v_s = (a.reshape(depth, nseq, 1, HEADS, HEAD_DIM) for a in kv_s)
    return ((hp.reshape(batch, seq, D_MODEL), hs.reshape(nseq, 1, D_MODEL), k_p, v_p)
            + tuple(stack(outs_p, j) for j in range(4)) + (k_s, v_s, c_s)
            + tuple(stack(outs_s, j) for j in range(3)))
```

```python
import functools

import jax
import jax.numpy as jnp
from jax import lax
from jax.experimental import pallas as pl
from jax.experimental.pallas import tpu as pltpu

F32 = jnp.float32
BF16 = jnp.bfloat16

D_MODEL = 1024
WIDTH = 512
HEADS = 4
HEAD_DIM = 128
FFN_DIM = 4 * D_MODEL
PLE_DIM = 256
RMS_EPS = 1e-6
LOG2E = 1.4426950408889634
LANES = 128
SUBLANES = 8

T_PC = 0
T_SQ = 3 * WIDTH
T_SK = 4 * WIDTH
T_SV = 5 * WIDTH
T_GT = 6 * WIDTH
N_HEAD_COLS = 4 * WIDTH
N_TAIL_COLS = 6 * WIDTH + 3 * D_MODEL
FFN_CHUNK = 1024

MLSTM_CHUNK = 256
SB_BLOCK = 512
SB_K_BLOCK = 256
VMEM_LIMIT = 56 * 1024 * 1024


def _softplus(x):
    return jnp.maximum(x, 0.0) + jnp.log(1.0 + jnp.exp(-jnp.abs(x)))


def _stick_terms(z):
    drop = jnp.maximum(z, 0.0) + jnp.log2(1.0 + jnp.exp2(-jnp.abs(z)))
    return drop, z - drop


def _rms(x, g):
    return x * lax.rsqrt(jnp.mean(x * x, axis=-1, keepdims=True) + RMS_EPS) * g


def _resident(shape, layer=None):
    nd = len(shape)
    if layer is None:
        return pl.BlockSpec(shape, lambda *_: (0,) * nd, pipeline_mode=pl.Buffered(1))
    return pl.BlockSpec((None,) + tuple(shape), lambda *_: (layer,) + (0,) * nd, pipeline_mode=pl.Buffered(1))


def _param_spec(arr, shape, layer):
    return _resident(shape, layer if arr.ndim == len(shape) + 1 else None)


def _proj_kernel(first_layer, n_cast, x_ref, g_ref, wh_ref, wt_ref, wif_ref, *refs):
    cast_in = refs[:n_cast]
    cast_out = refs[len(refs) - n_cast:]
    pm_ref, pc_ref, sq_ref, kf_ref, vf_ref, kvb_ref, gt_ref, gi_ref, gf_ref = refs[len(refs) - n_cast - 9:len(refs) - n_cast]
    for src, dst in zip(cast_in, cast_out):
        dst[...] = src[...].astype(BF16)
    xn = _rms(x_ref[...], g_ref[...]).astype(BF16)

    def mm(w_ref, c0, width):
        return lax.dot_general(xn, w_ref[c0:c0 + width, :], (((1,), (1,)), ((), ())), preferred_element_type=F32)

    scale = HEAD_DIM ** -0.5
    for j in range(4):
        part = mm(wh_ref, j * WIDTH, WIDTH)
        pm_ref[:, j * WIDTH:(j + 1) * WIDTH] = (part * scale if j == 1 else part).astype(pm_ref.dtype)
    for j in range(3):
        pc_ref[:, j * WIDTH:(j + 1) * WIDTH] = mm(wt_ref, T_PC + j * WIDTH, WIDTH).astype(pc_ref.dtype)
    sq_ref[...] = (mm(wt_ref, T_SQ, WIDTH) * (scale * LOG2E)).astype(sq_ref.dtype)
    k = mm(wt_ref, T_SK, WIDTH)
    v = mm(wt_ref, T_SV, WIDTH)
    tm = k.shape[0]
    slots = [kf_ref.at[l] for l in range(kf_ref.shape[0])] if first_layer else [kf_ref]
    vslots = [vf_ref.at[l] for l in range(vf_ref.shape[0])] if first_layer else [vf_ref]
    for h in range(HEADS):
        for kslot, vslot in zip(slots, vslots):
            kslot[pl.ds(h, tm, stride=HEADS), :] = k[:, h * HEAD_DIM:(h + 1) * HEAD_DIM]
            vslot[pl.ds(h, tm, stride=HEADS), :] = v[:, h * HEAD_DIM:(h + 1) * HEAD_DIM]
    kvb_ref[:, 0:WIDTH] = k.astype(BF16)
    kvb_ref[:, WIDTH:2 * WIDTH] = v.astype(BF16)
    for j in range(3 * D_MODEL // WIDTH):
        gt_ref[:, j * WIDTH:(j + 1) * WIDTH] = mm(wt_ref, T_GT + j * WIDTH, WIDTH).astype(gt_ref.dtype)
    gi_ref[...] = mm(wif_ref, 0, LANES)
    gf_ref[...] = mm(wif_ref, LANES, LANES)


def _proj(x2d, g, wh, wt, wif, tm, layer, depth, kv_prev, cast=(), act_dtype=BF16):
    m = x2d.shape[0]
    first = kv_prev is None
    row = lambda width: pl.BlockSpec((tm, width), lambda i: (i, 0))
    if first:
        kv_spec = pl.BlockSpec((depth, tm * HEADS, HEAD_DIM), lambda i: (0, i, 0))
    else:
        kv_spec = pl.BlockSpec((None, tm * HEADS, HEAD_DIM), lambda i: (layer, i, 0))
    widths = (4 * WIDTH, 3 * WIDTH, WIDTH, None, None, 2 * WIDTH, 3 * D_MODEL, LANES, LANES)
    dtypes = (act_dtype, act_dtype, act_dtype, F32, F32, BF16, act_dtype, F32, F32)
    out_specs = [row(w) if w else kv_spec for w in widths]
    out_shape = [jax.ShapeDtypeStruct((m, w) if w else (depth, m * HEADS, HEAD_DIM), d)
                 for w, d in zip(widths, dtypes)]
    in_specs = [row(D_MODEL), _resident((1, D_MODEL), layer), _resident((N_HEAD_COLS, D_MODEL), layer),
                _resident((N_TAIL_COLS, D_MODEL), layer), _resident((2 * LANES, D_MODEL), layer)]
    steps = m // tm
    for w in cast:
        rows, cols = w.shape[1] // steps, w.shape[2]
        assert rows * steps == w.shape[1] and rows % 16 == 0
        in_specs.append(pl.BlockSpec((None, rows, cols), lambda i: (layer, i, 0)))
        out_specs.append(pl.BlockSpec((rows, cols), lambda i: (i, 0)))
        out_shape.append(jax.ShapeDtypeStruct(w.shape[1:], BF16))
    args = [x2d, g, wh, wt, wif, *cast]
    aliases = {}
    if not first:
        in_specs += [pl.BlockSpec(memory_space=pl.ANY)] * 2
        aliases = {len(args): 3, len(args) + 1: 4}
        args += list(kv_prev)
    return pl.pallas_call(
        functools.partial(_proj_kernel, first, len(cast)),
        grid=(steps,),
        in_specs=in_specs,
        out_specs=out_specs,
        out_shape=out_shape,
        input_output_aliases=aliases,
        compiler_params=pltpu.CompilerParams(dimension_semantics=("parallel",),
                                             vmem_limit_bytes=VMEM_LIMIT),
        name="proj",
    )(*args)


def _scan_rows(x, op, fill):
    n = x.shape[0]
    row = lax.broadcasted_iota(jnp.int32, x.shape, 0)
    k = 1
    while k < n:
        shifted = pltpu.roll(x, k, axis=0)
        x = op(x, jnp.where(row >= k, shifted, fill(x)))
        k *= 2
    return x


def _mlstm_conv_chunk(rows, pm_ref, pc_ref, gi_ref, gf_ref, bi_ref, bf_ref, cw_ref,
                      ym_ref, yc_ref, s_ref, m_ref, cs_ref):
    L = rows.stop - rows.start

    cb = pc_ref[rows, 0:WIDTH].astype(F32)
    xc = pc_ref[rows, WIDTH:2 * WIDTH].astype(F32) * pc_ref[rows, 2 * WIDTH:3 * WIDTH].astype(F32)
    prev = cs_ref[...]
    row = lax.broadcasted_iota(jnp.int32, xc.shape, 0)
    xc1 = jnp.where(row == 0, prev[1:2, :], pltpu.roll(xc, 1, axis=0))
    xc2 = jnp.where(row == 0, prev[0:1, :], jnp.where(row == 1, prev[1:2, :], pltpu.roll(xc, 2, axis=0)))
    cw = cw_ref[...]
    yc_ref[rows, :] = (cb * (cw[0:1, :] * xc2 + cw[1:2, :] * xc1 + cw[2:3, :] * xc)).astype(BF16)
    cs_ref[...] = xc[L - 2:L, :]

    m_prev = m_ref[...]
    ig = gi_ref[rows, :] + bi_ref[...]
    logf = -_softplus(-(gf_ref[rows, :] + bf_ref[...]))
    b = _scan_rows(logf, jnp.add, jnp.zeros_like)
    a = ig - b
    mx = jnp.maximum(m_prev, _scan_rows(a, jnp.maximum, lambda x: x))
    m_t = b + mx
    a_t = a.T
    s_in = jnp.exp(m_prev - mx)
    floor = jnp.exp(-m_t)
    mx_last = mx[L - 1:L, :]
    wk = jnp.exp(a - mx_last)
    decay = jnp.exp(m_prev - mx_last)
    m_ref[...] = m_t[L - 1:L, :]

    r_i = lax.broadcasted_iota(jnp.int32, (L, L), 0)
    c_i = lax.broadcasted_iota(jnp.int32, (L, L), 1)
    causal = c_i <= r_i
    one_col = (lax.broadcasted_iota(jnp.int32, (L, HEAD_DIM), 1) == 0).astype(BF16)
    for h in range(HEADS):
        sl = slice(h * HEAD_DIM, (h + 1) * HEAD_DIM)
        q = pm_ref[rows, sl]
        k = pm_ref[rows, WIDTH + h * HEAD_DIM:WIDTH + (h + 1) * HEAD_DIM]
        v = pm_ref[rows, 2 * WIDTH + h * HEAD_DIM:2 * WIDTH + (h + 1) * HEAD_DIM]
        o = pm_ref[rows, 3 * WIDTH + h * HEAD_DIM:3 * WIDTH + (h + 1) * HEAD_DIM]
        vext = jnp.concatenate([v, one_col], axis=1)
        qk = lax.dot_general(q, k, (((1,), (1,)), ((), ())), preferred_element_type=F32)
        e = jnp.exp(a_t[h:h + 1, :] - mx[:, h:h + 1])
        w = jnp.where(causal, e, 0.0) * qk
        intra = jnp.dot(w.astype(BF16), vext, preferred_element_type=F32)
        state = s_ref[h]
        inter = jnp.dot(q, state.astype(BF16), preferred_element_type=F32)
        sc = s_in[:, h:h + 1]
        num = sc * inter[:, :HEAD_DIM] + intra[:, :HEAD_DIM]
        den = sc * inter[:, HEAD_DIM:HEAD_DIM + 1] + intra[:, HEAD_DIM:HEAD_DIM + 1]
        inv = 1.0 / jnp.maximum(jnp.abs(den), floor[:, h:h + 1])
        ym_ref[rows, sl] = (jax.nn.sigmoid(o.astype(F32)) * (num * inv)).astype(BF16)
        kw_t = (k.astype(F32) * wk[:, h:h + 1]).T.astype(BF16)
        s_ref[h] = decay[:, h:h + 1] * state + jnp.dot(kw_t, vext, preferred_element_type=F32)


def _sb_block(layer, i, bsb_ref, q_ref, kv_ref, o_ref, acc_ref, run_ref):
    tq = q_ref.shape[0]
    tk = SB_K_BLOCK
    r = tq // tk
    r_i = lax.broadcasted_iota(jnp.int32, (tk, tk), 0)
    c_i = lax.broadcasted_iota(jnp.int32, (tk, tk), 1)
    after = (r_i > c_i).astype(BF16)
    visible = c_i < r_i
    acc_ref[...] = jnp.zeros_like(acc_ref)
    run_ref[...] = jnp.zeros_like(run_ref)

    def tile(kb, rows, diagonal):
        k0 = pl.multiple_of(kb * tk, tk)
        for h in range(HEADS):
            sl = slice(h * HEAD_DIM, (h + 1) * HEAD_DIM)
            k = kv_ref[pl.ds(k0, tk), sl]
            v = kv_ref[pl.ds(k0, tk), WIDTH + h * HEAD_DIM:WIDTH + (h + 1) * HEAD_DIM]
            z = (lax.dot_general(q_ref[rows, sl], k, (((1,), (1,)), ((), ())), preferred_element_type=F32)
                 + bsb_ref[layer, h] * LOG2E)
            drop, log_beta = _stick_terms(z)
            if diagonal:
                drop = jnp.where(visible, drop, 0.0)
            run = run_ref[h, rows]
            later = run + jnp.dot(drop.astype(BF16), after, preferred_element_type=F32)
            a = jnp.exp2(log_beta - later)
            if diagonal:
                a = jnp.where(visible, a, 0.0)
            acc_ref[rows, sl] += jnp.dot(a.astype(BF16), v, preferred_element_type=F32)
            run_ref[h, rows] = run + jnp.sum(drop, axis=1, keepdims=True)

    for d in reversed(range(r)):
        tile(i * r + d, slice(d * tk, (d + 1) * tk), True)
        if d + 1 < r:
            tile(i * r + d, slice((d + 1) * tk, tq), False)

    def body(j, carry):
        for u in range(r):
            tile(i * r - 1 - r * j - u, slice(0, tq), False)
        return carry

    lax.fori_loop(0, i, body, 0)
    o_ref[...] = acc_ref[...].astype(BF16)


def _mixers_kernel(layer, bsb_ref, pm_ref, pc_ref, gi_ref, gf_ref, bi_ref, bf_ref, cw_ref, q_ref, kv_ref,
                   ym_ref, yc_ref, s_ref, m_ref, cs_ref, o_ref, acc_ref, run_ref):
    i = pl.program_id(1)

    @pl.when(i == 0)
    def _():
        s_ref[...] = jnp.zeros_like(s_ref)
        m_ref[...] = jnp.zeros_like(m_ref)
        cs_ref[...] = jnp.zeros_like(cs_ref)

    rows = pm_ref.shape[0]
    for c in range(rows // MLSTM_CHUNK):
        _mlstm_conv_chunk(slice(c * MLSTM_CHUNK, (c + 1) * MLSTM_CHUNK), pm_ref, pc_ref, gi_ref, gf_ref, bi_ref,
                          bf_ref, cw_ref, ym_ref, yc_ref, s_ref, m_ref, cs_ref)
    _sb_block(layer, i, bsb_ref, q_ref, kv_ref, o_ref, acc_ref, run_ref)


def _mixers(layer, pm, pc, gi, gf, bi, bf, cw, sq, kvb, bsb, batch, seq):
    blk = SB_BLOCK
    nq = seq // blk
    m = batch * seq
    row = lambda width: pl.BlockSpec((blk, width), lambda b, i: (b * nq + i, 0))
    return pl.pallas_call(
        functools.partial(_mixers_kernel, layer),
        grid=(batch, nq),
        in_specs=[pl.BlockSpec(memory_space=pltpu.SMEM),
                  row(4 * WIDTH), row(3 * WIDTH), row(LANES), row(LANES),
                  _resident((1, LANES), layer), _resident((1, LANES), layer), _resident((3, WIDTH), layer),
                  row(WIDTH), pl.BlockSpec((seq, 2 * WIDTH), lambda b, i: (b, 0))],
        out_specs=[row(WIDTH), row(WIDTH),
                   pl.BlockSpec((None, HEADS, HEAD_DIM, 2 * HEAD_DIM), lambda b, i: (b, 0, 0, 0)),
                   pl.BlockSpec((None, 1, LANES), lambda b, i: (b, 0, 0)),
                   pl.BlockSpec((None, 2, WIDTH), lambda b, i: (b, 0, 0)),
                   row(WIDTH)],
        out_shape=[jax.ShapeDtypeStruct((m, WIDTH), BF16), jax.ShapeDtypeStruct((m, WIDTH), BF16),
                   jax.ShapeDtypeStruct((batch, HEADS, HEAD_DIM, 2 * HEAD_DIM), F32),
                   jax.ShapeDtypeStruct((batch, 1, LANES), F32),
                   jax.ShapeDtypeStruct((batch, 2, WIDTH), F32),
                   jax.ShapeDtypeStruct((m, WIDTH), BF16)],
        scratch_shapes=[pltpu.VMEM((blk, WIDTH), F32), pltpu.VMEM((HEADS, blk, 1), F32)],
        compiler_params=pltpu.CompilerParams(dimension_semantics=("parallel", "arbitrary"),
                                             vmem_limit_bytes=VMEM_LIMIT),
        name="mixers",
    )(bsb, pm, pc, gi, gf, bi, bf, cw, sq, kvb)


def _decode_seq(layer, bsb_ref, pm_ref, pc_ref, sq_ref, gi_ref, gf_ref, c0_ref, n0_ref, m0_ref, cv0_ref, bi_ref,
                bf_ref, cw_ref, k_pages, v_pages, ym_ref, yc_ref, ys_ref, c_slots, n_ref, m_ref, cv_ref):
    n_pages = len(k_pages)
    cw = cw_ref[...]
    pw = k_pages[0].shape[0]

    cb = pc_ref[0, :, 0:WIDTH].astype(F32)
    xc = pc_ref[0, :, WIDTH:2 * WIDTH].astype(F32) * pc_ref[0, :, 2 * WIDTH:3 * WIDTH].astype(F32)
    prev = cv0_ref[0]
    yc_ref[0] = (cb * (cw[0:1, :] * prev[0:1, :] + cw[1:2, :] * prev[1:2, :] + cw[2:3, :] * xc)).astype(BF16)
    cv_ref[0, 0:1, :] = prev[1:2, :]
    cv_ref[0, 1:2, :] = xc

    ig = gi_ref[0] + bi_ref[...]
    logf = -_softplus(-(gf_ref[0] + bf_ref[...]))
    m0 = m0_ref[0]
    m_t = jnp.maximum(logf + m0, ig)
    w_in = jnp.exp(ig - m_t)
    s_in = jnp.exp(logf + m0 - m_t)
    floor = jnp.exp(-m_t)
    m_ref[0] = m_t
    row128 = lax.broadcasted_iota(jnp.int32, (HEAD_DIM, HEAD_DIM), 0)
    for h in range(HEADS):
        sl = slice(h * HEAD_DIM, (h + 1) * HEAD_DIM)
        q = pm_ref[0, :, sl].astype(F32)
        k = pm_ref[0, :, WIDTH + h * HEAD_DIM:WIDTH + (h + 1) * HEAD_DIM].astype(F32)
        v = pm_ref[0, :, 2 * WIDTH + h * HEAD_DIM:2 * WIDTH + (h + 1) * HEAD_DIM].astype(F32)
        o = pm_ref[0, :, 3 * WIDTH + h * HEAD_DIM:3 * WIDTH + (h + 1) * HEAD_DIM].astype(F32)
        qk_rows = jnp.where(row128 == 0, q, jnp.where(row128 == 1, k, 0.0))
        cols = qk_rows.T
        q_col = cols[:, 0:1]
        k_col = cols[:, 1:2]
        c0 = c0_ref[0, h]
        n0 = n0_ref[0, h:h + 1, :]
        sc = s_in[:, h:h + 1]
        wi = w_in[:, h:h + 1]
        w = wi * jnp.sum(q * k, axis=1, keepdims=True)
        num = sc * jnp.sum(q_col * c0, axis=0, keepdims=True) + w * v
        den = sc * jnp.sum(q * n0, axis=1, keepdims=True) + w
        hc = num * (1.0 / jnp.maximum(jnp.abs(den), floor[:, h:h + 1]))
        ym_ref[0, :, sl] = (jax.nn.sigmoid(o) * hc).astype(BF16)
        c_new = sc * c0 + wi * (k_col * v)
        for slot in c_slots:
            slot[0, h] = c_new
        n_ref[0, h:h + 1, :] = sc * n0 + wi * k

    r8 = lax.broadcasted_iota(jnp.int32, (SUBLANES, HEAD_DIM), 0)
    rcol = lax.broadcasted_iota(jnp.int32, (SUBLANES, 1), 0)
    sq = sq_ref[0].astype(F32)
    q_rows = jnp.zeros((SUBLANES, HEAD_DIM), F32)
    bias = jnp.zeros((SUBLANES, 1), F32)
    for h in range(HEADS):
        q_rows = jnp.where(r8 == h, sq[:, h * HEAD_DIM:(h + 1) * HEAD_DIM], q_rows)
        bias = jnp.where(rcol == h, bsb_ref[layer, h] * LOG2E, bias)
    q_rows = q_rows.astype(BF16)
    own = (lax.broadcasted_iota(jnp.int32, (SUBLANES, pw), 1) % HEADS
           == lax.broadcasted_iota(jnp.int32, (SUBLANES, pw), 0))
    r_i = lax.broadcasted_iota(jnp.int32, (pw, pw), 0)
    c_i = lax.broadcasted_iota(jnp.int32, (pw, pw), 1)
    after = jnp.logical_and(r_i // HEADS > c_i // HEADS, r_i % HEADS == c_i % HEADS).astype(BF16)
    k_all = jnp.concatenate([kp[...].astype(BF16) for kp in k_pages], axis=0)
    z = lax.dot_general(q_rows, k_all, (((1,), (1,)), ((), ())), preferred_element_type=F32) + bias
    drop, log_beta = _stick_terms(z)
    own_all = jnp.concatenate([own] * n_pages, axis=1)
    drop = jnp.where(own_all, drop, 0.0)
    drop_rows = jnp.concatenate([drop[:, j * pw:(j + 1) * pw] for j in range(n_pages)], axis=0)
    within = jnp.dot(drop_rows.astype(BF16), after, preferred_element_type=F32)
    totals = jnp.sum(drop_rows, axis=1, keepdims=True)
    run = jnp.zeros((SUBLANES, 1), F32)
    laters = [None] * n_pages
    for j in reversed(range(n_pages)):
        rows = slice(SUBLANES * j, SUBLANES * (j + 1))
        laters[j] = within[rows, :] + run
        run = run + totals[rows, :]
    later = jnp.concatenate(laters, axis=1)
    a = jnp.where(own_all, jnp.exp2(log_beta - later), 0.0)
    v_all = jnp.concatenate([vp[...].astype(BF16) for vp in v_pages], axis=0)
    out = jnp.dot(a.astype(BF16), v_all, preferred_element_type=F32)
    for h in range(HEADS):
        ys_ref[0, :, h * HEAD_DIM:(h + 1) * HEAD_DIM] = out[h:h + 1, :].astype(BF16)


_POST_PARAM_SHAPES = ((3, WIDTH, D_MODEL), (D_MODEL, D_MODEL), (1, D_MODEL), (D_MODEL, FFN_DIM), (FFN_DIM, D_MODEL),
                      (1, D_MODEL), (D_MODEL, D_MODEL), (PLE_DIM, D_MODEL), (1, D_MODEL))


def _post_kernel(final, ym_ref, yc_ref, ys_ref, gt_ref, h_ref, p_ref, wb_ref, wo_ref, gmlp_ref, wup_ref,
                 wdn_ref, gple_ref, wpg_ref, wple_ref, gfin_ref, o_ref):
    mix = None
    for n, y_ref in enumerate((ym_ref, yc_ref, ys_ref)):
        pb = jnp.dot(y_ref[...], wb_ref[n], preferred_element_type=F32)
        gate = jax.nn.sigmoid(gt_ref[:, n * D_MODEL:(n + 1) * D_MODEL].astype(F32))
        mix = gate * pb if mix is None else mix + gate * pb
    h = h_ref[...] + jnp.dot(mix.astype(BF16), wo_ref[...], preferred_element_type=F32)

    xn = _rms(h, gmlp_ref[...]).astype(BF16)
    for c in range(FFN_DIM // FFN_CHUNK):
        u = jnp.dot(xn, wup_ref[:, c * FFN_CHUNK:(c + 1) * FFN_CHUNK], preferred_element_type=F32)
        r = jnp.square(jnp.maximum(u, 0.0)).astype(BF16)
        h = h + jnp.dot(r, wdn_ref[c * FFN_CHUNK:(c + 1) * FFN_CHUNK, :], preferred_element_type=F32)

    gate = jax.nn.sigmoid(jnp.dot(_rms(h, gple_ref[...]).astype(BF16), wpg_ref[...],
                                  preferred_element_type=F32))
    h = h + jnp.dot(p_ref[...].astype(BF16), wple_ref[...], preferred_element_type=F32) * gate
    o_ref[...] = _rms(h, gfin_ref[...]) if final else h


def _post(final, layer, ym, yc, ys, gt, h, p, wb, wo, gmlp, wup, wdn, gple, wpg, wple, gfin, tm):
    m = h.shape[0]
    row = lambda width: pl.BlockSpec((tm, width), lambda i: (i, 0))
    p_spec = pl.BlockSpec((None, tm, PLE_DIM), lambda i: (layer, i, 0))
    return pl.pallas_call(
        functools.partial(_post_kernel, final),
        grid=(m // tm,),
        in_specs=[row(WIDTH), row(WIDTH), row(WIDTH), row(3 * D_MODEL), row(D_MODEL), p_spec,
                  *[_param_spec(w, shape, layer) for w, shape in zip(
                      (wb, wo, gmlp, wup, wdn, gple, wpg, wple, gfin), _POST_PARAM_SHAPES)]],
        out_specs=row(D_MODEL),
        out_shape=jax.ShapeDtypeStruct((m, D_MODEL), F32),
        compiler_params=pltpu.CompilerParams(dimension_semantics=("parallel",),
                                             vmem_limit_bytes=VMEM_LIMIT),
        name="post",
    )(ym, yc, ys, gt, h, p, wb, wo, gmlp, wup, wdn, gple, wpg, wple, gfin)


N_POST_IN = 15
N_DECODE_IN = 12


def _post_decode_kernel(final, layer, n_pages, first_layer, pt_ref, bsb_ref, *refs):
    (ym_ref, yc_ref, ys_ref, gt_ref, h_ref, p_ref, wb_ref, wo_ref, gmlp_ref, wup_ref, wdn_ref, gple_ref,
     wpg_ref, wple_ref, gfin_ref) = refs[:N_POST_IN]
    dec_in = refs[N_POST_IN:N_POST_IN + N_DECODE_IN]
    ck_hbm, cv_hbm = refs[N_POST_IN + N_DECODE_IN:N_POST_IN + N_DECODE_IN + 2]
    (o_ref, ym_s_ref, yc_s_ref, ys_s_ref, c_ref, n_ref, m_ref, cv_ref,
     acc_ref, xn_ref, kbuf, vbuf, sem) = refs[-13:]
    t = pl.program_id(0)
    slot = t % 2

    def page_copies(seq, buf):
        copies = []
        for j in range(n_pages):
            pid = pt_ref[seq, j]
            copies.append(pltpu.make_async_copy(ck_hbm.at[layer, pid], kbuf.at[buf, j], sem.at[buf]))
            copies.append(pltpu.make_async_copy(cv_hbm.at[layer, pid], vbuf.at[buf, j], sem.at[buf]))
        return copies

    @pl.when(t == 0)
    def _():
        for n, cp in enumerate(page_copies(0, 0)):
            cp.start(priority=n % 2)

    @pl.when(t + 1 < pl.num_programs(0))
    def _():
        for n, cp in enumerate(page_copies(t + 1, 1 - slot)):
            cp.start(priority=n % 2)

    for cp in page_copies(t, slot):
        cp.wait()
    k_pages = [kbuf.at[slot, j] for j in range(n_pages)]
    v_pages = [vbuf.at[slot, j] for j in range(n_pages)]
    c_slots = [c_ref.at[l] for l in range(c_ref.shape[0])] if first_layer else [c_ref]
    half = FFN_DIM // 2

    def decode():
        _decode_seq(layer, bsb_ref, *dec_in, k_pages, v_pages, ym_s_ref, yc_s_ref, ys_s_ref, c_slots, n_ref, m_ref,
                    cv_ref)

    def mlp_half(xn, h, base):
        for c in range(half // FFN_CHUNK):
            lo = base + c * FFN_CHUNK
            u = jnp.dot(xn, wup_ref[:, lo:lo + FFN_CHUNK], preferred_element_type=F32)
            r = jnp.square(jnp.maximum(u, 0.0)).astype(BF16)
            h = h + jnp.dot(r, wdn_ref[lo:lo + FFN_CHUNK, :], preferred_element_type=F32)
        return h

    parity = pl.program_id(0) % 2

    @pl.when(parity == 0)
    def _():
        decode()
        mix = None
        for n, y_ref in enumerate((ym_ref, yc_ref, ys_ref)):
            pb = jnp.dot(y_ref[...], wb_ref[n], preferred_element_type=F32)
            gate = jax.nn.sigmoid(gt_ref[:, n * D_MODEL:(n + 1) * D_MODEL].astype(F32))
            mix = gate * pb if mix is None else mix + gate * pb
        h = h_ref[...] + jnp.dot(mix.astype(BF16), wo_ref[...], preferred_element_type=F32)
        xn = _rms(h, gmlp_ref[...]).astype(BF16)
        xn_ref[...] = xn
        acc_ref[...] = mlp_half(xn, h, 0)

    @pl.when(parity == 1)
    def _():
        decode()
        h = mlp_half(xn_ref[...], acc_ref[...], half)
        gate = jax.nn.sigmoid(jnp.dot(_rms(h, gple_ref[...]).astype(BF16), wpg_ref[...],
                                      preferred_element_type=F32))
        h = h + jnp.dot(p_ref[...].astype(BF16), wple_ref[...], preferred_element_type=F32) * gate
        o_ref[...] = _rms(h, gfin_ref[...]) if final else h


def _post_decode(final, layer, depth, ym, yc, ys, gt, h, p, post_w, pm, pc, sq, gi, gf, state_c, state_n, m0,
                 state_conv, bi, bf, cw, bsb, page_table, cache_k, cache_v, c_prev):
    nseq, n_pages = page_table.shape
    pw = cache_k.shape[2]
    m = h.shape[0]
    tm = 2 * m // nseq
    assert tm * nseq == 2 * m and tm % 16 == 0
    first = c_prev is None
    prow = lambda width: pl.BlockSpec((tm, width), lambda t, pt: (t // 2, 0))
    fixed = lambda shape: _resident(shape, layer)
    srow = lambda width: pl.BlockSpec((1, 1, width), lambda t, pt: (t, 0, 0))
    state4 = pl.BlockSpec((None, 1, HEADS, HEAD_DIM, HEAD_DIM), lambda t, pt: (layer, t, 0, 0, 0))
    state3 = pl.BlockSpec((None, 1, HEADS, HEAD_DIM), lambda t, pt: (layer, t, 0, 0))
    conv3 = pl.BlockSpec((None, 1, 2, WIDTH), lambda t, pt: (layer, t, 0, 0))

    post_specs = [prow(WIDTH), prow(WIDTH), prow(WIDTH), prow(3 * D_MODEL), prow(D_MODEL),
                  pl.BlockSpec((None, tm, PLE_DIM), lambda t, pt: (layer, t // 2, 0)),
                  *[_param_spec(w, shape, layer) for w, shape in zip(post_w, _POST_PARAM_SHAPES)]]
    dec_specs = [srow(4 * WIDTH), srow(3 * WIDTH), srow(WIDTH), srow(LANES), srow(LANES), state4, state3, srow(LANES),
                 conv3, fixed((1, LANES)), fixed((1, LANES)), fixed((3, WIDTH))]
    assert len(post_specs) == N_POST_IN and len(dec_specs) == N_DECODE_IN
    in_specs = ([pl.BlockSpec(memory_space=pltpu.SMEM)] + post_specs + dec_specs
                + [pl.BlockSpec(memory_space=pl.ANY)] * 2)
    args = ([page_table, bsb, ym, yc, ys, gt, h, p, *post_w]
            + [a.reshape(nseq, 1, a.shape[-1]) for a in (pm, pc, sq, gi, gf)]
            + [state_c, state_n, m0, state_conv, bi, bf, cw, cache_k, cache_v])
    if first:
        c_spec = pl.BlockSpec((depth, 1, HEADS, HEAD_DIM, HEAD_DIM), lambda t, pt: (0, t, 0, 0, 0))
        aliases = {}
    else:
        c_spec = pl.BlockSpec((None, 1, HEADS, HEAD_DIM, HEAD_DIM), lambda t, pt: (layer, t, 0, 0, 0))
        in_specs.append(pl.BlockSpec(memory_space=pl.ANY))
        args.append(c_prev)
        aliases = {len(args) - 1: 4}
    out_specs = [prow(D_MODEL), srow(WIDTH), srow(WIDTH), srow(WIDTH), c_spec,
                 pl.BlockSpec((1, HEADS, HEAD_DIM), lambda t, pt: (t, 0, 0)),
                 srow(LANES),
                 pl.BlockSpec((1, 2, WIDTH), lambda t, pt: (t, 0, 0))]
    out_shape = [jax.ShapeDtypeStruct((m, D_MODEL), F32)] + [jax.ShapeDtypeStruct((nseq, 1, WIDTH), BF16)] * 3 + [
        jax.ShapeDtypeStruct((depth, nseq, HEADS, HEAD_DIM, HEAD_DIM), F32),
        jax.ShapeDtypeStruct((nseq, HEADS, HEAD_DIM), F32),
        jax.ShapeDtypeStruct((nseq, 1, LANES), F32),
        jax.ShapeDtypeStruct((nseq, 2, WIDTH), F32)]
    return pl.pallas_call(
        functools.partial(_post_decode_kernel, final, layer, n_pages, first),
        grid_spec=pltpu.PrefetchScalarGridSpec(
            num_scalar_prefetch=1, grid=(nseq,), in_specs=in_specs, out_specs=out_specs,
            scratch_shapes=[pltpu.VMEM((tm, D_MODEL), F32), pltpu.VMEM((tm, D_MODEL), BF16),
                            pltpu.VMEM((2, n_pages, pw, HEAD_DIM), F32), pltpu.VMEM((2, n_pages, pw, HEAD_DIM), F32),
                            pltpu.SemaphoreType.DMA((2,))]),
        out_shape=out_shape,
        input_output_aliases=aliases,
        compiler_params=pltpu.CompilerParams(dimension_semantics=("arbitrary",),
                                             vmem_limit_bytes=VMEM_LIMIT),
        name="post_decode",
    )(*args)


def kernel(x_prompt, x_sample, cache_k, cache_v, state_mlstm_C, state_mlstm_n, state_mlstm_m, state_conv,
           page_table, p_prompt, p_sample, g_mix, w_in, b_if, b_sb, conv_w, w_branch, w_out, g_mlp, w_up,
           w_down, g_ple, w_ple_gate, w_ple, g_final):
    depth = w_in.shape[0]
    batch, seq, _ = x_prompt.shape
    nseq = x_sample.shape[0]
    n_phys, page = cache_k.shape[1], cache_k.shape[2]
    cache_k = cache_k.reshape(depth, n_phys, page * HEADS, HEAD_DIM)
    cache_v = cache_v.reshape(depth, n_phys, page * HEADS, HEAD_DIM)
    hp = x_prompt.reshape(batch * seq, D_MODEL)
    hs = x_sample.reshape(nseq, D_MODEL)
    pp = p_prompt.reshape(depth, batch * seq, PLE_DIM)
    ps = p_sample.reshape(depth, nseq, PLE_DIM)
    gfin = g_final.reshape(1, D_MODEL)
    tm_p = 512
    tm_s = nseq

    vec = lambda g: g.reshape(depth, 1, D_MODEL)
    w_t = jnp.swapaxes(w_in, 1, 2)
    gates0 = N_HEAD_COLS
    wh = w_t[:, :gates0].astype(BF16)
    wt = w_t[:, gates0 + 2 * HEADS:].astype(BF16)
    zpad = jnp.zeros((depth, LANES - HEADS, D_MODEL), w_in.dtype)
    wif = jnp.concatenate([w_t[:, gates0:gates0 + HEADS], zpad,
                           w_t[:, gates0 + HEADS:gates0 + 2 * HEADS], zpad], axis=1).astype(BF16)
    gmix = vec(g_mix)
    bi = jnp.pad(b_if[:, :HEADS], ((0, 0), (0, LANES - HEADS))).reshape(depth, 1, LANES)
    bf = jnp.pad(b_if[:, HEADS:], ((0, 0), (0, LANES - HEADS))).reshape(depth, 1, LANES)
    post_f32 = (w_branch.reshape(depth, 3 * WIDTH, D_MODEL), w_out, w_up, w_down, w_ple_gate)
    wple = w_ple.astype(BF16)
    m0_all = jnp.pad(state_mlstm_m, ((0, 0), (0, 0), (0, LANES - HEADS)))

    outs_p, outs_s = [], []
    kv_p = kv_s = c_s = None
    for li in range(depth):
        final = li == depth - 1

        pm, pc, sq, kf, vf, kvb, gt, gi, gf, wb, wo, wup, wdn, wpg = _proj(
            hp, gmix, wh, wt, wif, tm_p, li, depth, kv_p, post_f32)
        kv_p = (kf, vf)
        post_w = (wb.reshape(3, WIDTH, D_MODEL), wo, vec(g_mlp), wup, wdn, vec(g_ple), wpg, wple, gfin)
        ym, yc, s_p, m_p, cv_p, ys = _mixers(li, pm, pc, gi, gf, bi, bf, conv_w, sq, kvb, b_sb, batch, seq)
        outs_p.append((s_p[..., :HEAD_DIM], s_p[..., HEAD_DIM], m_p[:, 0, :HEADS], cv_p))
        pm_s, pc_s, sq_s, kf, vf, _, gt_s, gi_s, gf_s = _proj(hs, gmix, wh, wt, wif, tm_s, li, depth, kv_s,
                                                              act_dtype=F32)
        kv_s = (kf, vf)
        m0 = m0_all[li].reshape(nseq, 1, LANES)

        hp, ym_s, yc_s, ys_s, c_s, n_s, m_s, cv_s = _post_decode(
            final, li, depth, ym, yc, ys, gt, hp, pp, post_w, pm_s, pc_s, sq_s, gi_s, gf_s, state_mlstm_C,
            state_mlstm_n, m0, state_conv, bi, bf, conv_w, b_sb, page_table, cache_k, cache_v, c_s)
        hs = _post(final, li, ym_s.reshape(nseq, WIDTH), yc_s.reshape(nseq, WIDTH), ys_s.reshape(nseq, WIDTH),
                   gt_s, hs, ps, *post_w, tm_s)
        outs_s.append((n_s, m_s[:, 0, :HEADS], cv_s))

    stack = lambda outs, j: jnp.stack([o[j] for o in outs])
    k_p, v_p = (a.reshape(depth, batch, seq, HEADS, HEAD_DIM) for a in kv_p)
    k_s, v_s = (a.reshape(depth, nseq, 1, HEADS, HEAD_DIM) for a in kv_s)
    return ((hp.reshape(batch, seq, D_MODEL), hs.reshape(nseq, 1, D_MODEL), k_p, v_p)
            + tuple(stack(outs_p, j) for j in range(4)) + (k_s, v_s, c_s)
            + tuple(stack(outs_s, j) for j in range(3)))
```
